```python
import math
import jax
import jax.numpy as jnp
from jax import lax
import numpy as np

D_MODEL = 4096
BATCH = 4
SEQ = 2048
DEPTH = 2
DEC_BATCH = 8
DEC_SEQ = 1
PAST_LEN = 16384
PAGE_SIZE = 128

N_EVEN = (DEPTH + 1) // 2
N_ODD = DEPTH // 2

HEAD_DIM = 128
A_HEADS = 8
DILATIONS = ((128, 1), (512, 4), (2048, 16))
N_DIL = len(DILATIONS)
A_WIDTH = A_HEADS * HEAD_DIM
A_QKV = N_DIL * 3 * A_HEADS * HEAD_DIM
ATTN_SCALE = HEAD_DIM ** -0.5
QB = 128

B_WIDTH = D_MODEL // 2
B_BLOCKS = 8
B_BLK = B_WIDTH // B_BLOCKS
B_CONV = 4
LRU_C = 8.0

C_CONV = 3

IN0 = A_QKV + 2 * B_WIDTH
OUT0 = A_WIDTH + B_WIDTH

N_EXPERTS = 16
N_GROUPS = 4
EXP_PER_GROUP = N_EXPERTS // N_GROUPS
TOP_K = 2
D_FF = 1152

EPS = 1e-6

kernel_name = 'hybrid_dilated_rglru_shortconv_moe_step'


def rms(x, g):
    x32 = x.astype(jnp.float32)
    return x32 * lax.rsqrt(jnp.mean(x32 * x32, axis=-1, keepdims=True) + EPS) * g


def alibi_slopes():
    return jnp.asarray(2.0 ** (-8.0 * np.arange(1, A_HEADS + 1) / A_HEADS), dtype=jnp.float32)


def causal_dwconv(u, buf, w):
    k = w.shape[0]
    s = u.shape[1]
    full = jnp.concatenate([buf.astype(u.dtype), u], axis=1)
    y = full[:, 0:s] * w[0]
    for j in range(1, k):
        y = y + full[:, j:j + s] * w[j]
    return y, full[:, s:]


def dilated_prompt(q, k, v, window, dil, slopes):
    bsz, seq, nh, hd = q.shape
    n_sub = seq // dil
    nk = window // dil
    bq = math.gcd(n_sub, QB)
    nb = n_sub // bq

    def split(t):
        return t.reshape(bsz, n_sub, dil, nh, hd).transpose(0, 2, 1, 3, 4)

    qs = split(q).reshape(bsz, dil, nb, bq, nh, hd)
    pad = ((0, 0), (0, 0), (nk, 0), (0, 0), (0, 0))
    kp = jnp.pad(split(k), pad)
    vp = jnp.pad(split(v), pad)
    kidx = jnp.arange(nb)[:, None] * bq + jnp.arange(bq + nk)[None, :]
    kb = kp[:, :, kidx]
    vb = vp[:, :, kidx]
    rel = jnp.arange(bq)[:, None] + nk - jnp.arange(bq + nk)[None, :]
    lk = kidx - nk
    valid = ((rel >= 0) & (rel <= nk))[None] & (lk >= 0)[:, None, :]
    s = jnp.einsum('brnqhd,brnkhd->brnhqk', qs, kb) * ATTN_SCALE
    s = s - slopes[:, None, None] * (dil * rel).astype(jnp.float32)
    s = jnp.where(valid[:, None], s, -jnp.inf)
    m = jnp.max(s, axis=-1, keepdims=True)
    p = jnp.exp(s - m)
    den = jnp.sum(p, axis=-1)
    o = jnp.einsum('brnhqk,brnkhd->brnhqd', p, vb) / den[..., None]
    lse = m[..., 0] + jnp.log(den)
    o = o.transpose(0, 2, 4, 1, 3, 5).reshape(bsz, seq, nh, hd)
    lse = lse.transpose(0, 2, 4, 1, 3).reshape(bsz, seq, nh)
    return o, lse


def dilated_sample(q, k, v, kv_buf, window, dil, slopes):
    nbuf = kv_buf.shape[1]
    ds = q.shape[1]
    kv_all = jnp.concatenate([kv_buf.astype(jnp.float32), jnp.stack([k, v], axis=2)], axis=1)
    nk = window // dil
    dist = jnp.arange(nk + 1) * dil
    idx = nbuf + jnp.arange(ds)[:, None] - dist[None, :]
    valid = idx >= 0
    g = kv_all[:, jnp.maximum(idx, 0)]
    s = jnp.einsum('bqhd,bqkhd->bhqk', q, g[:, :, :, 0]) * ATTN_SCALE
    s = s - slopes[:, None, None] * dist.astype(jnp.float32)
    s = jnp.where(valid, s, -jnp.inf)
    m = jnp.max(s, axis=-1, keepdims=True)
    p = jnp.exp(s - m)
    den = jnp.sum(p, axis=-1)
    o = jnp.einsum('bhqk,bqkhd->bqhd', p, g[:, :, :, 1]) / den.transpose(0, 2, 1)[..., None]
    lse = (m[..., 0] + jnp.log(den)).transpose(0, 2, 1)
    keep = min(window, nbuf + ds)
    return o, lse, kv_all[:, nbuf + ds - keep:]


def rglru(xc, h0, wa, ba, wi, bi, lam):
    bsz, seq, w = xc.shape
    xb = xc.reshape(bsz, seq, B_BLOCKS, B_BLK)
    r = jax.nn.sigmoid(jnp.einsum('bsnk,nkj->bsnj', xb, wa).reshape(bsz, seq, w) + ba)
    i = jax.nn.sigmoid(jnp.einsum('bsnk,nkj->bsnj', xb, wi).reshape(bsz, seq, w) + bi)
    log_a = -LRU_C * r * jax.nn.softplus(-lam.astype(jnp.float32))
    a = jnp.exp(log_a)
    b = jnp.sqrt(-jnp.expm1(2.0 * log_a)) * (i * xc)
    b = b.at[:, 0].add(a[:, 0] * h0)

    def combine(lhs, rhs):
        return (lhs[0] * rhs[0], rhs[0] * lhs[1] + rhs[1])

    _, h = lax.associative_scan(combine, (a, b), axis=1)
    return h


def grouped_moe(h, w_router, b_router, w_gate, w_up, w_down):
    bsz, seq, d = h.shape
    t = h.reshape(bsz * seq, d)
    scores = jax.nn.sigmoid((t @ w_router).astype(jnp.float32))
    sel = scores + b_router.astype(jnp.float32)
    grp_top = lax.top_k(sel.reshape(-1, N_GROUPS, EXP_PER_GROUP), 2)[0]
    group = jnp.argmax(jnp.sum(grp_top, axis=-1), axis=-1)
    in_group = (jnp.arange(N_EXPERTS) // EXP_PER_GROUP)[None, :] == group[:, None]
    _, top_i = lax.top_k(jnp.where(in_group, sel, -jnp.inf), TOP_K)
    top_w = jnp.take_along_axis(scores, top_i, axis=-1)
    top_w = top_w / jnp.sum(top_w, axis=-1, keepdims=True)
    comb = jnp.sum(jax.nn.one_hot(top_i, N_EXPERTS, dtype=jnp.float32) * top_w[..., None], axis=1)
    y = jnp.zeros_like(t)
    for e in range(N_EXPERTS):
        he = jax.nn.silu(t @ w_gate[e]) * (t @ w_up[e])
        y = y + comb[:, e:e + 1] * (he @ w_down[e])
    return y.reshape(bsz, seq, d)


def trunk(x, c, prompt, kv_bufs, b_h, b_conv, c_conv, W):
    bsz, seq, _ = x.shape
    f32 = jnp.float32
    res = x.astype(f32)
    cs = jax.nn.silu(c.astype(f32))
    slopes = alibi_slopes()
    new_kv = [[], [], []]
    new_bh, new_bconv, new_cconv = [], [], []
    for layer in range(DEPTH):
        mod = cs @ W['w_mod'][layer] + W['b_mod'][layer]
        sh1, sc1, g1, sh2, sc2, g2 = jnp.split(mod[:, None, :], 6, axis=-1)
        h = rms(res, W['norm_g'][layer, 0]) * (1.0 + sc1) + sh1
        if layer % 2 == 0:
            li = layer // 2
            z = h @ W['w_in0'][li]
            qkv = z[..., :A_QKV].reshape(bsz, seq, N_DIL, 3, A_HEADS, HEAD_DIM)
            xb = z[..., A_QKV:A_QKV + B_WIDTH]
            gb = z[..., A_QKV + B_WIDTH:]
            outs, lses = [], []
            for g, (window, dil) in enumerate(DILATIONS):
                q = rms(qkv[:, :, g, 0], W['qk_g'][li, 0, g])
                k = rms(qkv[:, :, g, 1], W['qk_g'][li, 1, g])
                v = qkv[:, :, g, 2]
                if prompt:
                    o, lse = dilated_prompt(q, k, v, window, dil, slopes)
                    kv_new = jnp.stack([k, v], axis=2)[:, seq - min(window, seq):]
                else:
                    o, lse, kv_new = dilated_sample(q, k, v, kv_bufs[g][li], window, dil, slopes)
                outs.append(o)
                lses.append(lse)
                new_kv[g].append(kv_new.astype(x.dtype))
            wts = jax.nn.softmax(jnp.stack(lses, axis=0), axis=0)
            ya = jnp.sum(wts[..., None] * jnp.stack(outs, axis=0), axis=0).reshape(bsz, seq, A_WIDTH)
            conv_in = jnp.zeros((bsz, B_CONV - 1, B_WIDTH), f32) if prompt else b_conv[li]
            xc, conv_buf = causal_dwconv(xb, conv_in, W['b_conv_w'][li])
            xc = xc + W['b_conv_b'][li]
            h0 = jnp.zeros((bsz, B_WIDTH), f32) if prompt else b_h[li].astype(f32)
            hs = rglru(xc, h0, W['b_wa'][li], W['b_ba'][li], W['b_wi'][li], W['b_bi'][li], W['b_lambda'][li])
            yb = hs * jax.nn.gelu(gb)
            out = jnp.concatenate([ya, yb], axis=-1) @ W['w_out0'][li]
            new_bh.append(hs[:, -1].astype(x.dtype))
            new_bconv.append(conv_buf.astype(x.dtype))
        else:
            li = layer // 2
            z = h @ W['w_in1'][li]
            gate_b, gate_c, val = jnp.split(z, 3, axis=-1)
            u = gate_c * val
            conv_in = jnp.zeros((bsz, C_CONV - 1, D_MODEL), f32) if prompt else c_conv[li]
            uc, conv_buf = causal_dwconv(u, conv_in, W['c_conv_w'][li])
            out = (gate_b * uc) @ W['w_out1'][li]
            new_cconv.append(conv_buf.astype(x.dtype))
        res = res + g1 * out
        h = rms(res, W['norm_g'][layer, 1]) * (1.0 + sc2) + sh2
        res = res + g2 * grouped_moe(h, W['w_router'], W['b_router'], W['w_gate'][layer],
                                     W['w_up'][layer], W['w_down'][layer])
    states = (jnp.stack(new_kv[0], 0), jnp.stack(new_kv[1], 0), jnp.stack(new_kv[2], 0),
              jnp.stack(new_bh, 0), jnp.stack(new_bconv, 0), jnp.stack(new_cconv, 0))
    return res.astype(x.dtype), states


def setup_inputs(seed: int = 0) -> dict:
    key = jax.random.key(seed)
    keys = iter(jax.random.split(key, 48))
    f32 = jnp.float32

    def normal(shape, scale=1.0):
        return scale * jax.random.normal(next(keys), shape, f32)

    def gain(shape):
        return 1.0 + normal(shape, 0.05)

    wb = [min(w, PAST_LEN) for w, _ in DILATIONS]
    u = jax.random.uniform(next(keys), (N_EVEN, B_WIDTH), f32, 0.9, 0.999)
    a0 = u ** (1.0 / LRU_C)
    b_lambda = jnp.log(a0) - jnp.log1p(-a0)
    return {
        'x_prompt': normal((BATCH, SEQ, D_MODEL)),
        'x_sample': normal((DEC_BATCH, DEC_SEQ, D_MODEL)),
        'cache_a_kv0': normal((N_EVEN, DEC_BATCH, wb[0], 2, A_HEADS, HEAD_DIM)),
        'cache_a_kv1': normal((N_EVEN, DEC_BATCH, wb[1], 2, A_HEADS, HEAD_DIM)),
        'cache_a_kv2': normal((N_EVEN, DEC_BATCH, wb[2], 2, A_HEADS, HEAD_DIM)),
        'state_b_h': normal((N_EVEN, DEC_BATCH, B_WIDTH), 0.5),
        'state_b_conv': normal((N_EVEN, DEC_BATCH, B_CONV - 1, B_WIDTH)),
        'state_c_conv': normal((N_ODD, DEC_BATCH, C_CONV - 1, D_MODEL)),
        'c_prompt': normal((BATCH, D_MODEL)),
        'c_sample': normal((DEC_BATCH, D_MODEL)),
        'w_mod': normal((DEPTH, D_MODEL, 6 * D_MODEL), 0.5 * D_MODEL ** -0.5),
        'b_mod': normal((DEPTH, 6 * D_MODEL), 0.02),
        'norm_g': gain((DEPTH, 2, D_MODEL)),
        'w_in0': normal((N_EVEN, D_MODEL, IN0), D_MODEL ** -0.5),
        'qk_g': gain((N_EVEN, 2, N_DIL, A_HEADS, HEAD_DIM)),
        'b_conv_w': normal((N_EVEN, B_CONV, B_WIDTH), B_CONV ** -0.5),
        'b_conv_b': normal((N_EVEN, B_WIDTH), 0.02),
        'b_wa': normal((N_EVEN, B_BLOCKS, B_BLK, B_BLK), B_BLK ** -0.5),
        'b_ba': normal((N_EVEN, B_WIDTH), 0.02),
        'b_wi': normal((N_EVEN, B_BLOCKS, B_BLK, B_BLK), B_BLK ** -0.5),
        'b_bi': normal((N_EVEN, B_WIDTH), 0.02),
        'b_lambda': b_lambda,
        'w_out0': normal((N_EVEN, OUT0, D_MODEL), OUT0 ** -0.5),
        'w_in1': normal((N_ODD, D_MODEL, 3 * D_MODEL), D_MODEL ** -0.5),
        'c_conv_w': normal((N_ODD, C_CONV, D_MODEL), C_CONV ** -0.5),
        'w_out1': normal((N_ODD, D_MODEL, D_MODEL), D_MODEL ** -0.5),
        'w_router': normal((D_MODEL, N_EXPERTS), D_MODEL ** -0.5),
        'b_router': normal((N_EXPERTS,), 0.01),
        'w_gate': normal((DEPTH, N_EXPERTS, D_MODEL, D_FF), D_MODEL ** -0.5),
        'w_up': normal((DEPTH, N_EXPERTS, D_MODEL, D_FF), D_MODEL ** -0.5),
        'w_down': normal((DEPTH, N_EXPERTS, D_FF, D_MODEL), D_FF ** -0.5),
    }


def reference(x_prompt, x_sample, cache_a_kv0, cache_a_kv1, cache_a_kv2, state_b_h, state_b_conv,
              state_c_conv, c_prompt, c_sample, w_mod, b_mod, norm_g, w_in0, qk_g, b_conv_w, b_conv_b,
              b_wa, b_ba, b_wi, b_bi, b_lambda, w_out0, w_in1, c_conv_w, w_out1, w_router, b_router,
              w_gate, w_up, w_down):
    W = {'w_mod': w_mod, 'b_mod': b_mod, 'norm_g': norm_g, 'w_in0': w_in0, 'qk_g': qk_g,
         'b_conv_w': b_conv_w, 'b_conv_b': b_conv_b, 'b_wa': b_wa, 'b_ba': b_ba, 'b_wi': b_wi,
         'b_bi': b_bi, 'b_lambda': b_lambda, 'w_out0': w_out0, 'w_in1': w_in1, 'c_conv_w': c_conv_w,
         'w_out1': w_out1, 'w_router': w_router, 'b_router': b_router, 'w_gate': w_gate,
         'w_up': w_up, 'w_down': w_down}
    y_prompt, (p_kv0, p_kv1, p_kv2, p_bh, p_bconv, p_cconv) = trunk(
        x_prompt, c_prompt, True, None, None, None, None, W)
    y_sample, (s_kv0, s_kv1, s_kv2, s_bh, s_bconv, s_cconv) = trunk(
        x_sample, c_sample, False, (cache_a_kv0, cache_a_kv1, cache_a_kv2), state_b_h, state_b_conv,
        state_c_conv, W)
    return (y_prompt, y_sample, p_kv0, p_kv1, p_kv2, p_bh, p_bconv, p_cconv,
            s_kv0, s_kv1, s_kv2, s_bh, s_bconv, s_cconv)
```

```python
import functools

import numpy as np
import jax
import jax.numpy as jnp
from jax import lax
from jax.experimental import pallas as pl
from jax.experimental.pallas import tpu as pltpu

F32 = jnp.float32
BF16 = jnp.bfloat16
I32 = jnp.int32
U32 = jnp.uint32

HEAD_DIM = 128
A_HEADS = 8
DILATIONS = ((128, 1), (512, 4), (2048, 16))
N_DIL = len(DILATIONS)
A_WIDTH = A_HEADS * HEAD_DIM
A_QKV = N_DIL * 3 * A_WIDTH
ATTN_SCALE = HEAD_DIM ** -0.5
QB = 128
B_BLOCKS = 8
B_CONV = 4
C_CONV = 3
LRU_C = 8.0
N_EXPERTS = 16
N_GROUPS = 4
EXP_PER_GROUP = N_EXPERTS // N_GROUPS
EPS = 1e-6
SAMPLE_ROWS = 16
SUBLANES = 8
LANES = 128
NEG_INF = float("-inf")

VMEM_LIMIT = 52 * 1024 * 1024


def _params(sem, vmem=VMEM_LIMIT):
    return pltpu.CompilerParams(dimension_semantics=sem, vmem_limit_bytes=vmem)


def _mod_kernel(c_ref, w_ref, b_ref, o_ref):
    c = c_ref[...]
    cs = (c * jax.nn.sigmoid(c)).astype(BF16)
    o_ref[...] = jnp.dot(cs, w_ref[...].astype(BF16), preferred_element_type=F32) + b_ref[...]


def _mod_call(c_all, w_mod, b_mod):
    depth, d, n = w_mod.shape
    rows = c_all.shape[0]
    tn = 512
    return pl.pallas_call(
        _mod_kernel,
        grid=(depth, n // tn),
        in_specs=[
            pl.BlockSpec((rows, d), lambda l, j: (0, 0)),
            pl.BlockSpec((None, d, tn), lambda l, j: (l, 0, j)),
            pl.BlockSpec((None, 1, tn), lambda l, j: (l, 0, j)),
        ],
        out_specs=pl.BlockSpec((None, rows, tn), lambda l, j: (l, 0, j)),
        out_shape=jax.ShapeDtypeStruct((depth, rows, n), F32),
        compiler_params=_params(("arbitrary", "arbitrary")),
        name="mod_proj",
    )(c_all, w_mod, b_mod.reshape(depth, 1, n))


def _modnorm(x, g, sc, sh):
    h = x * lax.rsqrt(jnp.mean(x * x, axis=-1, keepdims=True) + EPS) * g
    return h * (1.0 + sc) + sh


def _norm_kernel(x_ref, g_ref, sc_ref, sh_ref, o_ref):
    o_ref[...] = _modnorm(x_ref[...], g_ref[...], sc_ref[...], sh_ref[...]).astype(o_ref.dtype)


def _pack_halves(h):
    k = h.shape[1] // 2
    a = lax.bitcast_convert_type(h[:, :k].astype(BF16).astype(F32), U32)
    b = lax.bitcast_convert_type(h[:, k:].astype(BF16).astype(F32), U32)
    return lax.shift_right_logical(a, jnp.uint32(16)) | (b & jnp.uint32(0xFFFF0000))


def _unpack_halves(p):
    a = lax.bitcast_convert_type(lax.shift_left(p, jnp.uint32(16)), F32).astype(BF16)
    b = lax.bitcast_convert_type(p & jnp.uint32(0xFFFF0000), F32).astype(BF16)
    return a, b


def _route(logits_t, bias_ref):
    score = [jax.nn.sigmoid(logits_t[e:e + 1, :]) for e in range(N_EXPERTS)]
    sel = [score[e] + bias_ref[e] for e in range(N_EXPERTS)]
    gsum = []
    for g in range(N_GROUPS):
        a, b, c, d = sel[4 * g:4 * g + 4]
        hi1, lo1 = jnp.maximum(a, b), jnp.minimum(a, b)
        hi2, lo2 = jnp.maximum(c, d), jnp.minimum(c, d)
        top = jnp.maximum(hi1, hi2)
        second = jnp.maximum(jnp.minimum(hi1, hi2), jnp.maximum(lo1, lo2))
        gsum.append(top + second)
    best = gsum[0]
    grp = jnp.zeros_like(best, dtype=I32)
    for g in range(1, N_GROUPS):
        better = gsum[g] > best
        best = jnp.where(better, gsum[g], best)
        grp = jnp.where(better, g, grp)
    v1 = jnp.full_like(best, NEG_INF)
    e1 = jnp.zeros_like(grp)
    for e in range(N_EXPERTS):
        cand = jnp.where(grp == e // EXP_PER_GROUP, sel[e], NEG_INF)
        better = cand > v1
        v1 = jnp.where(better, cand, v1)
        e1 = jnp.where(better, e, e1)
    v2 = jnp.full_like(best, NEG_INF)
    e2 = jnp.zeros_like(grp)
    for e in range(N_EXPERTS):
        cand = jnp.where((grp == e // EXP_PER_GROUP) & (e1 != e), sel[e], NEG_INF)
        better = cand > v2
        v2 = jnp.where(better, cand, v2)
        e2 = jnp.where(better, e, e2)
    s1 = jnp.zeros_like(best)
    s2 = jnp.zeros_like(best)
    for e in range(N_EXPERTS):
        s1 = jnp.where(e1 == e, score[e], s1)
        s2 = jnp.where(e2 == e, score[e], s2)
    tot = s1 + s2
    return e1, e2, s1 / tot, s2 / tot


def _norm_route_kernel(bias_ref, x_ref, g_ref, sc_ref, sh_ref, wr_ref, o_ref, e_ref, w_ref):
    h = _modnorm(x_ref[...], g_ref[...], sc_ref[...], sh_ref[...])
    o_ref[...] = _pack_halves(h)
    nt = (((1,), (1,)), ((), ()))
    logits_t = lax.dot_general(wr_ref[...].astype(BF16), h.astype(BF16), nt, preferred_element_type=F32)
    e1, e2, w1, w2 = _route(logits_t, bias_ref)
    n = e1.shape[1]
    e_ref[...] = jnp.concatenate([e1, e2, jnp.zeros((SUBLANES - 2, n), I32)], axis=0)
    w_ref[...] = jnp.concatenate([w1, w2, jnp.zeros((SUBLANES - 2, n), F32)], axis=0)


def _mod_specs(per_row, tr, d, rows_per_mod):
    if per_row:
        spec = pl.BlockSpec((tr, d), lambda i, *_: (i, 0))
    else:
        spec = pl.BlockSpec((None, 1, d), lambda i, *_: (i * tr // rows_per_mod, 0, 0))
    return spec


def _norm_call(x, g, sc, sh, *, per_row, tr, rows_per_mod=None):
    rows, d = x.shape
    mspec = _mod_specs(per_row, tr, d, rows_per_mod)
    return pl.pallas_call(
        _norm_kernel,
        grid=(rows // tr,),
        in_specs=[pl.BlockSpec((tr, d), lambda i: (i, 0)),
                  pl.BlockSpec((1, d), lambda i: (0, 0)), mspec, mspec],
        out_specs=pl.BlockSpec((tr, d), lambda i: (i, 0)),
        out_shape=jax.ShapeDtypeStruct((rows, d), BF16),
        compiler_params=_params(("arbitrary",)),
        name="modnorm",
    )(x, g.reshape(1, d), sc, sh)


def _norm_route_call(x, g, sc, sh, w_router_t, b_router, *, per_row, tr, rows_per_mod=None):
    rows, d = x.shape
    mspec = _mod_specs(per_row, tr, d, rows_per_mod)
    grid_spec = pltpu.PrefetchScalarGridSpec(
        num_scalar_prefetch=1,
        grid=(rows // tr,),
        in_specs=[pl.BlockSpec((tr, d), lambda i, b: (i, 0)),
                  pl.BlockSpec((1, d), lambda i, b: (0, 0)), mspec, mspec,
                  pl.BlockSpec((N_EXPERTS, d), lambda i, b: (0, 0))],
        out_specs=[pl.BlockSpec((tr, d // 2), lambda i, b: (i, 0)),
                   pl.BlockSpec((SUBLANES, tr), lambda i, b: (0, i)),
                   pl.BlockSpec((SUBLANES, tr), lambda i, b: (0, i))],
    )
    return pl.pallas_call(
        _norm_route_kernel,
        grid_spec=grid_spec,
        out_shape=[jax.ShapeDtypeStruct((rows, d // 2), U32),
                   jax.ShapeDtypeStruct((SUBLANES, rows), I32),
                   jax.ShapeDtypeStruct((SUBLANES, rows), F32)],
        compiler_params=_params(("arbitrary",)),
        name="modnorm_route",
    )(b_router, x, g.reshape(1, d), sc, sh, w_router_t)


def _dense_kernel(*refs, residual):
    if residual:
        (xp_ref, xs_ref, w_ref, rp_ref, gp_ref, rs_ref, gs_ref, op_ref, os_ref, wb_ref) = refs
    else:
        (xp_ref, xs_ref, w_ref, op_ref, os_ref, wb_ref) = refs

    @pl.when(pl.program_id(1) == 0)
    def _():
        wb_ref[...] = w_ref[...].astype(BF16)
        acc_s = jnp.dot(xs_ref[...], wb_ref[...], preferred_element_type=F32)
        if residual:
            acc_s = rs_ref[...] + gs_ref[...] * acc_s
        os_ref[...] = acc_s.astype(os_ref.dtype)

    acc = jnp.dot(xp_ref[...], wb_ref[...], preferred_element_type=F32)
    if residual:
        acc = rp_ref[...] + gp_ref[...] * acc
    op_ref[...] = acc.astype(op_ref.dtype)


def _dense_call(xp, xs, w, li, *, res=None, rows_per_mod=None, out_dtype=F32, tm=1024, tn=512):
    tp, k = xp.shape
    n = w.shape[2]
    rs = xs.shape[0]
    in_specs = [pl.BlockSpec((tm, k), lambda j, i: (i, 0)),
                pl.BlockSpec((rs, k), lambda j, i: (0, 0)),
                pl.BlockSpec((None, k, tn), lambda j, i: (li, 0, j))]
    args = [xp, xs, w]
    if res is not None:
        in_specs += [pl.BlockSpec((tm, tn), lambda j, i: (i, j)),
                     pl.BlockSpec((None, 1, tn), lambda j, i: (i * tm // rows_per_mod, 0, j)),
                     pl.BlockSpec((rs, tn), lambda j, i: (0, j)),
                     pl.BlockSpec((rs, tn), lambda j, i: (0, j))]
        args += list(res)
    return pl.pallas_call(
        functools.partial(_dense_kernel, residual=res is not None),
        grid=(n // tn, tp // tm),
        in_specs=in_specs,
        out_specs=[pl.BlockSpec((tm, tn), lambda j, i: (i, j)),
                   pl.BlockSpec((rs, tn), lambda j, i: (0, j))],
        out_shape=[jax.ShapeDtypeStruct((tp, n), out_dtype), jax.ShapeDtypeStruct((rs, n), out_dtype)],
        scratch_shapes=[pltpu.VMEM((k, tn), BF16)],
        compiler_params=_params(("arbitrary", "arbitrary")),
        name="dense_proj",
    )(*args)


def _rms_rows(x, g):
    return x * lax.rsqrt(jnp.mean(x * x, axis=-1, keepdims=True) + EPS) * g


def _attn_prompt_kernel(q0, k0, v0, q1, k1, v1, q2, k2, v2, gq_ref, gk_ref, slope_ref,
                        ya_ref, kn0, kn1, kn2, qn_s, m_s, l_s, acc_s, *, seq):
    qkv = ((q0, k0, v0, kn0), (q1, k1, v1, kn1), (q2, k2, v2, kn2))
    slope = slope_ref[...]
    row = lax.broadcasted_iota(I32, (QB, 2 * QB), 0)
    col = lax.broadcasted_iota(I32, (QB, 2 * QB), 1)
    rel_prev = row + QB - col
    rel_first = (row - col)[:, :QB]
    nt = (((1,), (1,)), ((), ()))

    for g, (window, dil) in enumerate(DILATIONS):
        q_ref, k_ref, v_ref, kn_ref = qkv[g]
        nk = window // dil
        n_blk = seq // dil // QB
        qn_s[...] = _rms_rows(q_ref[...], gq_ref[g])
        kn_ref[...] = _rms_rows(k_ref[...], gk_ref[g])
        sl = slope[:, :1] * float(dil)
        bias_prev = jnp.where((rel_prev >= 0) & (rel_prev <= nk), -sl * rel_prev.astype(F32), NEG_INF)
        bias_first = jnp.where(rel_first >= 0, -sl * rel_first.astype(F32), NEG_INF)

        def rows(start, size, dil=dil):
            return pl.ds(start, size) if dil == 1 else pl.ds(start, size, stride=dil)

        def block(qstart, kstart, first, g=g, kn_ref=kn_ref, v_ref=v_ref, rows=rows,
                  bias_prev=bias_prev, bias_first=bias_first):
            nkeys = QB if first else 2 * QB
            qb = qn_s[rows(qstart, QB), :].astype(BF16)
            kb = kn_ref[rows(kstart, nkeys), :].astype(BF16)
            vb = v_ref[rows(kstart, nkeys), :].astype(BF16)
            s = lax.dot_general(qb, kb, nt, preferred_element_type=F32) * ATTN_SCALE
            s = s + (bias_first if first else bias_prev)
            m_blk = jnp.max(s, axis=-1, keepdims=True)
            p = jnp.exp(s - m_blk)
            l_blk = jnp.sum(p, axis=-1, keepdims=True)
            o_blk = jnp.dot(p.astype(BF16), vb, preferred_element_type=F32)
            qrows = rows(qstart, QB)
            if g == 0:
                m_s[qrows, :] = jnp.broadcast_to(m_blk, (QB, HEAD_DIM))
                l_s[qrows, :] = jnp.broadcast_to(l_blk, (QB, HEAD_DIM))
                acc_s[qrows, :] = o_blk
            else:
                m_old = m_s[qrows, :]
                m_new = jnp.maximum(m_old, m_blk)
                a_old = jnp.exp(m_old - m_new)
                a_blk = jnp.exp(m_blk - m_new)
                m_s[qrows, :] = m_new
                l_s[qrows, :] = l_s[qrows, :] * a_old + l_blk * a_blk
                acc_s[qrows, :] = acc_s[qrows, :] * a_old + o_blk * a_blk

        if dil == 1:
            block(0, 0, True)

            def body(n, carry, block=block):
                qstart = pl.multiple_of(n * QB, QB)
                block(qstart, pl.multiple_of(qstart - QB, QB), False)
                return carry

            lax.fori_loop(1, n_blk, body, 0)
        else:
            def body(r, carry, block=block, n_blk=n_blk, dil=dil):
                block(r, r, True)
                for n in range(1, n_blk):
                    block(r + n * QB * dil, r + (n - 1) * QB * dil, False)
                return carry

            lax.fori_loop(0, dil, body, 0)

    ya_ref[...] = (acc_s[...] / l_s[...]).astype(ya_ref.dtype)


def _attn_prompt_call(z, gq, gk, slopes, *, batch, seq):
    t = batch * seq

    def col(g, c):
        return lambda b, h: (b, g * 3 * A_HEADS + c * A_HEADS + h)

    in_specs = []
    for g in range(N_DIL):
        for c in range(3):
            in_specs.append(pl.BlockSpec((seq, HEAD_DIM), col(g, c)))
    gspec = pl.BlockSpec((N_DIL, None, 1, HEAD_DIM), lambda b, h: (0, h, 0, 0))
    in_specs += [gspec, gspec, pl.BlockSpec((None, 1, HEAD_DIM), lambda b, h: (h, 0, 0))]
    out_block = pl.BlockSpec((seq, HEAD_DIM), lambda b, h: (b, h))
    outs = pl.pallas_call(
        functools.partial(_attn_prompt_kernel, seq=seq),
        grid=(batch, A_HEADS),
        in_specs=in_specs,
        out_specs=[out_block] * 4,
        out_shape=[jax.ShapeDtypeStruct((t, A_WIDTH), BF16)] + [jax.ShapeDtypeStruct((t, A_WIDTH), F32)] * 3,
        scratch_shapes=[pltpu.VMEM((seq, HEAD_DIM), F32)] * 4,
        compiler_params=_params(("arbitrary", "arbitrary")),
        name="attn_prompt",
    )(*([z] * 9), gq.reshape(N_DIL, A_HEADS, 1, HEAD_DIM), gk.reshape(N_DIL, A_HEADS, 1, HEAD_DIM), slopes)
    return outs[0], outs[1:]


def _round_bf16(x):
    return x.astype(BF16).astype(F32)


def _attn_sample_kernel(z_ref, c0_ref, c1_ref, c2_ref, gq_ref, gk_ref, slope_ref, ya_ref, new_ref):
    caches = (c0_ref, c1_ref, c2_ref)
    slope = slope_ref[...][:, :1]
    nk = DILATIONS[0][0] // DILATIONS[0][1]
    steps = (nk - lax.broadcasted_iota(I32, (nk, 1, 1), 0)).astype(F32)
    m_run = l_run = acc = None
    for g, (window, dil) in enumerate(DILATIONS):
        base = g * 3 * A_HEADS
        q = _rms_rows(z_ref[base:base + A_HEADS, :], gq_ref[g])
        k = _rms_rows(z_ref[base + A_HEADS:base + 2 * A_HEADS, :], gk_ref[g])
        v = z_ref[base + 2 * A_HEADS:base + 3 * A_HEADS, :]
        new_ref[g, 0] = k
        new_ref[g, 1] = v
        qr = _round_bf16(q)
        ck = _round_bf16(caches[g][:, 0])
        cv = _round_bf16(caches[g][:, 1])
        s_c = jnp.sum(ck * qr[None], axis=-1, keepdims=True) * ATTN_SCALE
        s_c = s_c - (slope * float(dil))[None] * steps
        s_n = jnp.sum(_round_bf16(k) * qr, axis=-1, keepdims=True) * ATTN_SCALE
        m_g = jnp.maximum(jnp.max(s_c, axis=0), s_n)
        p_c = jnp.exp(s_c - m_g[None])
        p_n = jnp.exp(s_n - m_g)
        l_g = jnp.sum(p_c, axis=0) + p_n
        o_g = jnp.sum(_round_bf16(p_c) * cv, axis=0) + _round_bf16(p_n) * _round_bf16(v)
        if g == 0:
            m_run, l_run, acc = m_g, l_g, o_g
        else:
            m_new = jnp.maximum(m_run, m_g)
            a_old = jnp.exp(m_run - m_new)
            a_g = jnp.exp(m_g - m_new)
            l_run = l_run * a_old + l_g * a_g
            acc = acc * a_old + o_g * a_g
            m_run = m_new
    ya_ref[...] = acc / l_run


def _attn_sample_call(z_s, caches, gq, gk, slopes8, *, dec_batch):
    n_colblk = z_s.shape[1] // HEAD_DIM
    z3 = z_s.reshape(z_s.shape[0], n_colblk, HEAD_DIM)
    in_specs = [pl.BlockSpec((None, n_colblk, HEAD_DIM), lambda b: (b, 0, 0))]
    args = [z3]
    for g, (window, dil) in enumerate(DILATIONS):
        nk = window // dil
        c = caches[g].reshape(dec_batch, nk, dil, 2, A_HEADS, HEAD_DIM)
        in_specs.append(pl.BlockSpec((None, nk, None, 2, A_HEADS, HEAD_DIM), lambda b: (b, 0, 0, 0, 0, 0)))
        args.append(c)
    full = lambda shape: pl.BlockSpec(shape, lambda b: (0,) * len(shape))
    in_specs += [full((N_DIL, A_HEADS, HEAD_DIM)), full((N_DIL, A_HEADS, HEAD_DIM)), full((A_HEADS, HEAD_DIM))]
    return pl.pallas_call(
        _attn_sample_kernel,
        grid=(dec_batch,),
        in_specs=in_specs,
        out_specs=[pl.BlockSpec((None, A_HEADS, HEAD_DIM), lambda b: (b, 0, 0)),
                   pl.BlockSpec((None, N_DIL, 2, A_HEADS, HEAD_DIM), lambda b: (b, 0, 0, 0, 0))],
        out_shape=[jax.ShapeDtypeStruct((dec_batch, A_HEADS, HEAD_DIM), F32),
                   jax.ShapeDtypeStruct((dec_batch, N_DIL, 2, A_HEADS, HEAD_DIM), F32)],
        compiler_params=_params(("arbitrary",)),
        name="attn_sample",
    )(*args, gq, gk, slopes8)


def _softplus(x):
    return jnp.maximum(x, 0.0) + jnp.log1p(jnp.exp(-jnp.abs(x)))


def _lru_coeffs(xc, wa, ba, wi, bi, lam):
    xcb = xc.astype(BF16)
    r = jax.nn.sigmoid(jnp.dot(xcb, wa.astype(BF16), preferred_element_type=F32) + ba)
    i = jax.nn.sigmoid(jnp.dot(xcb, wi.astype(BF16), preferred_element_type=F32) + bi)
    log_a = -LRU_C * r * _softplus(-lam)
    a = jnp.exp(log_a)
    t = jnp.tanh(log_a)
    one_minus_a2 = -2.0 * t / (1.0 - t)
    return a, jnp.sqrt(one_minus_a2) * (i * xc)


def _lru_prompt_kernel(x_ref, gate_ref, cw_ref, cb_ref, wa_ref, ba_ref, wi_ref, bi_ref, lam_ref,
                       y_ref, hl_ref, pad_a, pad_b, *, seq):
    blk = x_ref.shape[1]
    off = SUBLANES
    x = x_ref[...]
    pad_a[0:off, :] = jnp.zeros((off, blk), F32)
    pad_a[off:off + seq, :] = x
    xc = pad_a[off - 3:off - 3 + seq, :] * cw_ref[0:1, :]
    xc = xc + pad_a[off - 2:off - 2 + seq, :] * cw_ref[1:2, :]
    xc = xc + pad_a[off - 1:off - 1 + seq, :] * cw_ref[2:3, :]
    xc = xc + x * cw_ref[3:4, :] + cb_ref[...]
    a, b = _lru_coeffs(xc, wa_ref[...], ba_ref[...], wi_ref[...], bi_ref[...], lam_ref[...])

    tpos = lax.broadcasted_iota(I32, (seq, blk), 0) & (SUBLANES - 1)
    pad_b[0:off, :] = jnp.zeros((off, blk), F32)
    for o in (1, 2, 4):
        pad_a[off:off + seq, :] = a
        pad_b[off:off + seq, :] = b
        keep = tpos >= o
        a_prev = jnp.where(keep, pad_a[off - o:off - o + seq, :], 1.0)
        b_prev = jnp.where(keep, pad_b[off - o:off - o + seq, :], 0.0)
        b = a * b_prev + b
        a = a * a_prev
    pad_a[off:off + seq, :] = a
    pad_b[off:off + seq, :] = b

    def tile(i, h_prev):
        r0 = pl.multiple_of(off + i * SUBLANES, SUBLANES)
        h = pad_b[pl.ds(r0, SUBLANES), :] + pad_a[pl.ds(r0, SUBLANES), :] * h_prev
        pad_b[pl.ds(r0, SUBLANES), :] = h
        return jnp.broadcast_to(h[SUBLANES - 1:SUBLANES, :], (SUBLANES, blk))

    lax.fori_loop(0, seq // SUBLANES, tile, jnp.zeros((SUBLANES, blk), F32))
    h = pad_b[off:off + seq, :]
    hl_ref[...] = h[seq - SUBLANES:, :]
    y_ref[...] = (h * jax.nn.gelu(gate_ref[...])).astype(y_ref.dtype)


def _lru_prompt_call(z, cw, cb, wa, ba, wi, bi, lam, *, batch, seq):
    t = batch * seq
    nb, blk, _ = wa.shape
    width = nb * blk
    x0 = A_QKV // blk
    g0 = (A_QKV + width) // blk
    vec = lambda v: v.reshape(1, width)
    vspec = pl.BlockSpec((1, blk), lambda b, n: (0, n))
    return pl.pallas_call(
        functools.partial(_lru_prompt_kernel, seq=seq),
        grid=(batch, nb),
        in_specs=[pl.BlockSpec((seq, blk), lambda b, n: (b, x0 + n)),
                  pl.BlockSpec((seq, blk), lambda b, n: (b, g0 + n)),
                  pl.BlockSpec((B_CONV, blk), lambda b, n: (0, n)), vspec,
                  pl.BlockSpec((None, blk, blk), lambda b, n: (n, 0, 0)), vspec,
                  pl.BlockSpec((None, blk, blk), lambda b, n: (n, 0, 0)), vspec, vspec],
        out_specs=[pl.BlockSpec((seq, blk), lambda b, n: (b, n)),
                   pl.BlockSpec((None, SUBLANES, blk), lambda b, n: (b, 0, n))],
        out_shape=[jax.ShapeDtypeStruct((t, width), BF16),
                   jax.ShapeDtypeStruct((batch, SUBLANES, width), F32)],
        scratch_shapes=[pltpu.VMEM((seq + SUBLANES, blk), F32)] * 2,
        compiler_params=_params(("arbitrary", "arbitrary")),
        name="rglru_prompt",
    )(z, z, cw, vec(cb), wa, vec(ba), wi, vec(bi), vec(lam))


def _lru_sample_kernel(x_ref, gate_ref, buf_ref, h0_ref, cw_ref, cb_ref, wa_ref, ba_ref, wi_ref, bi_ref,
                       lam_ref, y_ref, h_ref):
    nb, blk, _ = wa_ref.shape
    for n in range(nb):
        cs = slice(n * blk, (n + 1) * blk)
        x = x_ref[:, cs]
        xc = buf_ref[0, :, cs] * cw_ref[0:1, cs]
        xc = xc + buf_ref[1, :, cs] * cw_ref[1:2, cs]
        xc = xc + buf_ref[2, :, cs] * cw_ref[2:3, cs]
        xc = xc + x * cw_ref[3:4, cs] + cb_ref[:, cs]
        a, b = _lru_coeffs(xc, wa_ref[n], ba_ref[:, cs], wi_ref[n], bi_ref[:, cs], lam_ref[:, cs])
        h = b + a * h0_ref[:, cs]
        h_ref[:, cs] = h
        y_ref[:, cs] = (h * jax.nn.gelu(gate_ref[:, cs])).astype(y_ref.dtype)


def _lru_sample_call(x, gate, buf, h0, cw, cb, wa, ba, wi, bi, lam):
    rows, width = x.shape
    vec = lambda v: v.reshape(1, width)
    return pl.pallas_call(
        _lru_sample_kernel,
        out_shape=[jax.ShapeDtypeStruct((rows, width), BF16), jax.ShapeDtypeStruct((rows, width), F32)],
        compiler_params=pltpu.CompilerParams(vmem_limit_bytes=VMEM_LIMIT),
        name="rglru_sample",
    )(x, gate, buf, h0, cw, vec(cb), wa, vec(ba), wi, vec(bi), vec(lam))


def _gconv_prompt_kernel(b_ref, c_ref, v_ref, cw_ref, m_ref, ut_ref, pad, *, seq):
    blk = b_ref.shape[1]
    off = SUBLANES
    u = c_ref[...] * v_ref[...]
    pad[0:off, :] = jnp.zeros((off, blk), F32)
    pad[off:off + seq, :] = u
    uc = pad[off - 2:off - 2 + seq, :] * cw_ref[0:1, :]
    uc = uc + pad[off - 1:off - 1 + seq, :] * cw_ref[1:2, :]
    uc = uc + u * cw_ref[2:3, :]
    m_ref[...] = (b_ref[...] * uc).astype(m_ref.dtype)
    ut_ref[...] = u[seq - SUBLANES:, :]


def _gconv_prompt_call(z, cw, *, batch, seq, blk=256):
    t = batch * seq
    d = z.shape[1] // 3
    nblk = d // blk
    return pl.pallas_call(
        functools.partial(_gconv_prompt_kernel, seq=seq),
        grid=(batch, nblk),
        in_specs=[pl.BlockSpec((seq, blk), lambda b, n: (b, n)),
                  pl.BlockSpec((seq, blk), lambda b, n: (b, nblk + n)),
                  pl.BlockSpec((seq, blk), lambda b, n: (b, 2 * nblk + n)),
                  pl.BlockSpec((C_CONV, blk), lambda b, n: (0, n))],
        out_specs=[pl.BlockSpec((seq, blk), lambda b, n: (b, n)),
                   pl.BlockSpec((None, SUBLANES, blk), lambda b, n: (b, 0, n))],
        out_shape=[jax.ShapeDtypeStruct((t, d), BF16), jax.ShapeDtypeStruct((batch, SUBLANES, d), F32)],
        scratch_shapes=[pltpu.VMEM((seq + SUBLANES, blk), F32)],
        compiler_params=_params(("arbitrary", "arbitrary")),
        name="gconv_prompt",
    )(z, z, z, cw)


def _gconv_sample_kernel(z_ref, buf_ref, cw_ref, m_ref, u_ref):
    d = m_ref.shape[1]
    u = z_ref[:, d:2 * d] * z_ref[:, 2 * d:3 * d]
    uc = buf_ref[0] * cw_ref[0:1, :] + buf_ref[1] * cw_ref[1:2, :]
    uc = uc + u * cw_ref[2:3, :]
    m_ref[...] = (z_ref[:, 0:d] * uc).astype(m_ref.dtype)
    u_ref[...] = u


def _gconv_sample_call(z_s, buf, cw):
    rows = z_s.shape[0]
    d = z_s.shape[1] // 3
    return pl.pallas_call(
        _gconv_sample_kernel,
        out_shape=[jax.ShapeDtypeStruct((rows, d), BF16), jax.ShapeDtypeStruct((rows, d), F32)],
        name="gconv_sample",
    )(z_s, buf, cw)


MOE_TM = 256
MOE_CHUNK = 512


def _bucket_kernel(pos_p_ref, pos_s_ref, hp_ref, hs_ref, xin_ref, xs_ref, sem):
    del xin_ref
    step = pl.program_id(0)
    n_s = pos_s_ref.shape[1]

    def copy_p(t, k):
        return pltpu.make_async_copy(hp_ref.at[pl.ds(step * MOE_CHUNK + t, 1)],
                                     xs_ref.at[pl.ds(pos_p_ref[k, t], 1)], sem)

    def copy_s(t, k):
        return pltpu.make_async_copy(hs_ref.at[pl.ds(t, 1)], xs_ref.at[pl.ds(pos_s_ref[k, t], 1)], sem)

    def start_p(t, c):
        copy_p(t, 0).start()
        copy_p(t, 1).start()
        return c

    def wait_p(t, c):
        copy_p(t, 0).wait()
        copy_p(t, 1).wait()
        return c

    lax.fori_loop(0, MOE_CHUNK, start_p, 0)

    @pl.when(step == 0)
    def _():
        for t in range(n_s):
            copy_s(t, 0).start()
            copy_s(t, 1).start()
        for t in range(n_s):
            copy_s(t, 0).wait()
            copy_s(t, 1).wait()

    lax.fori_loop(0, MOE_CHUNK, wait_p, 0)


def _bucket_call(pos_p, pos_s, hp, hs, xs_init):
    tp = hp.shape[0]
    n_s = pos_s.shape[1]
    any_spec = pl.BlockSpec(memory_space=pl.ANY)
    return pl.pallas_call(
        _bucket_kernel,
        grid=(tp // MOE_CHUNK,),
        in_specs=[pl.BlockSpec((2, MOE_CHUNK), lambda i: (0, i), memory_space=pltpu.SMEM),
                  pl.BlockSpec((2, n_s), lambda i: (0, 0), memory_space=pltpu.SMEM),
                  any_spec, any_spec, any_spec],
        out_specs=any_spec,
        out_shape=jax.ShapeDtypeStruct(xs_init.shape, xs_init.dtype),
        scratch_shapes=[pltpu.SemaphoreType.DMA],
        input_output_aliases={4: 0},
        compiler_params=_params(("arbitrary",)),
        name="moe_bucket",
    )(pos_p, pos_s, hp, hs, xs_init)


S_XTILE, S_OTILE, S_EXPERT, S_WCHUNK, S_OCHUNK, S_FIRST, S_VALID = range(7)


def _gate_up_kernel(s_ref, x_ref, wg_ref, wu_ref, o_ref, wgb_ref, wub_ref):
    it = pl.program_id(0)

    @pl.when(s_ref[S_FIRST, it] == 1)
    def _():
        wgb_ref[...] = wg_ref[...].astype(BF16)
        wub_ref[...] = wu_ref[...].astype(BF16)

    @pl.when(s_ref[S_VALID, it] == 1)
    def _():
        half = wgb_ref.shape[0] // 2
        xa, xb = _unpack_halves(x_ref[...])
        gate = (jnp.dot(xa, wgb_ref[:half, :], preferred_element_type=F32)
                + jnp.dot(xb, wgb_ref[half:, :], preferred_element_type=F32))
        up = (jnp.dot(xa, wub_ref[:half, :], preferred_element_type=F32)
              + jnp.dot(xb, wub_ref[half:, :], preferred_element_type=F32))
        o_ref[...] = (gate * jax.nn.sigmoid(gate) * up).astype(o_ref.dtype)

    @pl.when(s_ref[S_VALID, it] == 0)
    def _():
        o_ref[...] = jnp.zeros_like(o_ref)


def _gate_up_call(sched, xs, w_gate, w_up, layer, *, tn):
    p = xs.shape[0]
    _, _, d, f = w_gate.shape
    n_items = sched.shape[1]
    wspec = pl.BlockSpec((None, None, d, tn), lambda it, s: (layer, s[S_EXPERT, it], 0, s[S_WCHUNK, it]))
    grid_spec = pltpu.PrefetchScalarGridSpec(
        num_scalar_prefetch=1,
        grid=(n_items,),
        in_specs=[pl.BlockSpec((MOE_TM, d // 2), lambda it, s: (s[S_XTILE, it], 0)), wspec, wspec],
        out_specs=pl.BlockSpec((MOE_TM, tn), lambda it, s: (s[S_OTILE, it], s[S_OCHUNK, it])),
        scratch_shapes=[pltpu.VMEM((d, tn), BF16)] * 2,
    )
    return pl.pallas_call(
        _gate_up_kernel,
        grid_spec=grid_spec,
        out_shape=jax.ShapeDtypeStruct((p, f), BF16),
        compiler_params=_params(("arbitrary",)),
        name="moe_gate_up",
    )(sched, xs, w_gate, w_up)


def _down_kernel(s_ref, x_ref, w_ref, o_ref, wb_ref):
    it = pl.program_id(0)

    @pl.when(s_ref[S_FIRST, it] == 1)
    def _():
        wb_ref[...] = w_ref[...].astype(BF16)

    @pl.when(s_ref[S_VALID, it] == 1)
    def _():
        o_ref[...] = jnp.dot(x_ref[...], wb_ref[...], preferred_element_type=F32)

    @pl.when(s_ref[S_VALID, it] == 0)
    def _():
        o_ref[...] = jnp.zeros_like(o_ref)


def _down_call(sched, he, w_down, layer, *, tn):
    p, f = he.shape
    d = w_down.shape[3]
    n_items = sched.shape[1]
    grid_spec = pltpu.PrefetchScalarGridSpec(
        num_scalar_prefetch=1,
        grid=(n_items,),
        in_specs=[pl.BlockSpec((MOE_TM, f), lambda it, s: (s[S_XTILE, it], 0)),
                  pl.BlockSpec((None, None, f, tn), lambda it, s: (layer, s[S_EXPERT, it], 0, s[S_WCHUNK, it]))],
        out_specs=pl.BlockSpec((MOE_TM, tn), lambda it, s: (s[S_OTILE, it], s[S_OCHUNK, it])),
        scratch_shapes=[pltpu.VMEM((f, tn), BF16)],
    )
    return pl.pallas_call(
        _down_kernel,
        grid_spec=grid_spec,
        out_shape=jax.ShapeDtypeStruct((p, d), F32),
        compiler_params=_params(("arbitrary",)),
        name="moe_down",
    )(sched, he, w_down)


def _combine_kernel(pos_ref, res_ref, gate_ref, w_ref, y_ref, o_ref, buf, sem, *, tr):
    def copy(t, k):
        return pltpu.make_async_copy(y_ref.at[pl.ds(pos_ref[k, t], 1)], buf.at[k, pl.ds(t, 1)], sem)

    def start(t, c):
        copy(t, 0).start()
        copy(t, 1).start()
        return c

    def wait(t, c):
        copy(t, 0).wait()
        copy(t, 1).wait()
        return c

    lax.fori_loop(0, tr, start, 0)
    lax.fori_loop(0, tr, wait, 0)
    w = w_ref[...]
    y = w[:, 0:1] * buf[0] + w[:, 1:2] * buf[1]
    o_ref[...] = res_ref[...] + gate_ref[...] * y


def _combine_call(pos, res, gate, w, y, *, per_row, tr, rows_per_mod=None):
    rows, d = res.shape
    gspec = _mod_specs(per_row, tr, d, rows_per_mod)
    return pl.pallas_call(
        functools.partial(_combine_kernel, tr=tr),
        grid=(rows // tr,),
        in_specs=[pl.BlockSpec((2, tr), lambda i: (0, i), memory_space=pltpu.SMEM),
                  pl.BlockSpec((tr, d), lambda i: (i, 0)), gspec,
                  pl.BlockSpec((tr, 2), lambda i: (i, 0)),
                  pl.BlockSpec(memory_space=pl.ANY)],
        out_specs=pl.BlockSpec((tr, d), lambda i: (i, 0)),
        out_shape=jax.ShapeDtypeStruct((rows, d), F32),
        scratch_shapes=[pltpu.VMEM((2, tr, d), F32), pltpu.SemaphoreType.DMA],
        compiler_params=_params(("arbitrary",)),
        name="moe_combine",
    )(pos, res, gate, w, y)


def _moe_schedule(e_flat, n_chunks):
    a = e_flat.shape[0]
    onehot = (e_flat[:, None] == jnp.arange(N_EXPERTS, dtype=I32)[None, :]).astype(I32)
    csum = jnp.cumsum(onehot, axis=0)
    counts = csum[-1]
    rank = jnp.sum(csum * onehot, axis=1) - 1
    tiles_per_e = (counts + MOE_TM - 1) // MOE_TM
    tile_end = jnp.cumsum(tiles_per_e)
    tile_start = tile_end - tiles_per_e
    pos = tile_start[e_flat] * MOE_TM + rank
    max_tiles = (a + N_EXPERTS * (MOE_TM - 1)) // MOE_TM
    n_items = n_chunks * max_tiles
    item_start = tile_start * n_chunks
    item_end = tile_end * n_chunks
    it = jnp.arange(n_items, dtype=I32)
    ex = jnp.minimum(jnp.searchsorted(item_end, it, side="right").astype(I32), N_EXPERTS - 1)
    valid = (it < item_end[-1]).astype(I32)
    local = it - item_start[ex]
    t_e = jnp.maximum(tiles_per_e[ex], 1)
    chunk = local // t_e
    tile = tile_start[ex] + local % t_e
    first = ((local % t_e) == 0).astype(I32) * valid
    n_valid = item_end[-1]
    last = jnp.maximum(n_valid - 1, 0)
    pick = lambda v: jnp.where(valid == 1, v, v[last])
    q = it - n_valid
    otile = jnp.where(valid == 1, tile, tile_end[-1] + q // n_chunks)
    ochunk = jnp.where(valid == 1, chunk, q % n_chunks)
    sched = jnp.stack([pick(tile), otile, pick(ex), pick(chunk), ochunk, first, valid]).astype(I32)
    return pos, sched, max_tiles * MOE_TM


def _moe(hp, hs, ep, es, w_gate, w_up, w_down, layer):
    tp = hp.shape[0]
    ts = hs.shape[0]
    e_flat = jnp.concatenate([ep[0], es[0], ep[1], es[1]])
    gu_chunks, dn_chunks = 3, 2
    pos, sched_gu, p_rows = _moe_schedule(e_flat, gu_chunks)
    _, sched_dn, _ = _moe_schedule(e_flat, dn_chunks)
    tall = tp + ts
    pos_p = jnp.stack([pos[:tp], pos[tall:tall + tp]])
    pos_s = jnp.stack([pos[tp:tall], pos[tall + tp:]])
    xs = _bucket_call(pos_p, pos_s, hp, hs, jnp.zeros((p_rows, hp.shape[1]), U32))
    he = _gate_up_call(sched_gu, xs, w_gate, w_up, layer, tn=w_gate.shape[3] // gu_chunks)
    y = _down_call(sched_dn, he, w_down, layer, tn=w_down.shape[3] // dn_chunks)
    return y, pos_p, pos_s


def kernel(x_prompt, x_sample, cache_a_kv0, cache_a_kv1, cache_a_kv2, state_b_h, state_b_conv, state_c_conv,
           c_prompt, c_sample, w_mod, b_mod, norm_g, w_in0, qk_g, b_conv_w, b_conv_b, b_wa, b_ba, b_wi, b_bi,
           b_lambda, w_out0, w_in1, c_conv_w, w_out1, w_router, b_router, w_gate, w_up, w_down):
    batch, seq, d = x_prompt.shape
    dec_batch = x_sample.shape[0]
    depth = w_mod.shape[0]
    tp = batch * seq
    width = b_lambda.shape[1]
    pad_s = SAMPLE_ROWS - dec_batch
    caches = (cache_a_kv0, cache_a_kv1, cache_a_kv2)

    def pad_rows(v):
        return jnp.pad(v, ((0, pad_s),) + ((0, 0),) * (v.ndim - 1))

    res_p = x_prompt.reshape(tp, d)
    res_s = pad_rows(x_sample.reshape(dec_batch, d))
    c_all = jnp.concatenate([c_prompt, jnp.zeros((SUBLANES - batch, d), F32), pad_rows(c_sample)], axis=0)
    mod = _mod_call(c_all, w_mod, b_mod)
    slopes = 2.0 ** (-8.0 * np.arange(1, A_HEADS + 1) / A_HEADS)
    slopes8 = jnp.asarray(np.repeat(slopes[:, None], HEAD_DIM, axis=1), dtype=F32)
    w_router_t = w_router.T

    new_kv_p = [[], [], []]
    new_kv_s = [[], [], []]
    bh_p, bconv_p, cconv_p, bh_s, bconv_s, cconv_s = [], [], [], [], [], []

    for layer in range(depth):
        li = layer // 2
        m = mod[layer]
        chunks_p = [m[:batch, k * d:(k + 1) * d].reshape(batch, 1, d) for k in range(6)]
        chunks_s = [m[SUBLANES:, k * d:(k + 1) * d] for k in range(6)]
        sh1p, sc1p, g1p, sh2p, sc2p, g2p = chunks_p
        sh1s, sc1s, g1s, sh2s, sc2s, g2s = chunks_s

        h_p = _norm_call(res_p, norm_g[layer, 0], sc1p, sh1p, per_row=False, tr=256, rows_per_mod=seq)
        h_s = _norm_call(res_s, norm_g[layer, 0], sc1s, sh1s, per_row=True, tr=SAMPLE_ROWS)

        if layer % 2 == 0:
            z_p, z_s = _dense_call(h_p, h_s, w_in0, li)
            ya_p, kn_p = _attn_prompt_call(z_p, qk_g[li, 0], qk_g[li, 1], slopes8.reshape(A_HEADS, 1, HEAD_DIM),
                                           batch=batch, seq=seq)
            for g, (window, _) in enumerate(DILATIONS):
                keep = min(window, seq)
                kk = kn_p[g].reshape(batch, seq, A_HEADS, HEAD_DIM)[:, seq - keep:]
                v0 = g * 3 * A_WIDTH + 2 * A_WIDTH
                vv = z_p[:, v0:v0 + A_WIDTH].reshape(batch, seq, A_HEADS, HEAD_DIM)[:, seq - keep:]
                new_kv_p[g].append(jnp.stack([kk, vv], axis=2))
            yb_p, hl_p = _lru_prompt_call(z_p, b_conv_w[li], b_conv_b[li], b_wa[li], b_ba[li], b_wi[li],
                                          b_bi[li], b_lambda[li], batch=batch, seq=seq)
            bh_p.append(hl_p[:, SUBLANES - 1])
            xb_p = z_p[:, A_QKV:A_QKV + width].reshape(batch, seq, width)
            bconv_p.append(xb_p[:, seq - (B_CONV - 1):])

            ya_s, kvnew_s = _attn_sample_call(z_s, [c[li] for c in caches], qk_g[li, 0], qk_g[li, 1], slopes8,
                                              dec_batch=dec_batch)
            for g in range(N_DIL):
                new_kv_s[g].append(jnp.concatenate([caches[g][li][:, 1:], kvnew_s[:, g][:, None]], axis=1))
            xb_s = z_s[:, A_QKV:A_QKV + width]
            gb_s = z_s[:, A_QKV + width:]
            buf = jnp.transpose(pad_rows(state_b_conv[li]), (1, 0, 2))
            yb_s, hnew_s = _lru_sample_call(xb_s, gb_s, buf, pad_rows(state_b_h[li]), b_conv_w[li], b_conv_b[li],
                                            b_wa[li], b_ba[li], b_wi[li], b_bi[li], b_lambda[li])
            bh_s.append(hnew_s[:dec_batch])
            bconv_s.append(jnp.concatenate([state_b_conv[li][:, 1:], xb_s[:dec_batch, None]], axis=1))

            mix_p = jnp.concatenate([ya_p, yb_p], axis=1)
            mix_s = jnp.concatenate([pad_rows(ya_s.reshape(dec_batch, A_WIDTH)).astype(BF16), yb_s], axis=1)
            w_out = w_out0
        else:
            z_p, z_s = _dense_call(h_p, h_s, w_in1, li)
            mix_p, ut_p = _gconv_prompt_call(z_p, c_conv_w[li], batch=batch, seq=seq)
            cconv_p.append(ut_p[:, SUBLANES - (C_CONV - 1):])
            buf = jnp.transpose(pad_rows(state_c_conv[li]), (1, 0, 2))
            mix_s, u_s = _gconv_sample_call(z_s, buf, c_conv_w[li])
            cconv_s.append(jnp.concatenate([state_c_conv[li][:, 1:], u_s[:dec_batch, None]], axis=1))
            w_out = w_out1

        res_p, res_s = _dense_call(mix_p, mix_s, w_out, li, res=(res_p, g1p, res_s, g1s), rows_per_mod=seq)

        hp, ep, wp = _norm_route_call(res_p, norm_g[layer, 1], sc2p, sh2p, w_router_t, b_router,
                                      per_row=False, tr=256, rows_per_mod=seq)
        hs, es, ws = _norm_route_call(res_s, norm_g[layer, 1], sc2s, sh2s, w_router_t, b_router,
                                      per_row=True, tr=SAMPLE_ROWS)
        y, pos_p, pos_s = _moe(hp, hs, ep[:2], es[:2], w_gate, w_up, w_down, layer)
        res_p = _combine_call(pos_p, res_p, g2p, wp[:2].T, y, per_row=False, tr=256, rows_per_mod=seq)
        res_s = _combine_call(pos_s, res_s, g2s, ws[:2].T, y, per_row=True, tr=SAMPLE_ROWS)

    y_prompt = res_p.reshape(batch, seq, d)
    y_sample = res_s[:dec_batch].reshape(dec_batch, 1, d)
    st = lambda xs: jnp.stack(xs, axis=0)
    return (y_prompt, y_sample, st(new_kv_p[0]), st(new_kv_p[1]), st(new_kv_p[2]), st(bh_p), st(bconv_p),
            st(cconv_p), st(new_kv_s[0]), st(new_kv_s[1]), st(new_kv_s[2]), st(bh_s), st(bconv_s), st(cconv_s))
```

```python
import functools

import numpy as np
import jax
import jax.numpy as jnp
from jax import lax
from jax.experimental import pallas as pl
from jax.experimental.pallas import tpu as pltpu

F32 = jnp.float32
BF16 = jnp.bfloat16
I32 = jnp.int32
U32 = jnp.uint32

HEAD_DIM = 128
A_HEADS = 8
DILATIONS = ((128, 1), (512, 4), (2048, 16))
N_DIL = len(DILATIONS)
A_WIDTH = A_HEADS * HEAD_DIM
A_QKV = N_DIL * 3 * A_WIDTH
ATTN_SCALE = HEAD_DIM ** -0.5
QB = 128
B_BLOCKS = 8
B_CONV = 4
C_CONV = 3
LRU_C = 8.0
N_EXPERTS = 16
N_GROUPS = 4
EXP_PER_GROUP = N_EXPERTS // N_GROUPS
EPS = 1e-6
SAMPLE_ROWS = 16
SUBLANES = 8
LANES = 128
NEG_INF = float("-inf")

VMEM_LIMIT = 52 * 1024 * 1024


def _params(sem, vmem=VMEM_LIMIT):
    return pltpu.CompilerParams(dimension_semantics=sem, vmem_limit_bytes=vmem)


def _mod_kernel(c_ref, w_ref, b_ref, o_ref):
    c = c_ref[...]
    cs = (c * jax.nn.sigmoid(c)).astype(BF16)
    o_ref[...] = jnp.dot(cs, w_ref[...].astype(BF16), preferred_element_type=F32) + b_ref[...]


def _mod_call(c_all, w_mod, b_mod):
    depth, d, n = w_mod.shape
    rows = c_all.shape[0]
    tn = 512
    return pl.pallas_call(
        _mod_kernel,
        grid=(depth, n // tn),
        in_specs=[
            pl.BlockSpec((rows, d), lambda l, j: (0, 0)),
            pl.BlockSpec((None, d, tn), lambda l, j: (l, 0, j)),
            pl.BlockSpec((None, 1, tn), lambda l, j: (l, 0, j)),
        ],
        out_specs=pl.BlockSpec((None, rows, tn), lambda l, j: (l, 0, j)),
        out_shape=jax.ShapeDtypeStruct((depth, rows, n), F32),
        compiler_params=_params(("arbitrary", "arbitrary")),
        name="mod_proj",
    )(c_all, w_mod, b_mod.reshape(depth, 1, n))


def _modnorm(x, g, sc, sh):
    h = x * lax.rsqrt(jnp.mean(x * x, axis=-1, keepdims=True) + EPS) * g
    return h * (1.0 + sc) + sh


def _norm_kernel(x_ref, g_ref, sc_ref, sh_ref, o_ref):
    o_ref[...] = _modnorm(x_ref[...], g_ref[...], sc_ref[...], sh_ref[...]).astype(o_ref.dtype)


def _pack_halves(h):
    k = h.shape[1] // 2
    a = lax.bitcast_convert_type(h[:, :k].astype(BF16).astype(F32), U32)
    b = lax.bitcast_convert_type(h[:, k:].astype(BF16).astype(F32), U32)
    return lax.shift_right_logical(a, jnp.uint32(16)) | (b & jnp.uint32(0xFFFF0000))


def _unpack_halves(p):
    a = lax.bitcast_convert_type(lax.shift_left(p, jnp.uint32(16)), F32).astype(BF16)
    b = lax.bitcast_convert_type(p & jnp.uint32(0xFFFF0000), F32).astype(BF16)
    return a, b


def _route(logits_t, bias_ref):
    score = [jax.nn.sigmoid(logits_t[e:e + 1, :]) for e in range(N_EXPERTS)]
    sel = [score[e] + bias_ref[e] for e in range(N_EXPERTS)]
    gsum = []
    for g in range(N_GROUPS):
        a, b, c, d = sel[4 * g:4 * g + 4]
        hi1, lo1 = jnp.maximum(a, b), jnp.minimum(a, b)
        hi2, lo2 = jnp.maximum(c, d), jnp.minimum(c, d)
        top = jnp.maximum(hi1, hi2)
        second = jnp.maximum(jnp.minimum(hi1, hi2), jnp.maximum(lo1, lo2))
        gsum.append(top + second)
    best = gsum[0]
    grp = jnp.zeros_like(best, dtype=I32)
    for g in range(1, N_GROUPS):
        better = gsum[g] > best
        best = jnp.where(better, gsum[g], best)
        grp = jnp.where(better, g, grp)
    v1 = jnp.full_like(best, NEG_INF)
    e1 = jnp.zeros_like(grp)
    for e in range(N_EXPERTS):
        cand = jnp.where(grp == e // EXP_PER_GROUP, sel[e], NEG_INF)
        better = cand > v1
        v1 = jnp.where(better, cand, v1)
        e1 = jnp.where(better, e, e1)
    v2 = jnp.full_like(best, NEG_INF)
    e2 = jnp.zeros_like(grp)
    for e in range(N_EXPERTS):
        cand = jnp.where((grp == e // EXP_PER_GROUP) & (e1 != e), sel[e], NEG_INF)
        better = cand > v2
        v2 = jnp.where(better, cand, v2)
        e2 = jnp.where(better, e, e2)
    s1 = jnp.zeros_like(best)
    s2 = jnp.zeros_like(best)
    for e in range(N_EXPERTS):
        s1 = jnp.where(e1 == e, score[e], s1)
        s2 = jnp.where(e2 == e, score[e], s2)
    tot = s1 + s2
    return e1, e2, s1 / tot, s2 / tot


def _norm_route_kernel(bias_ref, x_ref, g_ref, sc_ref, sh_ref, wr_ref, cnt0_ref,
                       o_ref, e_ref, w_ref, cnt_ref, run_ref):
    @pl.when(pl.program_id(0) == 0)
    def _():
        run_ref[...] = cnt0_ref[...]

    h = _modnorm(x_ref[...], g_ref[...], sc_ref[...], sh_ref[...])
    o_ref[...] = _pack_halves(h)
    nt = (((1,), (1,)), ((), ()))
    logits_t = lax.dot_general(wr_ref[...].astype(BF16), h.astype(BF16), nt, preferred_element_type=F32)
    e1, e2, w1, w2 = _route(logits_t, bias_ref)
    n = e1.shape[1]
    expert = lax.broadcasted_iota(I32, (N_EXPERTS, n), 0)
    oh1 = (expert == e1).astype(F32)
    oh2 = (expert == e2).astype(F32)
    before = (lax.broadcasted_iota(I32, (n, n), 0) < lax.broadcasted_iota(I32, (n, n), 1)).astype(BF16)
    pre1 = jnp.dot(oh1.astype(BF16), before, preferred_element_type=F32)
    pre2 = jnp.dot(oh2.astype(BF16), before, preferred_element_type=F32)
    tot1 = jnp.sum(oh1, axis=1, keepdims=True)
    tot2 = jnp.sum(oh2, axis=1, keepdims=True)
    base = run_ref[...][:, :1]
    r1 = jnp.sum(oh1 * (base + pre1), axis=0, keepdims=True)
    r2 = jnp.sum(oh2 * (base + tot1 + pre2), axis=0, keepdims=True)
    run_ref[...] = run_ref[...] + (tot1 + tot2)
    cnt_ref[...] = run_ref[...]
    e_ref[...] = jnp.concatenate([e1, e2, r1.astype(I32), r2.astype(I32), jnp.zeros((SUBLANES - 4, n), I32)], axis=0)
    w_ref[...] = jnp.concatenate([w1, w2, jnp.zeros((SUBLANES - 2, n), F32)], axis=0)


def _mod_specs(per_row, tr, d, rows_per_mod):
    if per_row:
        spec = pl.BlockSpec((tr, d), lambda i, *_: (i, 0))
    else:
        spec = pl.BlockSpec((None, 1, d), lambda i, *_: (i * tr // rows_per_mod, 0, 0))
    return spec


def _norm_call(x, g, sc, sh, *, per_row, tr, rows_per_mod=None):
    rows, d = x.shape
    mspec = _mod_specs(per_row, tr, d, rows_per_mod)
    return pl.pallas_call(
        _norm_kernel,
        grid=(rows // tr,),
        in_specs=[pl.BlockSpec((tr, d), lambda i: (i, 0)),
                  pl.BlockSpec((1, d), lambda i: (0, 0)), mspec, mspec],
        out_specs=pl.BlockSpec((tr, d), lambda i: (i, 0)),
        out_shape=jax.ShapeDtypeStruct((rows, d), BF16),
        compiler_params=_params(("arbitrary",)),
        name="modnorm",
    )(x, g.reshape(1, d), sc, sh)


def _norm_route_call(x, g, sc, sh, w_router_t, b_router, cnt0, *, per_row, tr, rows_per_mod=None):
    rows, d = x.shape
    mspec = _mod_specs(per_row, tr, d, rows_per_mod)
    cspec = pl.BlockSpec((N_EXPERTS, LANES), lambda i, b: (0, 0))
    grid_spec = pltpu.PrefetchScalarGridSpec(
        num_scalar_prefetch=1,
        grid=(rows // tr,),
        in_specs=[pl.BlockSpec((tr, d), lambda i, b: (i, 0)),
                  pl.BlockSpec((1, d), lambda i, b: (0, 0)), mspec, mspec,
                  pl.BlockSpec((N_EXPERTS, d), lambda i, b: (0, 0)), cspec],
        out_specs=[pl.BlockSpec((tr, d // 2), lambda i, b: (i, 0)),
                   pl.BlockSpec((SUBLANES, tr), lambda i, b: (0, i)),
                   pl.BlockSpec((SUBLANES, tr), lambda i, b: (0, i)), cspec],
        scratch_shapes=[pltpu.VMEM((N_EXPERTS, LANES), F32)],
    )
    return pl.pallas_call(
        _norm_route_kernel,
        grid_spec=grid_spec,
        out_shape=[jax.ShapeDtypeStruct((rows, d // 2), U32),
                   jax.ShapeDtypeStruct((SUBLANES, rows), I32),
                   jax.ShapeDtypeStruct((SUBLANES, rows), F32),
                   jax.ShapeDtypeStruct((N_EXPERTS, LANES), F32)],
        compiler_params=_params(("arbitrary",)),
        name="modnorm_route",
    )(b_router, x, g.reshape(1, d), sc, sh, w_router_t, cnt0)


def _dense_kernel(*refs, residual):
    if residual:
        (xp_ref, xs_ref, w_ref, rp_ref, gp_ref, rs_ref, gs_ref, op_ref, os_ref, wb_ref) = refs
    else:
        (xp_ref, xs_ref, w_ref, op_ref, os_ref, wb_ref) = refs

    @pl.when(pl.program_id(1) == 0)
    def _():
        wb_ref[...] = w_ref[...].astype(BF16)
        acc_s = jnp.dot(xs_ref[...], wb_ref[...], preferred_element_type=F32)
        if residual:
            acc_s = rs_ref[...] + gs_ref[...] * acc_s
        os_ref[...] = acc_s.astype(os_ref.dtype)

    acc = jnp.dot(xp_ref[...], wb_ref[...], preferred_element_type=F32)
    if residual:
        acc = rp_ref[...] + gp_ref[...] * acc
    op_ref[...] = acc.astype(op_ref.dtype)


def _dense_call(xp, xs, w, li, *, res=None, rows_per_mod=None, out_dtype=F32, tm=1024, tn=512):
    tp, k = xp.shape
    n = w.shape[2]
    rs = xs.shape[0]
    in_specs = [pl.BlockSpec((tm, k), lambda j, i: (i, 0)),
                pl.BlockSpec((rs, k), lambda j, i: (0, 0)),
                pl.BlockSpec((None, k, tn), lambda j, i: (li, 0, j))]
    args = [xp, xs, w]
    if res is not None:
        in_specs += [pl.BlockSpec((tm, tn), lambda j, i: (i, j)),
                     pl.BlockSpec((None, 1, tn), lambda j, i: (i * tm // rows_per_mod, 0, j)),
                     pl.BlockSpec((rs, tn), lambda j, i: (0, j)),
                     pl.BlockSpec((rs, tn), lambda j, i: (0, j))]
        args += list(res)
    return pl.pallas_call(
        functools.partial(_dense_kernel, residual=res is not None),
        grid=(n // tn, tp // tm),
        in_specs=in_specs,
        out_specs=[pl.BlockSpec((tm, tn), lambda j, i: (i, j)),
                   pl.BlockSpec((rs, tn), lambda j, i: (0, j))],
        out_shape=[jax.ShapeDtypeStruct((tp, n), out_dtype), jax.ShapeDtypeStruct((rs, n), out_dtype)],
        scratch_shapes=[pltpu.VMEM((k, tn), BF16)],
        compiler_params=_params(("arbitrary", "arbitrary")),
        name="dense_proj",
    )(*args)


def _rms_rows(x, g):
    return x * lax.rsqrt(jnp.mean(x * x, axis=-1, keepdims=True) + EPS) * g


def _attn_prompt_kernel(q0, k0, v0, q1, k1, v1, q2, k2, v2, gq_ref, gk_ref, slope_ref,
                        ya_ref, kn0, kn1, kn2, qn_s, m_s, l_s, acc_s, *, seq):
    qkv = ((q0, k0, v0, kn0), (q1, k1, v1, kn1), (q2, k2, v2, kn2))
    slope = slope_ref[...]
    row = lax.broadcasted_iota(I32, (QB, 2 * QB), 0)
    col = lax.broadcasted_iota(I32, (QB, 2 * QB), 1)
    rel_prev = row + QB - col
    rel_first = (row - col)[:, :QB]
    nt = (((1,), (1,)), ((), ()))

    for g, (window, dil) in enumerate(DILATIONS):
        q_ref, k_ref, v_ref, kn_ref = qkv[g]
        nk = window // dil
        n_blk = seq // dil // QB
        qn_s[...] = _rms_rows(q_ref[...], gq_ref[g])
        kn_ref[...] = _rms_rows(k_ref[...], gk_ref[g])
        sl = slope[:, :1] * float(dil)
        bias_prev = jnp.where((rel_prev >= 0) & (rel_prev <= nk), -sl * rel_prev.astype(F32), NEG_INF)
        bias_first = jnp.where(rel_first >= 0, -sl * rel_first.astype(F32), NEG_INF)

        def rows(start, size, dil=dil):
            return pl.ds(start, size) if dil == 1 else pl.ds(start, size, stride=dil)

        def block(qstart, kstart, first, g=g, kn_ref=kn_ref, v_ref=v_ref, rows=rows,
                  bias_prev=bias_prev, bias_first=bias_first):
            nkeys = QB if first else 2 * QB
            qb = qn_s[rows(qstart, QB), :].astype(BF16)
            kb = kn_ref[rows(kstart, nkeys), :].astype(BF16)
            vb = v_ref[rows(kstart, nkeys), :].astype(BF16)
            s = lax.dot_general(qb, kb, nt, preferred_element_type=F32) * ATTN_SCALE
            s = s + (bias_first if first else bias_prev)
            m_blk = jnp.max(s, axis=-1, keepdims=True)
            p = jnp.exp(s - m_blk)
            l_blk = jnp.sum(p, axis=-1, keepdims=True)
            o_blk = jnp.dot(p.astype(BF16), vb, preferred_element_type=F32)
            qrows = rows(qstart, QB)
            if g == 0:
                m_s[qrows, :] = jnp.broadcast_to(m_blk, (QB, HEAD_DIM))
                l_s[qrows, :] = jnp.broadcast_to(l_blk, (QB, HEAD_DIM))
                acc_s[qrows, :] = o_blk
            else:
                m_old = m_s[qrows, :]
                m_new = jnp.maximum(m_old, m_blk)
                a_old = jnp.exp(m_old - m_new)
                a_blk = jnp.exp(m_blk - m_new)
                m_s[qrows, :] = m_new
                l_s[qrows, :] = l_s[qrows, :] * a_old + l_blk * a_blk
                acc_s[qrows, :] = acc_s[qrows, :] * a_old + o_blk * a_blk

        if dil == 1:
            block(0, 0, True)

            def body(n, carry, block=block):
                qstart = pl.multiple_of(n * QB, QB)
                block(qstart, pl.multiple_of(qstart - QB, QB), False)
                return carry

            lax.fori_loop(1, n_blk, body, 0)
        else:
            def body(r, carry, block=block, n_blk=n_blk, dil=dil):
                block(r, r, True)
                for n in range(1, n_blk):
                    block(r + n * QB * dil, r + (n - 1) * QB * dil, False)
                return carry

            lax.fori_loop(0, dil, body, 0)

    ya_ref[...] = (acc_s[...] / l_s[...]).astype(ya_ref.dtype)


def _attn_prompt_call(z, gq, gk, slopes, *, batch, seq):
    t = batch * seq

    def col(g, c):
        return lambda b, h: (b, g * 3 * A_HEADS + c * A_HEADS + h)

    in_specs = []
    for g in range(N_DIL):
        for c in range(3):
            in_specs.append(pl.BlockSpec((seq, HEAD_DIM), col(g, c)))
    gspec = pl.BlockSpec((N_DIL, None, 1, HEAD_DIM), lambda b, h: (0, h, 0, 0))
    in_specs += [gspec, gspec, pl.BlockSpec((None, 1, HEAD_DIM), lambda b, h: (h, 0, 0))]
    out_block = pl.BlockSpec((seq, HEAD_DIM), lambda b, h: (b, h))
    outs = pl.pallas_call(
        functools.partial(_attn_prompt_kernel, seq=seq),
        grid=(batch, A_HEADS),
        in_specs=in_specs,
        out_specs=[out_block] * 4,
        out_shape=[jax.ShapeDtypeStruct((t, A_WIDTH), BF16)] + [jax.ShapeDtypeStruct((t, A_WIDTH), F32)] * 3,
        scratch_shapes=[pltpu.VMEM((seq, HEAD_DIM), F32)] * 4,
        compiler_params=_params(("arbitrary", "arbitrary")),
        name="attn_prompt",
    )(*([z] * 9), gq.reshape(N_DIL, A_HEADS, 1, HEAD_DIM), gk.reshape(N_DIL, A_HEADS, 1, HEAD_DIM), slopes)
    return outs[0], outs[1:]


def _round_bf16(x):
    return x.astype(BF16).astype(F32)


def _attn_sample_kernel(z_ref, c0_ref, c1_ref, c2_ref, gq_ref, gk_ref, slope_ref, ya_ref, new_ref):
    caches = (c0_ref, c1_ref, c2_ref)
    slope = slope_ref[...][:, :1]
    nk = DILATIONS[0][0] // DILATIONS[0][1]
    steps = (nk - lax.broadcasted_iota(I32, (nk, 1, 1), 0)).astype(F32)
    m_run = l_run = acc = None
    for g, (window, dil) in enumerate(DILATIONS):
        base = g * 3 * A_HEADS
        q = _rms_rows(z_ref[base:base + A_HEADS, :], gq_ref[g])
        k = _rms_rows(z_ref[base + A_HEADS:base + 2 * A_HEADS, :], gk_ref[g])
        v = z_ref[base + 2 * A_HEADS:base + 3 * A_HEADS, :]
        new_ref[g, 0] = k
        new_ref[g, 1] = v
        qr = _round_bf16(q)
        ck = _round_bf16(caches[g][:, 0])
        cv = _round_bf16(caches[g][:, 1])
        s_c = jnp.sum(ck * qr[None], axis=-1, keepdims=True) * ATTN_SCALE
        s_c = s_c - (slope * float(dil))[None] * steps
        s_n = jnp.sum(_round_bf16(k) * qr, axis=-1, keepdims=True) * ATTN_SCALE
        m_g = jnp.maximum(jnp.max(s_c, axis=0), s_n)
        p_c = jnp.exp(s_c - m_g[None])
        p_n = jnp.exp(s_n - m_g)
        l_g = jnp.sum(p_c, axis=0) + p_n
        o_g = jnp.sum(_round_bf16(p_c) * cv, axis=0) + _round_bf16(p_n) * _round_bf16(v)
        if g == 0:
            m_run, l_run, acc = m_g, l_g, o_g
        else:
            m_new = jnp.maximum(m_run, m_g)
            a_old = jnp.exp(m_run - m_new)
            a_g = jnp.exp(m_g - m_new)
            l_run = l_run * a_old + l_g * a_g
            acc = acc * a_old + o_g * a_g
            m_run = m_new
    ya_ref[...] = acc / l_run


def _attn_sample_call(z_s, caches, gq, gk, slopes8, *, dec_batch):
    n_colblk = z_s.shape[1] // HEAD_DIM
    z3 = z_s.reshape(z_s.shape[0], n_colblk, HEAD_DIM)
    in_specs = [pl.BlockSpec((None, n_colblk, HEAD_DIM), lambda b: (b, 0, 0))]
    args = [z3]
    for g, (window, dil) in enumerate(DILATIONS):
        nk = window // dil
        c = caches[g].reshape(dec_batch, nk, dil, 2, A_HEADS, HEAD_DIM)
        in_specs.append(pl.BlockSpec((None, nk, None, 2, A_HEADS, HEAD_DIM), lambda b: (b, 0, 0, 0, 0, 0)))
        args.append(c)
    full = lambda shape: pl.BlockSpec(shape, lambda b: (0,) * len(shape))
    in_specs += [full((N_DIL, A_HEADS, HEAD_DIM)), full((N_DIL, A_HEADS, HEAD_DIM)), full((A_HEADS, HEAD_DIM))]
    return pl.pallas_call(
        _attn_sample_kernel,
        grid=(dec_batch,),
        in_specs=in_specs,
        out_specs=[pl.BlockSpec((None, A_HEADS, HEAD_DIM), lambda b: (b, 0, 0)),
                   pl.BlockSpec((None, N_DIL, 2, A_HEADS, HEAD_DIM), lambda b: (b, 0, 0, 0, 0))],
        out_shape=[jax.ShapeDtypeStruct((dec_batch, A_HEADS, HEAD_DIM), F32),
                   jax.ShapeDtypeStruct((dec_batch, N_DIL, 2, A_HEADS, HEAD_DIM), F32)],
        compiler_params=_params(("arbitrary",)),
        name="attn_sample",
    )(*args, gq, gk, slopes8)


def _softplus(x):
    return jnp.maximum(x, 0.0) + jnp.log1p(jnp.exp(-jnp.abs(x)))


def _lru_coeffs(xc, wa, ba, wi, bi, lam):
    xcb = xc.astype(BF16)
    r = jax.nn.sigmoid(jnp.dot(xcb, wa.astype(BF16), preferred_element_type=F32) + ba)
    i = jax.nn.sigmoid(jnp.dot(xcb, wi.astype(BF16), preferred_element_type=F32) + bi)
    log_a = -LRU_C * r * _softplus(-lam)
    a = jnp.exp(log_a)
    t = jnp.tanh(log_a)
    one_minus_a2 = -2.0 * t / (1.0 - t)
    return a, jnp.sqrt(one_minus_a2) * (i * xc)


def _lru_prompt_kernel(x_ref, gate_ref, cw_ref, cb_ref, wa_ref, ba_ref, wi_ref, bi_ref, lam_ref,
                       y_ref, hl_ref, pad_a, pad_b, *, seq):
    blk = x_ref.shape[1]
    off = SUBLANES
    x = x_ref[...]
    pad_a[0:off, :] = jnp.zeros((off, blk), F32)
    pad_a[off:off + seq, :] = x
    xc = pad_a[off - 3:off - 3 + seq, :] * cw_ref[0:1, :]
    xc = xc + pad_a[off - 2:off - 2 + seq, :] * cw_ref[1:2, :]
    xc = xc + pad_a[off - 1:off - 1 + seq, :] * cw_ref[2:3, :]
    xc = xc + x * cw_ref[3:4, :] + cb_ref[...]
    a, b = _lru_coeffs(xc, wa_ref[...], ba_ref[...], wi_ref[...], bi_ref[...], lam_ref[...])

    tpos = lax.broadcasted_iota(I32, (seq, blk), 0) & (SUBLANES - 1)
    pad_b[0:off, :] = jnp.zeros((off, blk), F32)
    for o in (1, 2, 4):
        pad_a[off:off + seq, :] = a
        pad_b[off:off + seq, :] = b
        keep = tpos >= o
        a_prev = jnp.where(keep, pad_a[off - o:off - o + seq, :], 1.0)
        b_prev = jnp.where(keep, pad_b[off - o:off - o + seq, :], 0.0)
        b = a * b_prev + b
        a = a * a_prev
    pad_a[off:off + seq, :] = a
    pad_b[off:off + seq, :] = b

    def tile(i, h_prev):
        r0 = pl.multiple_of(off + i * SUBLANES, SUBLANES)
        h = pad_b[pl.ds(r0, SUBLANES), :] + pad_a[pl.ds(r0, SUBLANES), :] * h_prev
        pad_b[pl.ds(r0, SUBLANES), :] = h
        return jnp.broadcast_to(h[SUBLANES - 1:SUBLANES, :], (SUBLANES, blk))

    lax.fori_loop(0, seq // SUBLANES, tile, jnp.zeros((SUBLANES, blk), F32))
    h = pad_b[off:off + seq, :]
    hl_ref[...] = h[seq - SUBLANES:, :]
    y_ref[...] = (h * jax.nn.gelu(gate_ref[...])).astype(y_ref.dtype)


def _lru_prompt_call(z, cw, cb, wa, ba, wi, bi, lam, *, batch, seq):
    t = batch * seq
    nb, blk, _ = wa.shape
    width = nb * blk
    x0 = A_QKV // blk
    g0 = (A_QKV + width) // blk
    vec = lambda v: v.reshape(1, width)
    vspec = pl.BlockSpec((1, blk), lambda b, n: (0, n))
    return pl.pallas_call(
        functools.partial(_lru_prompt_kernel, seq=seq),
        grid=(batch, nb),
        in_specs=[pl.BlockSpec((seq, blk), lambda b, n: (b, x0 + n)),
                  pl.BlockSpec((seq, blk), lambda b, n: (b, g0 + n)),
                  pl.BlockSpec((B_CONV, blk), lambda b, n: (0, n)), vspec,
                  pl.BlockSpec((None, blk, blk), lambda b, n: (n, 0, 0)), vspec,
                  pl.BlockSpec((None, blk, blk), lambda b, n: (n, 0, 0)), vspec, vspec],
        out_specs=[pl.BlockSpec((seq, blk), lambda b, n: (b, n)),
                   pl.BlockSpec((None, SUBLANES, blk), lambda b, n: (b, 0, n))],
        out_shape=[jax.ShapeDtypeStruct((t, width), BF16),
                   jax.ShapeDtypeStruct((batch, SUBLANES, width), F32)],
        scratch_shapes=[pltpu.VMEM((seq + SUBLANES, blk), F32)] * 2,
        compiler_params=_params(("arbitrary", "arbitrary")),
        name="rglru_prompt",
    )(z, z, cw, vec(cb), wa, vec(ba), wi, vec(bi), vec(lam))


def _lru_sample_kernel(x_ref, gate_ref, buf_ref, h0_ref, cw_ref, cb_ref, wa_ref, ba_ref, wi_ref, bi_ref,
                       lam_ref, y_ref, h_ref):
    nb, blk, _ = wa_ref.shape
    for n in range(nb):
        cs = slice(n * blk, (n + 1) * blk)
        x = x_ref[:, cs]
        xc = buf_ref[0, :, cs] * cw_ref[0:1, cs]
        xc = xc + buf_ref[1, :, cs] * cw_ref[1:2, cs]
        xc = xc + buf_ref[2, :, cs] * cw_ref[2:3, cs]
        xc = xc + x * cw_ref[3:4, cs] + cb_ref[:, cs]
        a, b = _lru_coeffs(xc, wa_ref[n], ba_ref[:, cs], wi_ref[n], bi_ref[:, cs], lam_ref[:, cs])
        h = b + a * h0_ref[:, cs]
        h_ref[:, cs] = h
        y_ref[:, cs] = (h * jax.nn.gelu(gate_ref[:, cs])).astype(y_ref.dtype)


def _lru_sample_call(x, gate, buf, h0, cw, cb, wa, ba, wi, bi, lam):
    rows, width = x.shape
    vec = lambda v: v.reshape(1, width)
    return pl.pallas_call(
        _lru_sample_kernel,
        out_shape=[jax.ShapeDtypeStruct((rows, width), BF16), jax.ShapeDtypeStruct((rows, width), F32)],
        compiler_params=pltpu.CompilerParams(vmem_limit_bytes=VMEM_LIMIT),
        name="rglru_sample",
    )(x, gate, buf, h0, cw, vec(cb), wa, vec(ba), wi, vec(bi), vec(lam))


def _gconv_prompt_kernel(b_ref, c_ref, v_ref, cw_ref, m_ref, ut_ref, pad, *, seq):
    blk = b_ref.shape[1]
    off = SUBLANES
    u = c_ref[...] * v_ref[...]
    pad[0:off, :] = jnp.zeros((off, blk), F32)
    pad[off:off + seq, :] = u
    uc = pad[off - 2:off - 2 + seq, :] * cw_ref[0:1, :]
    uc = uc + pad[off - 1:off - 1 + seq, :] * cw_ref[1:2, :]
    uc = uc + u * cw_ref[2:3, :]
    m_ref[...] = (b_ref[...] * uc).astype(m_ref.dtype)
    ut_ref[...] = u[seq - SUBLANES:, :]


def _gconv_prompt_call(z, cw, *, batch, seq, blk=256):
    t = batch * seq
    d = z.shape[1] // 3
    nblk = d // blk
    return pl.pallas_call(
        functools.partial(_gconv_prompt_kernel, seq=seq),
        grid=(batch, nblk),
        in_specs=[pl.BlockSpec((seq, blk), lambda b, n: (b, n)),
                  pl.BlockSpec((seq, blk), lambda b, n: (b, nblk + n)),
                  pl.BlockSpec((seq, blk), lambda b, n: (b, 2 * nblk + n)),
                  pl.BlockSpec((C_CONV, blk), lambda b, n: (0, n))],
        out_specs=[pl.BlockSpec((seq, blk), lambda b, n: (b, n)),
                   pl.BlockSpec((None, SUBLANES, blk), lambda b, n: (b, 0, n))],
        out_shape=[jax.ShapeDtypeStruct((t, d), BF16), jax.ShapeDtypeStruct((batch, SUBLANES, d), F32)],
        scratch_shapes=[pltpu.VMEM((seq + SUBLANES, blk), F32)],
        compiler_params=_params(("arbitrary", "arbitrary")),
        name="gconv_prompt",
    )(z, z, z, cw)


def _gconv_sample_kernel(z_ref, buf_ref, cw_ref, m_ref, u_ref):
    d = m_ref.shape[1]
    u = z_ref[:, d:2 * d] * z_ref[:, 2 * d:3 * d]
    uc = buf_ref[0] * cw_ref[0:1, :] + buf_ref[1] * cw_ref[1:2, :]
    uc = uc + u * cw_ref[2:3, :]
    m_ref[...] = (z_ref[:, 0:d] * uc).astype(m_ref.dtype)
    u_ref[...] = u


def _gconv_sample_call(z_s, buf, cw):
    rows = z_s.shape[0]
    d = z_s.shape[1] // 3
    return pl.pallas_call(
        _gconv_sample_kernel,
        out_shape=[jax.ShapeDtypeStruct((rows, d), BF16), jax.ShapeDtypeStruct((rows, d), F32)],
        name="gconv_sample",
    )(z_s, buf, cw)


MOE_TM = 256


def _bucket_kernel(pos_p_ref, pos_s_ref, ztile_ref, hp_ref, hs_ref, xs_ref, zero_buf, sem, zsem):
    step = pl.program_id(0)
    tr = hp_ref.shape[0]
    n_s = pos_s_ref.shape[1]

    def copy_p(t, k):
        return pltpu.make_async_copy(hp_ref.at[pl.ds(t, 1)], xs_ref.at[pl.ds(pos_p_ref[k, t], 1)], sem)

    def copy_s(t, k):
        return pltpu.make_async_copy(hs_ref.at[pl.ds(t, 1)], xs_ref.at[pl.ds(pos_s_ref[k, t], 1)], sem)

    def copy_z(j):
        start = pl.multiple_of(ztile_ref[0, j] * MOE_TM, MOE_TM)
        return pltpu.make_async_copy(zero_buf, xs_ref.at[pl.ds(start, MOE_TM)], zsem)

    @pl.when(step == 0)
    def _():
        zero_buf[...] = jnp.zeros_like(zero_buf)
        for j in range(ztile_ref.shape[1]):
            pl.when(ztile_ref[1, j] == 1)(lambda j=j: copy_z(j).start())
        for j in range(ztile_ref.shape[1]):
            pl.when(ztile_ref[1, j] == 1)(lambda j=j: copy_z(j).wait())
        for t in range(n_s):
            copy_s(t, 0).start()
            copy_s(t, 1).start()
        for t in range(n_s):
            copy_s(t, 0).wait()
            copy_s(t, 1).wait()

    def start_p(t, c):
        copy_p(t, 0).start()
        copy_p(t, 1).start()
        return c

    def wait_p(t, c):
        copy_p(t, 0).wait()
        copy_p(t, 1).wait()
        return c

    lax.fori_loop(0, tr, start_p, 0)
    lax.fori_loop(0, tr, wait_p, 0)


def _bucket_call(pos_p, pos_s, ztiles, hp, hs, p_rows, *, tr=256):
    tp, dh = hp.shape
    n_s = pos_s.shape[1]
    smem = lambda shape, imap: pl.BlockSpec(shape, imap, memory_space=pltpu.SMEM)
    return pl.pallas_call(
        _bucket_kernel,
        grid=(tp // tr,),
        in_specs=[smem((2, tr), lambda i: (0, i)), smem((2, n_s), lambda i: (0, 0)),
                  smem(ztiles.shape, lambda i: (0, 0)),
                  pl.BlockSpec((tr, dh), lambda i: (i, 0)),
                  pl.BlockSpec((n_s, dh), lambda i: (0, 0))],
        out_specs=pl.BlockSpec(memory_space=pl.ANY),
        out_shape=jax.ShapeDtypeStruct((p_rows, dh), hp.dtype),
        scratch_shapes=[pltpu.VMEM((MOE_TM, dh), hp.dtype), pltpu.SemaphoreType.DMA, pltpu.SemaphoreType.DMA],
        compiler_params=_params(("arbitrary",)),
        name="moe_bucket",
    )(pos_p, pos_s, ztiles, hp, hs)


S_XTILE, S_OTILE, S_EXPERT, S_WCHUNK, S_OCHUNK, S_FIRST, S_VALID = range(7)


def _gate_up_kernel(s_ref, x_ref, wg_ref, wu_ref, o_ref, wgb_ref, wub_ref):
    it = pl.program_id(0)

    @pl.when(s_ref[S_FIRST, it] == 1)
    def _():
        wgb_ref[...] = wg_ref[...].astype(BF16)
        wub_ref[...] = wu_ref[...].astype(BF16)

    @pl.when(s_ref[S_VALID, it] == 1)
    def _():
        half = wgb_ref.shape[0] // 2
        xa, xb = _unpack_halves(x_ref[...])
        gate = (jnp.dot(xa, wgb_ref[:half, :], preferred_element_type=F32)
                + jnp.dot(xb, wgb_ref[half:, :], preferred_element_type=F32))
        up = (jnp.dot(xa, wub_ref[:half, :], preferred_element_type=F32)
              + jnp.dot(xb, wub_ref[half:, :], preferred_element_type=F32))
        o_ref[...] = (gate * jax.nn.sigmoid(gate) * up).astype(o_ref.dtype)

    @pl.when(s_ref[S_VALID, it] == 0)
    def _():
        o_ref[...] = jnp.zeros_like(o_ref)


def _gate_up_call(sched, xs, w_gate, w_up, layer, *, tn):
    p = xs.shape[0]
    _, _, d, f = w_gate.shape
    n_items = sched.shape[1]
    wspec = pl.BlockSpec((None, None, d, tn), lambda it, s: (layer, s[S_EXPERT, it], 0, s[S_WCHUNK, it]))
    grid_spec = pltpu.PrefetchScalarGridSpec(
        num_scalar_prefetch=1,
        grid=(n_items,),
        in_specs=[pl.BlockSpec((MOE_TM, d // 2), lambda it, s: (s[S_XTILE, it], 0)), wspec, wspec],
        out_specs=pl.BlockSpec((MOE_TM, tn), lambda it, s: (s[S_OTILE, it], s[S_OCHUNK, it])),
        scratch_shapes=[pltpu.VMEM((d, tn), BF16)] * 2,
    )
    return pl.pallas_call(
        _gate_up_kernel,
        grid_spec=grid_spec,
        out_shape=jax.ShapeDtypeStruct((p, f), BF16),
        compiler_params=_params(("arbitrary",)),
        name="moe_gate_up",
    )(sched, xs, w_gate, w_up)


def _down_kernel(s_ref, x_ref, w_ref, o_ref, wb_ref):
    it = pl.program_id(0)

    @pl.when(s_ref[S_FIRST, it] == 1)
    def _():
        wb_ref[...] = w_ref[...].astype(BF16)

    @pl.when(s_ref[S_VALID, it] == 1)
    def _():
        o_ref[...] = jnp.dot(x_ref[...], wb_ref[...], preferred_element_type=F32)

    @pl.when(s_ref[S_VALID, it] == 0)
    def _():
        o_ref[...] = jnp.zeros_like(o_ref)


def _down_call(sched, he, w_down, layer, *, tn):
    p, f = he.shape
    d = w_down.shape[3]
    n_items = sched.shape[1]
    grid_spec = pltpu.PrefetchScalarGridSpec(
        num_scalar_prefetch=1,
        grid=(n_items,),
        in_specs=[pl.BlockSpec((MOE_TM, f), lambda it, s: (s[S_XTILE, it], 0)),
                  pl.BlockSpec((None, None, f, tn), lambda it, s: (layer, s[S_EXPERT, it], 0, s[S_WCHUNK, it]))],
        out_specs=pl.BlockSpec((MOE_TM, tn), lambda it, s: (s[S_OTILE, it], s[S_OCHUNK, it])),
        scratch_shapes=[pltpu.VMEM((f, tn), BF16)],
    )
    return pl.pallas_call(
        _down_kernel,
        grid_spec=grid_spec,
        out_shape=jax.ShapeDtypeStruct((p, d), F32),
        compiler_params=_params(("arbitrary",)),
        name="moe_down",
    )(sched, he, w_down)


def _combine_kernel(pos_ref, res_ref, gate_ref, w_ref, y_ref, o_ref, buf, sem, *, tr):
    def copy(t, k):
        return pltpu.make_async_copy(y_ref.at[pl.ds(pos_ref[k, t], 1)], buf.at[k, pl.ds(t, 1)], sem)

    def start(t, c):
        copy(t, 0).start()
        copy(t, 1).start()
        return c

    def wait(t, c):
        copy(t, 0).wait()
        copy(t, 1).wait()
        return c

    lax.fori_loop(0, tr, start, 0)
    lax.fori_loop(0, tr, wait, 0)
    w = w_ref[...]
    y = w[:, 0:1] * buf[0] + w[:, 1:2] * buf[1]
    o_ref[...] = res_ref[...] + gate_ref[...] * y


def _combine_call(pos, res, gate, w, y, *, per_row, tr, rows_per_mod=None):
    rows, d = res.shape
    gspec = _mod_specs(per_row, tr, d, rows_per_mod)
    return pl.pallas_call(
        functools.partial(_combine_kernel, tr=tr),
        grid=(rows // tr,),
        in_specs=[pl.BlockSpec((2, tr), lambda i: (0, i), memory_space=pltpu.SMEM),
                  pl.BlockSpec((tr, d), lambda i: (i, 0)), gspec,
                  pl.BlockSpec((tr, 2), lambda i: (i, 0)),
                  pl.BlockSpec(memory_space=pl.ANY)],
        out_specs=pl.BlockSpec((tr, d), lambda i: (i, 0)),
        out_shape=jax.ShapeDtypeStruct((rows, d), F32),
        scratch_shapes=[pltpu.VMEM((2, tr, d), F32), pltpu.SemaphoreType.DMA],
        compiler_params=_params(("arbitrary",)),
        name="moe_combine",
    )(pos, res, gate, w, y)


def _moe_schedule(tiles_per_e, max_tiles, n_chunks):
    tile_end = jnp.cumsum(tiles_per_e)
    tile_start = tile_end - tiles_per_e
    n_items = n_chunks * max_tiles
    item_start = tile_start * n_chunks
    item_end = tile_end * n_chunks
    it = jnp.arange(n_items, dtype=I32)
    ex = jnp.minimum(jnp.sum((it[:, None] >= item_end[None, :]).astype(I32), axis=1), N_EXPERTS - 1)
    valid = (it < item_end[-1]).astype(I32)
    local = it - item_start[ex]
    t_e = jnp.maximum(tiles_per_e[ex], 1)
    chunk = local // t_e
    tile = tile_start[ex] + local % t_e
    first = ((local % t_e) == 0).astype(I32) * valid
    n_valid = item_end[-1]
    last = jnp.maximum(n_valid - 1, 0)
    pick = lambda v: jnp.where(valid == 1, v, v[last])
    q = it - n_valid
    otile = jnp.where(valid == 1, tile, tile_end[-1] + q // n_chunks)
    ochunk = jnp.where(valid == 1, chunk, q % n_chunks)
    return jnp.stack([pick(tile), otile, pick(ex), pick(chunk), ochunk, first, valid]).astype(I32)


def _moe(hp, hs, rp, rs, counts, w_gate, w_up, w_down, layer):
    n_assign = 2 * (hp.shape[0] + hs.shape[0])
    max_tiles = (n_assign + N_EXPERTS * (MOE_TM - 1)) // MOE_TM
    tiles_per_e = (counts + MOE_TM - 1) // MOE_TM
    tile_end = jnp.cumsum(tiles_per_e)
    row_start = (tile_end - tiles_per_e) * MOE_TM
    pos_p = row_start[rp[:2]] + rp[2:4]
    pos_s = row_start[rs[:2]] + rs[2:4]
    unused = tile_end[-1] + jnp.arange(N_EXPERTS, dtype=I32)
    ztile = jnp.concatenate([tile_end - 1, unused])
    zflag = jnp.concatenate([tiles_per_e > 0, unused < max_tiles])
    ztiles = jnp.stack([jnp.clip(ztile, 0, max_tiles - 1), zflag.astype(I32)]).astype(I32)
    gu_chunks, dn_chunks = 3, 2
    xs = _bucket_call(pos_p, pos_s, ztiles, hp, hs, max_tiles * MOE_TM)
    he = _gate_up_call(_moe_schedule(tiles_per_e, max_tiles, gu_chunks), xs, w_gate, w_up, layer,
                       tn=w_gate.shape[3] // gu_chunks)
    y = _down_call(_moe_schedule(tiles_per_e, max_tiles, dn_chunks), he, w_down, layer,
                   tn=w_down.shape[3] // dn_chunks)
    return y, pos_p, pos_s


def kernel(x_prompt, x_sample, cache_a_kv0, cache_a_kv1, cache_a_kv2, state_b_h, state_b_conv, state_c_conv,
           c_prompt, c_sample, w_mod, b_mod, norm_g, w_in0, qk_g, b_conv_w, b_conv_b, b_wa, b_ba, b_wi, b_bi,
           b_lambda, w_out0, w_in1, c_conv_w, w_out1, w_router, b_router, w_gate, w_up, w_down):
    batch, seq, d = x_prompt.shape
    dec_batch = x_sample.shape[0]
    depth = w_mod.shape[0]
    tp = batch * seq
    width = b_lambda.shape[1]
    pad_s = SAMPLE_ROWS - dec_batch
    caches = (cache_a_kv0, cache_a_kv1, cache_a_kv2)

    def pad_rows(v):
        return jnp.pad(v, ((0, pad_s),) + ((0, 0),) * (v.ndim - 1))

    res_p = x_prompt.reshape(tp, d)
    res_s = pad_rows(x_sample.reshape(dec_batch, d))
    c_all = jnp.concatenate([c_prompt, jnp.zeros((SUBLANES - batch, d), F32), pad_rows(c_sample)], axis=0)
    mod = _mod_call(c_all, w_mod, b_mod)
    slopes = 2.0 ** (-8.0 * np.arange(1, A_HEADS + 1) / A_HEADS)
    slopes8 = jnp.asarray(np.repeat(slopes[:, None], HEAD_DIM, axis=1), dtype=F32)
    w_router_t = w_router.T

    new_kv_p = [[], [], []]
    new_kv_s = [[], [], []]
    bh_p, bconv_p, cconv_p, bh_s, bconv_s, cconv_s = [], [], [], [], [], []

    for layer in range(depth):
        li = layer // 2
        m = mod[layer]
        chunks_p = [m[:batch, k * d:(k + 1) * d].reshape(batch, 1, d) for k in range(6)]
        chunks_s = [m[SUBLANES:, k * d:(k + 1) * d] for k in range(6)]
        sh1p, sc1p, g1p, sh2p, sc2p, g2p = chunks_p
        sh1s, sc1s, g1s, sh2s, sc2s, g2s = chunks_s

        h_p = _norm_call(res_p, norm_g[layer, 0], sc1p, sh1p, per_row=False, tr=256, rows_per_mod=seq)
        h_s = _norm_call(res_s, norm_g[layer, 0], sc1s, sh1s, per_row=True, tr=SAMPLE_ROWS)

        if layer % 2 == 0:
            z_p, z_s = _dense_call(h_p, h_s, w_in0, li)
            ya_p, kn_p = _attn_prompt_call(z_p, qk_g[li, 0], qk_g[li, 1], slopes8.reshape(A_HEADS, 1, HEAD_DIM),
                                           batch=batch, seq=seq)
            for g, (window, _) in enumerate(DILATIONS):
                keep = min(window, seq)
                kk = kn_p[g].reshape(batch, seq, A_HEADS, HEAD_DIM)[:, seq - keep:]
                v0 = g * 3 * A_WIDTH + 2 * A_WIDTH
                vv = z_p[:, v0:v0 + A_WIDTH].reshape(batch, seq, A_HEADS, HEAD_DIM)[:, seq - keep:]
                new_kv_p[g].append(jnp.stack([kk, vv], axis=2))
            yb_p, hl_p = _lru_prompt_call(z_p, b_conv_w[li], b_conv_b[li], b_wa[li], b_ba[li], b_wi[li],
                                          b_bi[li], b_lambda[li], batch=batch, seq=seq)
            bh_p.append(hl_p[:, SUBLANES - 1])
            xb_p = z_p[:, A_QKV:A_QKV + width].reshape(batch, seq, width)
            bconv_p.append(xb_p[:, seq - (B_CONV - 1):])

            ya_s, kvnew_s = _attn_sample_call(z_s, [c[li] for c in caches], qk_g[li, 0], qk_g[li, 1], slopes8,
                                              dec_batch=dec_batch)
            for g in range(N_DIL):
                new_kv_s[g].append(jnp.concatenate([caches[g][li][:, 1:], kvnew_s[:, g][:, None]], axis=1))
            xb_s = z_s[:, A_QKV:A_QKV + width]
            gb_s = z_s[:, A_QKV + width:]
            buf = jnp.transpose(pad_rows(state_b_conv[li]), (1, 0, 2))
            yb_s, hnew_s = _lru_sample_call(xb_s, gb_s, buf, pad_rows(state_b_h[li]), b_conv_w[li], b_conv_b[li],
                                            b_wa[li], b_ba[li], b_wi[li], b_bi[li], b_lambda[li])
            bh_s.append(hnew_s[:dec_batch])
            bconv_s.append(jnp.concatenate([state_b_conv[li][:, 1:], xb_s[:dec_batch, None]], axis=1))

            mix_p = jnp.concatenate([ya_p, yb_p], axis=1)
            mix_s = jnp.concatenate([pad_rows(ya_s.reshape(dec_batch, A_WIDTH)).astype(BF16), yb_s], axis=1)
            w_out = w_out0
        else:
            z_p, z_s = _dense_call(h_p, h_s, w_in1, li)
            mix_p, ut_p = _gconv_prompt_call(z_p, c_conv_w[li], batch=batch, seq=seq)
            cconv_p.append(ut_p[:, SUBLANES - (C_CONV - 1):])
            buf = jnp.transpose(pad_rows(state_c_conv[li]), (1, 0, 2))
            mix_s, u_s = _gconv_sample_call(z_s, buf, c_conv_w[li])
            cconv_s.append(jnp.concatenate([state_c_conv[li][:, 1:], u_s[:dec_batch, None]], axis=1))
            w_out = w_out1

        res_p, res_s = _dense_call(mix_p, mix_s, w_out, li, res=(res_p, g1p, res_s, g1s), rows_per_mod=seq)

        hp, ep, wp, cnt = _norm_route_call(res_p, norm_g[layer, 1], sc2p, sh2p, w_router_t, b_router,
                                           jnp.zeros((N_EXPERTS, LANES), F32),
                                           per_row=False, tr=256, rows_per_mod=seq)
        hs, es, ws, cnt = _norm_route_call(res_s, norm_g[layer, 1], sc2s, sh2s, w_router_t, b_router, cnt,
                                           per_row=True, tr=SAMPLE_ROWS)
        y, pos_p, pos_s = _moe(hp, hs, ep[:4], es[:4], cnt[:, 0].astype(I32), w_gate, w_up, w_down, layer)
        res_p = _combine_call(pos_p, res_p, g2p, wp[:2].T, y, per_row=False, tr=256, rows_per_mod=seq)
        res_s = _combine_call(pos_s, res_s, g2s, ws[:2].T, y, per_row=True, tr=SAMPLE_ROWS)

    y_prompt = res_p.reshape(batch, seq, d)
    y_sample = res_s[:dec_batch].reshape(dec_batch, 1, d)
    st = lambda xs: jnp.stack(xs, axis=0)
    return (y_prompt, y_sample, st(new_kv_p[0]), st(new_kv_p[1]), st(new_kv_p[2]), st(bh_p), st(bconv_p),
            st(cconv_p), st(new_kv_s[0]), st(new_kv_s[1]), st(new_kv_s[2]), st(bh_s), st(bconv_s), st(cconv_s))
```

```python
import functools

import numpy as np
import jax
import jax.numpy as jnp
from jax import lax
from jax.experimental import pallas as pl
from jax.experimental.pallas import tpu as pltpu

F32 = jnp.float32
BF16 = jnp.bfloat16
I32 = jnp.int32
U32 = jnp.uint32

HEAD_DIM = 128
A_HEADS = 8
DILATIONS = ((128, 1), (512, 4), (2048, 16))
N_DIL = len(DILATIONS)
A_WIDTH = A_HEADS * HEAD_DIM
A_QKV = N_DIL * 3 * A_WIDTH
ATTN_SCALE = HEAD_DIM ** -0.5
QB = 128
B_BLOCKS = 8
B_CONV = 4
C_CONV = 3
LRU_C = 8.0
N_EXPERTS = 16
N_GROUPS = 4
EXP_PER_GROUP = N_EXPERTS // N_GROUPS
EPS = 1e-6
SAMPLE_ROWS = 16
SUBLANES = 8
LANES = 128
NEG_INF = float("-inf")

VMEM_LIMIT = 52 * 1024 * 1024


def _params(sem, vmem=VMEM_LIMIT):
    return pltpu.CompilerParams(dimension_semantics=sem, vmem_limit_bytes=vmem)


def _mod_kernel(c_ref, w_ref, b_ref, o_ref):
    c = c_ref[...]
    cs = (c * jax.nn.sigmoid(c)).astype(BF16)
    o_ref[...] = jnp.dot(cs, w_ref[...].astype(BF16), preferred_element_type=F32) + b_ref[...]


def _mod_call(c_all, w_mod, b_mod):
    depth, d, n = w_mod.shape
    rows = c_all.shape[0]
    tn = 512
    return pl.pallas_call(
        _mod_kernel,
        grid=(depth, n // tn),
        in_specs=[
            pl.BlockSpec((rows, d), lambda l, j: (0, 0)),
            pl.BlockSpec((None, d, tn), lambda l, j: (l, 0, j)),
            pl.BlockSpec((None, 1, tn), lambda l, j: (l, 0, j)),
        ],
        out_specs=pl.BlockSpec((None, rows, tn), lambda l, j: (l, 0, j)),
        out_shape=jax.ShapeDtypeStruct((depth, rows, n), F32),
        compiler_params=_params(("arbitrary", "arbitrary")),
        name="mod_proj",
    )(c_all, w_mod, b_mod.reshape(depth, 1, n))


def _modnorm(x, g, sc, sh):
    h = x * lax.rsqrt(jnp.mean(x * x, axis=-1, keepdims=True) + EPS) * g
    return h * (1.0 + sc) + sh


def _norm_kernel(x_ref, g_ref, sc_ref, sh_ref, o_ref):
    o_ref[...] = _modnorm(x_ref[...], g_ref[...], sc_ref[...], sh_ref[...]).astype(o_ref.dtype)


def _route(logits_t, bias_ref):
    score = [jax.nn.sigmoid(logits_t[e:e + 1, :]) for e in range(N_EXPERTS)]
    sel = [score[e] + bias_ref[e] for e in range(N_EXPERTS)]
    gsum = []
    for g in range(N_GROUPS):
        a, b, c, d = sel[4 * g:4 * g + 4]
        hi1, lo1 = jnp.maximum(a, b), jnp.minimum(a, b)
        hi2, lo2 = jnp.maximum(c, d), jnp.minimum(c, d)
        top = jnp.maximum(hi1, hi2)
        second = jnp.maximum(jnp.minimum(hi1, hi2), jnp.maximum(lo1, lo2))
        gsum.append(top + second)
    best = gsum[0]
    grp = jnp.zeros_like(best, dtype=I32)
    for g in range(1, N_GROUPS):
        better = gsum[g] > best
        best = jnp.where(better, gsum[g], best)
        grp = jnp.where(better, g, grp)
    v1 = jnp.full_like(best, NEG_INF)
    e1 = jnp.zeros_like(grp)
    for e in range(N_EXPERTS):
        cand = jnp.where(grp == e // EXP_PER_GROUP, sel[e], NEG_INF)
        better = cand > v1
        v1 = jnp.where(better, cand, v1)
        e1 = jnp.where(better, e, e1)
    v2 = jnp.full_like(best, NEG_INF)
    e2 = jnp.zeros_like(grp)
    for e in range(N_EXPERTS):
        cand = jnp.where((grp == e // EXP_PER_GROUP) & (e1 != e), sel[e], NEG_INF)
        better = cand > v2
        v2 = jnp.where(better, cand, v2)
        e2 = jnp.where(better, e, e2)
    s1 = jnp.zeros_like(best)
    s2 = jnp.zeros_like(best)
    for e in range(N_EXPERTS):
        s1 = jnp.where(e1 == e, score[e], s1)
        s2 = jnp.where(e2 == e, score[e], s2)
    tot = s1 + s2
    return e1, e2, s1 / tot, s2 / tot


def _norm_route_kernel(bias_ref, x_ref, g_ref, sc_ref, sh_ref, wr_ref, cnt0_ref,
                       o_ref, e_ref, w_ref, cnt_ref, run_ref):
    @pl.when(pl.program_id(0) == 0)
    def _():
        run_ref[...] = cnt0_ref[...]

    h = _modnorm(x_ref[...], g_ref[...], sc_ref[...], sh_ref[...])
    o_ref[...] = h
    nt = (((1,), (1,)), ((), ()))
    logits_t = lax.dot_general(wr_ref[...].astype(BF16), h.astype(BF16), nt, preferred_element_type=F32)
    e1, e2, w1, w2 = _route(logits_t, bias_ref)
    n = e1.shape[1]
    expert = lax.broadcasted_iota(I32, (N_EXPERTS, n), 0)
    oh1 = (expert == e1).astype(F32)
    oh2 = (expert == e2).astype(F32)
    before = (lax.broadcasted_iota(I32, (n, n), 0) < lax.broadcasted_iota(I32, (n, n), 1)).astype(BF16)
    pre1 = jnp.dot(oh1.astype(BF16), before, preferred_element_type=F32)
    pre2 = jnp.dot(oh2.astype(BF16), before, preferred_element_type=F32)
    tot1 = jnp.sum(oh1, axis=1, keepdims=True)
    tot2 = jnp.sum(oh2, axis=1, keepdims=True)
    base = run_ref[...][:, :1]
    r1 = jnp.sum(oh1 * (base + pre1), axis=0, keepdims=True)
    r2 = jnp.sum(oh2 * (base + tot1 + pre2), axis=0, keepdims=True)
    run_ref[...] = run_ref[...] + (tot1 + tot2)
    cnt_ref[...] = run_ref[...]
    e_ref[...] = jnp.concatenate([e1, e2, r1.astype(I32), r2.astype(I32), jnp.zeros((SUBLANES - 4, n), I32)], axis=0)
    w_ref[...] = jnp.concatenate([w1, w2, jnp.zeros((SUBLANES - 2, n), F32)], axis=0)


def _mod_specs(per_row, tr, d, rows_per_mod):
    if per_row:
        spec = pl.BlockSpec((tr, d), lambda i, *_: (i, 0))
    else:
        spec = pl.BlockSpec((None, 1, d), lambda i, *_: (i * tr // rows_per_mod, 0, 0))
    return spec


def _norm_call(x, g, sc, sh, *, per_row, tr, rows_per_mod=None):
    rows, d = x.shape
    mspec = _mod_specs(per_row, tr, d, rows_per_mod)
    return pl.pallas_call(
        _norm_kernel,
        grid=(rows // tr,),
        in_specs=[pl.BlockSpec((tr, d), lambda i: (i, 0)),
                  pl.BlockSpec((1, d), lambda i: (0, 0)), mspec, mspec],
        out_specs=pl.BlockSpec((tr, d), lambda i: (i, 0)),
        out_shape=jax.ShapeDtypeStruct((rows, d), BF16),
        compiler_params=_params(("arbitrary",)),
        name="modnorm",
    )(x, g.reshape(1, d), sc, sh)


def _norm_route_call(x, g, sc, sh, w_router_t, b_router, cnt0, *, per_row, tr, rows_per_mod=None):
    rows, d = x.shape
    mspec = _mod_specs(per_row, tr, d, rows_per_mod)
    cspec = pl.BlockSpec((N_EXPERTS, LANES), lambda i, b: (0, 0))
    grid_spec = pltpu.PrefetchScalarGridSpec(
        num_scalar_prefetch=1,
        grid=(rows // tr,),
        in_specs=[pl.BlockSpec((tr, d), lambda i, b: (i, 0)),
                  pl.BlockSpec((1, d), lambda i, b: (0, 0)), mspec, mspec,
                  pl.BlockSpec((N_EXPERTS, d), lambda i, b: (0, 0)), cspec],
        out_specs=[pl.BlockSpec((tr, d), lambda i, b: (i, 0)),
                   pl.BlockSpec((SUBLANES, tr), lambda i, b: (0, i)),
                   pl.BlockSpec((SUBLANES, tr), lambda i, b: (0, i)), cspec],
        scratch_shapes=[pltpu.VMEM((N_EXPERTS, LANES), F32)],
    )
    return pl.pallas_call(
        _norm_route_kernel,
        grid_spec=grid_spec,
        out_shape=[jax.ShapeDtypeStruct((rows, d), F32),
                   jax.ShapeDtypeStruct((SUBLANES, rows), I32),
                   jax.ShapeDtypeStruct((SUBLANES, rows), F32),
                   jax.ShapeDtypeStruct((N_EXPERTS, LANES), F32)],
        compiler_params=_params(("arbitrary",)),
        name="modnorm_route",
    )(b_router, x, g.reshape(1, d), sc, sh, w_router_t, cnt0)


def _dense_kernel(*refs, residual):
    if residual:
        (xp_ref, xs_ref, w_ref, rp_ref, gp_ref, rs_ref, gs_ref, op_ref, os_ref, wb_ref) = refs
    else:
        (xp_ref, xs_ref, w_ref, op_ref, os_ref, wb_ref) = refs

    @pl.when(pl.program_id(1) == 0)
    def _():
        wb_ref[...] = w_ref[...].astype(BF16)
        acc_s = jnp.dot(xs_ref[...], wb_ref[...], preferred_element_type=F32)
        if residual:
            acc_s = rs_ref[...] + gs_ref[...] * acc_s
        os_ref[...] = acc_s.astype(os_ref.dtype)

    acc = jnp.dot(xp_ref[...], wb_ref[...], preferred_element_type=F32)
    if residual:
        acc = rp_ref[...] + gp_ref[...] * acc
    op_ref[...] = acc.astype(op_ref.dtype)


def _dense_call(xp, xs, w, li, *, res=None, rows_per_mod=None, out_dtype=F32, tm=1024, tn=512):
    tp, k = xp.shape
    n = w.shape[2]
    rs = xs.shape[0]
    in_specs = [pl.BlockSpec((tm, k), lambda j, i: (i, 0)),
                pl.BlockSpec((rs, k), lambda j, i: (0, 0)),
                pl.BlockSpec((None, k, tn), lambda j, i: (li, 0, j))]
    args = [xp, xs, w]
    if res is not None:
        in_specs += [pl.BlockSpec((tm, tn), lambda j, i: (i, j)),
                     pl.BlockSpec((None, 1, tn), lambda j, i: (i * tm // rows_per_mod, 0, j)),
                     pl.BlockSpec((rs, tn), lambda j, i: (0, j)),
                     pl.BlockSpec((rs, tn), lambda j, i: (0, j))]
        args += list(res)
    return pl.pallas_call(
        functools.partial(_dense_kernel, residual=res is not None),
        grid=(n // tn, tp // tm),
        in_specs=in_specs,
        out_specs=[pl.BlockSpec((tm, tn), lambda j, i: (i, j)),
                   pl.BlockSpec((rs, tn), lambda j, i: (0, j))],
        out_shape=[jax.ShapeDtypeStruct((tp, n), out_dtype), jax.ShapeDtypeStruct((rs, n), out_dtype)],
        scratch_shapes=[pltpu.VMEM((k, tn), BF16)],
        compiler_params=_params(("arbitrary", "arbitrary")),
        name="dense_proj",
    )(*args)


def _rms_rows(x, g):
    return x * lax.rsqrt(jnp.mean(x * x, axis=-1, keepdims=True) + EPS) * g


def _attn_prompt_kernel(q0, k0, v0, q1, k1, v1, q2, k2, v2, gq_ref, gk_ref, slope_ref,
                        ya_ref, kn0, kn1, kn2, qn_s, m_s, l_s, acc_s, *, seq):
    qkv = ((q0, k0, v0, kn0), (q1, k1, v1, kn1), (q2, k2, v2, kn2))
    slope = slope_ref[...]
    row = lax.broadcasted_iota(I32, (QB, 2 * QB), 0)
    col = lax.broadcasted_iota(I32, (QB, 2 * QB), 1)
    rel_prev = row + QB - col
    rel_first = (row - col)[:, :QB]
    nt = (((1,), (1,)), ((), ()))

    for g, (window, dil) in enumerate(DILATIONS):
        q_ref, k_ref, v_ref, kn_ref = qkv[g]
        nk = window // dil
        n_blk = seq // dil // QB
        qn_s[...] = _rms_rows(q_ref[...], gq_ref[g])
        kn_ref[...] = _rms_rows(k_ref[...], gk_ref[g])
        sl = slope[:, :1] * float(dil)
        bias_prev = jnp.where((rel_prev >= 0) & (rel_prev <= nk), -sl * rel_prev.astype(F32), NEG_INF)
        bias_first = jnp.where(rel_first >= 0, -sl * rel_first.astype(F32), NEG_INF)

        def rows(start, size, dil=dil):
            return pl.ds(start, size) if dil == 1 else pl.ds(start, size, stride=dil)

        def block(qstart, kstart, first, g=g, kn_ref=kn_ref, v_ref=v_ref, rows=rows,
                  bias_prev=bias_prev, bias_first=bias_first):
            nkeys = QB if first else 2 * QB
            qb = qn_s[rows(qstart, QB), :].astype(BF16)
            kb = kn_ref[rows(kstart, nkeys), :].astype(BF16)
            vb = v_ref[rows(kstart, nkeys), :].astype(BF16)
            s = lax.dot_general(qb, kb, nt, preferred_element_type=F32) * ATTN_SCALE
            s = s + (bias_first if first else bias_prev)
            m_blk = jnp.max(s, axis=-1, keepdims=True)
            p = jnp.exp(s - m_blk)
            l_blk = jnp.sum(p, axis=-1, keepdims=True)
            o_blk = jnp.dot(p.astype(BF16), vb, preferred_element_type=F32)
            qrows = rows(qstart, QB)
            if g == 0:
                m_s[qrows, :] = jnp.broadcast_to(m_blk, (QB, HEAD_DIM))
                l_s[qrows, :] = jnp.broadcast_to(l_blk, (QB, HEAD_DIM))
                acc_s[qrows, :] = o_blk
            else:
                m_old = m_s[qrows, :]
                m_new = jnp.maximum(m_old, m_blk)
                a_old = jnp.exp(m_old - m_new)
                a_blk = jnp.exp(m_blk - m_new)
                m_s[qrows, :] = m_new
                l_s[qrows, :] = l_s[qrows, :] * a_old + l_blk * a_blk
                acc_s[qrows, :] = acc_s[qrows, :] * a_old + o_blk * a_blk

        if dil == 1:
            block(0, 0, True)

            def body(n, carry, block=block):
                qstart = pl.multiple_of(n * QB, QB)
                block(qstart, pl.multiple_of(qstart - QB, QB), False)
                return carry

            lax.fori_loop(1, n_blk, body, 0)
        else:
            def body(r, carry, block=block, n_blk=n_blk, dil=dil):
                block(r, r, True)
                for n in range(1, n_blk):
                    block(r + n * QB * dil, r + (n - 1) * QB * dil, False)
                return carry

            lax.fori_loop(0, dil, body, 0)

    ya_ref[...] = (acc_s[...] / l_s[...]).astype(ya_ref.dtype)


def _attn_prompt_call(z, gq, gk, slopes, *, batch, seq):
    t = batch * seq

    def col(g, c):
        return lambda b, h: (b, g * 3 * A_HEADS + c * A_HEADS + h)

    in_specs = []
    for g in range(N_DIL):
        for c in range(3):
            in_specs.append(pl.BlockSpec((seq, HEAD_DIM), col(g, c)))
    gspec = pl.BlockSpec((N_DIL, None, 1, HEAD_DIM), lambda b, h: (0, h, 0, 0))
    in_specs += [gspec, gspec, pl.BlockSpec((None, 1, HEAD_DIM), lambda b, h: (h, 0, 0))]
    out_block = pl.BlockSpec((seq, HEAD_DIM), lambda b, h: (b, h))
    outs = pl.pallas_call(
        functools.partial(_attn_prompt_kernel, seq=seq),
        grid=(batch, A_HEADS),
        in_specs=in_specs,
        out_specs=[out_block] * 4,
        out_shape=[jax.ShapeDtypeStruct((t, A_WIDTH), BF16)] + [jax.ShapeDtypeStruct((t, A_WIDTH), F32)] * 3,
        scratch_shapes=[pltpu.VMEM((seq, HEAD_DIM), F32)] * 4,
        compiler_params=_params(("arbitrary", "arbitrary")),
        name="attn_prompt",
    )(*([z] * 9), gq.reshape(N_DIL, A_HEADS, 1, HEAD_DIM), gk.reshape(N_DIL, A_HEADS, 1, HEAD_DIM), slopes)
    return outs[0], outs[1:]


def _round_bf16(x):
    return x.astype(BF16).astype(F32)


def _attn_sample_kernel(z_ref, c0_ref, c1_ref, c2_ref, gq_ref, gk_ref, slope_ref, ya_ref, new_ref):
    caches = (c0_ref, c1_ref, c2_ref)
    slope = slope_ref[...][:, :1]
    nk = DILATIONS[0][0] // DILATIONS[0][1]
    steps = (nk - lax.broadcasted_iota(I32, (nk, 1, 1), 0)).astype(F32)
    m_run = l_run = acc = None
    for g, (window, dil) in enumerate(DILATIONS):
        base = g * 3 * A_HEADS
        q = _rms_rows(z_ref[base:base + A_HEADS, :], gq_ref[g])
        k = _rms_rows(z_ref[base + A_HEADS:base + 2 * A_HEADS, :], gk_ref[g])
        v = z_ref[base + 2 * A_HEADS:base + 3 * A_HEADS, :]
        new_ref[g, 0] = k
        new_ref[g, 1] = v
        qr = _round_bf16(q)
        ck = _round_bf16(caches[g][:, 0])
        cv = _round_bf16(caches[g][:, 1])
        s_c = jnp.sum(ck * qr[None], axis=-1, keepdims=True) * ATTN_SCALE
        s_c = s_c - (slope * float(dil))[None] * steps
        s_n = jnp.sum(_round_bf16(k) * qr, axis=-1, keepdims=True) * ATTN_SCALE
        m_g = jnp.maximum(jnp.max(s_c, axis=0), s_n)
        p_c = jnp.exp(s_c - m_g[None])
        p_n = jnp.exp(s_n - m_g)
        l_g = jnp.sum(p_c, axis=0) + p_n
        o_g = jnp.sum(_round_bf16(p_c) * cv, axis=0) + _round_bf16(p_n) * _round_bf16(v)
        if g == 0:
            m_run, l_run, acc = m_g, l_g, o_g
        else:
            m_new = jnp.maximum(m_run, m_g)
            a_old = jnp.exp(m_run - m_new)
            a_g = jnp.exp(m_g - m_new)
            l_run = l_run * a_old + l_g * a_g
            acc = acc * a_old + o_g * a_g
            m_run = m_new
    ya_ref[...] = acc / l_run


def _attn_sample_call(z_s, caches, gq, gk, slopes8, *, dec_batch):
    n_colblk = z_s.shape[1] // HEAD_DIM
    z3 = z_s.reshape(z_s.shape[0], n_colblk, HEAD_DIM)
    in_specs = [pl.BlockSpec((None, n_colblk, HEAD_DIM), lambda b: (b, 0, 0))]
    args = [z3]
    for g, (window, dil) in enumerate(DILATIONS):
        nk = window // dil
        c = caches[g].reshape(dec_batch, nk, dil, 2, A_HEADS, HEAD_DIM)
        in_specs.append(pl.BlockSpec((None, nk, None, 2, A_HEADS, HEAD_DIM), lambda b: (b, 0, 0, 0, 0, 0)))
        args.append(c)
    full = lambda shape: pl.BlockSpec(shape, lambda b: (0,) * len(shape))
    in_specs += [full((N_DIL, A_HEADS, HEAD_DIM)), full((N_DIL, A_HEADS, HEAD_DIM)), full((A_HEADS, HEAD_DIM))]
    return pl.pallas_call(
        _attn_sample_kernel,
        grid=(dec_batch,),
        in_specs=in_specs,
        out_specs=[pl.BlockSpec((None, A_HEADS, HEAD_DIM), lambda b: (b, 0, 0)),
                   pl.BlockSpec((None, N_DIL, 2, A_HEADS, HEAD_DIM), lambda b: (b, 0, 0, 0, 0))],
        out_shape=[jax.ShapeDtypeStruct((dec_batch, A_HEADS, HEAD_DIM), F32),
                   jax.ShapeDtypeStruct((dec_batch, N_DIL, 2, A_HEADS, HEAD_DIM), F32)],
        compiler_params=_params(("arbitrary",)),
        name="attn_sample",
    )(*args, gq, gk, slopes8)


def _softplus(x):
    return jnp.maximum(x, 0.0) + jnp.log1p(jnp.exp(-jnp.abs(x)))


def _lru_coeffs(xc, wa, ba, wi, bi, lam):
    xcb = xc.astype(BF16)
    r = jax.nn.sigmoid(jnp.dot(xcb, wa.astype(BF16), preferred_element_type=F32) + ba)
    i = jax.nn.sigmoid(jnp.dot(xcb, wi.astype(BF16), preferred_element_type=F32) + bi)
    log_a = -LRU_C * r * _softplus(-lam)
    a = jnp.exp(log_a)
    t = jnp.tanh(log_a)
    one_minus_a2 = -2.0 * t / (1.0 - t)
    return a, jnp.sqrt(one_minus_a2) * (i * xc)


def _lru_prompt_kernel(x_ref, gate_ref, cw_ref, cb_ref, wa_ref, ba_ref, wi_ref, bi_ref, lam_ref,
                       y_ref, hl_ref, pad_a, pad_b, *, seq):
    blk = x_ref.shape[1]
    off = SUBLANES
    x = x_ref[...]
    pad_a[0:off, :] = jnp.zeros((off, blk), F32)
    pad_a[off:off + seq, :] = x
    xc = pad_a[off - 3:off - 3 + seq, :] * cw_ref[0:1, :]
    xc = xc + pad_a[off - 2:off - 2 + seq, :] * cw_ref[1:2, :]
    xc = xc + pad_a[off - 1:off - 1 + seq, :] * cw_ref[2:3, :]
    xc = xc + x * cw_ref[3:4, :] + cb_ref[...]
    a, b = _lru_coeffs(xc, wa_ref[...], ba_ref[...], wi_ref[...], bi_ref[...], lam_ref[...])

    tpos = lax.broadcasted_iota(I32, (seq, blk), 0) & (SUBLANES - 1)
    pad_b[0:off, :] = jnp.zeros((off, blk), F32)
    for o in (1, 2, 4):
        pad_a[off:off + seq, :] = a
        pad_b[off:off + seq, :] = b
        keep = tpos >= o
        a_prev = jnp.where(keep, pad_a[off - o:off - o + seq, :], 1.0)
        b_prev = jnp.where(keep, pad_b[off - o:off - o + seq, :], 0.0)
        b = a * b_prev + b
        a = a * a_prev
    pad_a[off:off + seq, :] = a
    pad_b[off:off + seq, :] = b

    def tile(i, h_prev):
        r0 = pl.multiple_of(off + i * SUBLANES, SUBLANES)
        h = pad_b[pl.ds(r0, SUBLANES), :] + pad_a[pl.ds(r0, SUBLANES), :] * h_prev
        pad_b[pl.ds(r0, SUBLANES), :] = h
        return jnp.broadcast_to(h[SUBLANES - 1:SUBLANES, :], (SUBLANES, blk))

    lax.fori_loop(0, seq // SUBLANES, tile, jnp.zeros((SUBLANES, blk), F32))
    h = pad_b[off:off + seq, :]
    hl_ref[...] = h[seq - SUBLANES:, :]
    y_ref[...] = (h * jax.nn.gelu(gate_ref[...])).astype(y_ref.dtype)


def _lru_prompt_call(z, cw, cb, wa, ba, wi, bi, lam, *, batch, seq):
    t = batch * seq
    nb, blk, _ = wa.shape
    width = nb * blk
    x0 = A_QKV // blk
    g0 = (A_QKV + width) // blk
    vec = lambda v: v.reshape(1, width)
    vspec = pl.BlockSpec((1, blk), lambda b, n: (0, n))
    return pl.pallas_call(
        functools.partial(_lru_prompt_kernel, seq=seq),
        grid=(batch, nb),
        in_specs=[pl.BlockSpec((seq, blk), lambda b, n: (b, x0 + n)),
                  pl.BlockSpec((seq, blk), lambda b, n: (b, g0 + n)),
                  pl.BlockSpec((B_CONV, blk), lambda b, n: (0, n)), vspec,
                  pl.BlockSpec((None, blk, blk), lambda b, n: (n, 0, 0)), vspec,
                  pl.BlockSpec((None, blk, blk), lambda b, n: (n, 0, 0)), vspec, vspec],
        out_specs=[pl.BlockSpec((seq, blk), lambda b, n: (b, n)),
                   pl.BlockSpec((None, SUBLANES, blk), lambda b, n: (b, 0, n))],
        out_shape=[jax.ShapeDtypeStruct((t, width), BF16),
                   jax.ShapeDtypeStruct((batch, SUBLANES, width), F32)],
        scratch_shapes=[pltpu.VMEM((seq + SUBLANES, blk), F32)] * 2,
        compiler_params=_params(("arbitrary", "arbitrary")),
        name="rglru_prompt",
    )(z, z, cw, vec(cb), wa, vec(ba), wi, vec(bi), vec(lam))


def _lru_sample_kernel(x_ref, gate_ref, buf_ref, h0_ref, cw_ref, cb_ref, wa_ref, ba_ref, wi_ref, bi_ref,
                       lam_ref, y_ref, h_ref):
    nb, blk, _ = wa_ref.shape
    for n in range(nb):
        cs = slice(n * blk, (n + 1) * blk)
        x = x_ref[:, cs]
        xc = buf_ref[0, :, cs] * cw_ref[0:1, cs]
        xc = xc + buf_ref[1, :, cs] * cw_ref[1:2, cs]
        xc = xc + buf_ref[2, :, cs] * cw_ref[2:3, cs]
        xc = xc + x * cw_ref[3:4, cs] + cb_ref[:, cs]
        a, b = _lru_coeffs(xc, wa_ref[n], ba_ref[:, cs], wi_ref[n], bi_ref[:, cs], lam_ref[:, cs])
        h = b + a * h0_ref[:, cs]
        h_ref[:, cs] = h
        y_ref[:, cs] = (h * jax.nn.gelu(gate_ref[:, cs])).astype(y_ref.dtype)


def _lru_sample_call(x, gate, buf, h0, cw, cb, wa, ba, wi, bi, lam):
    rows, width = x.shape
    vec = lambda v: v.reshape(1, width)
    return pl.pallas_call(
        _lru_sample_kernel,
        out_shape=[jax.ShapeDtypeStruct((rows, width), BF16), jax.ShapeDtypeStruct((rows, width), F32)],
        compiler_params=pltpu.CompilerParams(vmem_limit_bytes=VMEM_LIMIT),
        name="rglru_sample",
    )(x, gate, buf, h0, cw, vec(cb), wa, vec(ba), wi, vec(bi), vec(lam))


def _gconv_prompt_kernel(b_ref, c_ref, v_ref, cw_ref, m_ref, ut_ref, pad, *, seq):
    blk = b_ref.shape[1]
    off = SUBLANES
    u = c_ref[...] * v_ref[...]
    pad[0:off, :] = jnp.zeros((off, blk), F32)
    pad[off:off + seq, :] = u
    uc = pad[off - 2:off - 2 + seq, :] * cw_ref[0:1, :]
    uc = uc + pad[off - 1:off - 1 + seq, :] * cw_ref[1:2, :]
    uc = uc + u * cw_ref[2:3, :]
    m_ref[...] = (b_ref[...] * uc).astype(m_ref.dtype)
    ut_ref[...] = u[seq - SUBLANES:, :]


def _gconv_prompt_call(z, cw, *, batch, seq, blk=256):
    t = batch * seq
    d = z.shape[1] // 3
    nblk = d // blk
    return pl.pallas_call(
        functools.partial(_gconv_prompt_kernel, seq=seq),
        grid=(batch, nblk),
        in_specs=[pl.BlockSpec((seq, blk), lambda b, n: (b, n)),
                  pl.BlockSpec((seq, blk), lambda b, n: (b, nblk + n)),
                  pl.BlockSpec((seq, blk), lambda b, n: (b, 2 * nblk + n)),
                  pl.BlockSpec((C_CONV, blk), lambda b, n: (0, n))],
        out_specs=[pl.BlockSpec((seq, blk), lambda b, n: (b, n)),
                   pl.BlockSpec((None, SUBLANES, blk), lambda b, n: (b, 0, n))],
        out_shape=[jax.ShapeDtypeStruct((t, d), BF16), jax.ShapeDtypeStruct((batch, SUBLANES, d), F32)],
        scratch_shapes=[pltpu.VMEM((seq + SUBLANES, blk), F32)],
        compiler_params=_params(("arbitrary", "arbitrary")),
        name="gconv_prompt",
    )(z, z, z, cw)


def _gconv_sample_kernel(z_ref, buf_ref, cw_ref, m_ref, u_ref):
    d = m_ref.shape[1]
    u = z_ref[:, d:2 * d] * z_ref[:, 2 * d:3 * d]
    uc = buf_ref[0] * cw_ref[0:1, :] + buf_ref[1] * cw_ref[1:2, :]
    uc = uc + u * cw_ref[2:3, :]
    m_ref[...] = (z_ref[:, 0:d] * uc).astype(m_ref.dtype)
    u_ref[...] = u


def _gconv_sample_call(z_s, buf, cw):
    rows = z_s.shape[0]
    d = z_s.shape[1] // 3
    return pl.pallas_call(
        _gconv_sample_kernel,
        out_shape=[jax.ShapeDtypeStruct((rows, d), BF16), jax.ShapeDtypeStruct((rows, d), F32)],
        name="gconv_sample",
    )(z_s, buf, cw)


MOE_TM = 512
MOE_HALF = MOE_TM // 2


def _bucket_kernel(pos_p_ref, pos_s_ref, ztile_ref, hp_ref, hs_ref, xs_ref, zero_buf, sem, zsem):
    step = pl.program_id(0)
    tr = hp_ref.shape[0]
    n_s = pos_s_ref.shape[1]

    def copy_p(t, k):
        return pltpu.make_async_copy(hp_ref.at[pl.ds(t, 1)], xs_ref.at[pl.ds(pos_p_ref[k, t], 1)], sem)

    def copy_s(t, k):
        return pltpu.make_async_copy(hs_ref.at[pl.ds(t, 1)], xs_ref.at[pl.ds(pos_s_ref[k, t], 1)], sem)

    def copy_z(j):
        start = pl.multiple_of(ztile_ref[0, j] * MOE_HALF, MOE_HALF)
        return pltpu.make_async_copy(zero_buf, xs_ref.at[pl.ds(start, MOE_HALF)], zsem)

    @pl.when(step == 0)
    def _():
        zero_buf[...] = jnp.zeros_like(zero_buf)
        for j in range(ztile_ref.shape[1]):
            pl.when(ztile_ref[1, j] == 1)(lambda j=j: copy_z(j).start())
        for j in range(ztile_ref.shape[1]):
            pl.when(ztile_ref[1, j] == 1)(lambda j=j: copy_z(j).wait())
        for t in range(n_s):
            copy_s(t, 0).start()
            copy_s(t, 1).start()
        for t in range(n_s):
            copy_s(t, 0).wait()
            copy_s(t, 1).wait()

    def start_p(t, c):
        copy_p(t, 0).start()
        copy_p(t, 1).start()
        return c

    def wait_p(t, c):
        copy_p(t, 0).wait()
        copy_p(t, 1).wait()
        return c

    lax.fori_loop(0, tr, start_p, 0)
    lax.fori_loop(0, tr, wait_p, 0)


def _bucket_call(pos_p, pos_s, ztiles, hp, hs, p_rows, *, tr=256):
    tp, dh = hp.shape
    n_s = pos_s.shape[1]
    smem = lambda shape, imap: pl.BlockSpec(shape, imap, memory_space=pltpu.SMEM)
    return pl.pallas_call(
        _bucket_kernel,
        grid=(tp // tr,),
        in_specs=[smem((2, tr), lambda i: (0, i)), smem((2, n_s), lambda i: (0, 0)),
                  smem(ztiles.shape, lambda i: (0, 0)),
                  pl.BlockSpec((tr, dh), lambda i: (i, 0)),
                  pl.BlockSpec((n_s, dh), lambda i: (0, 0))],
        out_specs=pl.BlockSpec(memory_space=pl.ANY),
        out_shape=jax.ShapeDtypeStruct((p_rows, dh), hp.dtype),
        scratch_shapes=[pltpu.VMEM((MOE_HALF, dh), hp.dtype), pltpu.SemaphoreType.DMA, pltpu.SemaphoreType.DMA],
        compiler_params=_params(("arbitrary",)),
        name="moe_bucket",
    )(pos_p, pos_s, ztiles, hp, hs)


(S_XTILE, S_OTILE, S_EXPERT, S_WCHUNK, S_OCHUNK, S_FIRST, S_VALID, S_FULL,
 S_HAS_NEXT, S_NEXT_EXPERT, S_NEXT_CHUNK) = range(11)


def _grouped_kernel(s_ref, x_ref, *refs, n_w, layer, tn, swiglu):
    w_hbm = refs[:n_w]
    o_ref = refs[n_w]
    w_f32 = refs[n_w + 1:2 * n_w + 1]
    w_b16 = refs[2 * n_w + 1:3 * n_w + 1]
    sem = refs[3 * n_w + 1]
    it = pl.program_id(0)

    def w_copy(k, expert, chunk):
        col = pl.multiple_of(chunk * tn, LANES)
        return pltpu.make_async_copy(w_hbm[k].at[layer, expert, :, pl.ds(col, tn)], w_f32[k], sem.at[k])

    @pl.when(it == 0)
    def _():
        for k in range(n_w):
            w_copy(k, s_ref[S_EXPERT, 0], s_ref[S_WCHUNK, 0]).start()

    @pl.when(s_ref[S_FIRST, it] == 1)
    def _():
        for k in range(n_w):
            w_copy(k, s_ref[S_EXPERT, it], s_ref[S_WCHUNK, it]).wait()
            w_b16[k][...] = w_f32[k][...].astype(BF16)

        @pl.when(s_ref[S_HAS_NEXT, it] == 1)
        def _():
            for k in range(n_w):
                w_copy(k, s_ref[S_NEXT_EXPERT, it], s_ref[S_NEXT_CHUNK, it]).start()

    def compute(rows):
        x = x_ref[0:rows, :].astype(BF16)
        acc = jnp.dot(x, w_b16[0][...], preferred_element_type=F32)
        if swiglu:
            acc = acc * jax.nn.sigmoid(acc) * jnp.dot(x, w_b16[1][...], preferred_element_type=F32)
        o_ref[0:rows, :] = acc.astype(o_ref.dtype)

    valid = s_ref[S_VALID, it] == 1
    full = s_ref[S_FULL, it] == 1

    @pl.when(valid & full)
    def _():
        compute(MOE_TM)

    @pl.when(valid & jnp.logical_not(full))
    def _():
        compute(MOE_HALF)
        o_ref[MOE_HALF:, :] = jnp.zeros((MOE_TM - MOE_HALF, tn), o_ref.dtype)

    @pl.when(jnp.logical_not(valid))
    def _():
        o_ref[...] = jnp.zeros_like(o_ref)


def _grouped_call(sched, x, weights, layer, *, tn, out_dtype, swiglu, name):
    p, k = x.shape
    n = weights[0].shape[3]
    n_w = len(weights)
    n_items = sched.shape[1]
    grid_spec = pltpu.PrefetchScalarGridSpec(
        num_scalar_prefetch=1,
        grid=(n_items,),
        in_specs=[pl.BlockSpec((MOE_TM, k), lambda it, s: (s[S_XTILE, it], 0))]
                 + [pl.BlockSpec(memory_space=pl.ANY)] * n_w,
        out_specs=pl.BlockSpec((MOE_TM, tn), lambda it, s: (s[S_OTILE, it], s[S_OCHUNK, it])),
        scratch_shapes=[pltpu.VMEM((k, tn), F32)] * n_w + [pltpu.VMEM((k, tn), BF16)] * n_w
                       + [pltpu.SemaphoreType.DMA((n_w,))],
    )
    return pl.pallas_call(
        functools.partial(_grouped_kernel, n_w=n_w, layer=layer, tn=tn, swiglu=swiglu),
        grid_spec=grid_spec,
        out_shape=jax.ShapeDtypeStruct((p, n), out_dtype),
        compiler_params=_params(("arbitrary",)),
        name=name,
    )(sched, x, *weights)


def _combine_kernel(pos_ref, res_ref, gate_ref, w_ref, y_ref, o_ref, buf, sem, *, tr):
    def copy(t, k):
        return pltpu.make_async_copy(y_ref.at[pl.ds(pos_ref[k, t], 1)], buf.at[k, pl.ds(t, 1)], sem)

    def start(t, c):
        copy(t, 0).start()
        copy(t, 1).start()
        return c

    def wait(t, c):
        copy(t, 0).wait()
        copy(t, 1).wait()
        return c

    lax.fori_loop(0, tr, start, 0)
    lax.fori_loop(0, tr, wait, 0)
    w = w_ref[...]
    y = w[:, 0:1] * buf[0] + w[:, 1:2] * buf[1]
    o_ref[...] = res_ref[...] + gate_ref[...] * y


def _combine_call(pos, res, gate, w, y, *, per_row, tr, rows_per_mod=None):
    rows, d = res.shape
    gspec = _mod_specs(per_row, tr, d, rows_per_mod)
    return pl.pallas_call(
        functools.partial(_combine_kernel, tr=tr),
        grid=(rows // tr,),
        in_specs=[pl.BlockSpec((2, tr), lambda i: (0, i), memory_space=pltpu.SMEM),
                  pl.BlockSpec((tr, d), lambda i: (i, 0)), gspec,
                  pl.BlockSpec((tr, 2), lambda i: (i, 0)),
                  pl.BlockSpec(memory_space=pl.ANY)],
        out_specs=pl.BlockSpec((tr, d), lambda i: (i, 0)),
        out_shape=jax.ShapeDtypeStruct((rows, d), F32),
        scratch_shapes=[pltpu.VMEM((2, tr, d), F32), pltpu.SemaphoreType.DMA],
        compiler_params=_params(("arbitrary",)),
        name="moe_combine",
    )(pos, res, gate, w, y)


def _moe_schedule(counts, max_tiles, n_chunks):
    tiles_per_e = (counts + MOE_TM - 1) // MOE_TM
    tile_end = jnp.cumsum(tiles_per_e)
    tile_start = tile_end - tiles_per_e
    n_items = n_chunks * max_tiles
    item_start = tile_start * n_chunks
    item_end = tile_end * n_chunks
    it = jnp.arange(n_items, dtype=I32)
    ex = jnp.minimum(jnp.sum((it[:, None] >= item_end[None, :]).astype(I32), axis=1), N_EXPERTS - 1)
    valid = (it < item_end[-1]).astype(I32)
    local = it - item_start[ex]
    t_e = jnp.maximum(tiles_per_e[ex], 1)
    chunk = local // t_e
    tile = tile_start[ex] + local % t_e
    idx = local % t_e
    first = (idx == 0).astype(I32) * valid
    full = (counts[ex] - idx * MOE_TM > MOE_HALF).astype(I32)
    nxt = jnp.minimum(it + t_e, n_items - 1)
    has_next = first * (it + t_e < item_end[-1]).astype(I32)
    n_valid = item_end[-1]
    last = jnp.maximum(n_valid - 1, 0)
    pick = lambda v: jnp.where(valid == 1, v, v[last])
    q = it - n_valid
    otile = jnp.where(valid == 1, tile, tile_end[-1] + q // n_chunks)
    ochunk = jnp.where(valid == 1, chunk, q % n_chunks)
    return jnp.stack([pick(tile), otile, pick(ex), pick(chunk), ochunk, first, valid, full,
                      has_next, ex[nxt], chunk[nxt]]).astype(I32)


def _moe(hp, hs, rp, rs, counts, w_gate, w_up, w_down, layer):
    n_assign = 2 * (hp.shape[0] + hs.shape[0])
    max_tiles = (n_assign + N_EXPERTS * (MOE_TM - 1)) // MOE_TM
    tiles_per_e = (counts + MOE_TM - 1) // MOE_TM
    tile_end = jnp.cumsum(tiles_per_e)
    row_start = (tile_end - tiles_per_e) * MOE_TM

    def bucket_rows(r):
        start = sum(jnp.where(r[:2] == e, row_start[e], 0) for e in range(N_EXPERTS))
        return start + r[2:4]

    pos_p = bucket_rows(rp)
    pos_s = bucket_rows(rs)
    half0 = row_start // MOE_HALF
    n_half = 2 * tiles_per_e
    last_real = (counts + MOE_HALF - 1) // MOE_HALF - 1
    unused = 2 * tile_end[-1] + jnp.arange(2 * N_EXPERTS, dtype=I32)
    zhalf = jnp.concatenate([half0 + last_real, half0 + n_half - 1, unused])
    zflag = jnp.concatenate([counts % MOE_HALF != 0, last_real < n_half - 1, unused < 2 * max_tiles])
    ztiles = jnp.stack([jnp.clip(zhalf, 0, 2 * max_tiles - 1), zflag.astype(I32)]).astype(I32)
    gu_chunks, dn_chunks = 3, 2
    xs = _bucket_call(pos_p, pos_s, ztiles, hp, hs, max_tiles * MOE_TM)
    he = _grouped_call(_moe_schedule(counts, max_tiles, gu_chunks), xs, [w_gate, w_up], layer,
                       tn=w_gate.shape[3] // gu_chunks, out_dtype=BF16, swiglu=True, name="moe_gate_up")
    y = _grouped_call(_moe_schedule(counts, max_tiles, dn_chunks), he, [w_down], layer,
                      tn=w_down.shape[3] // dn_chunks, out_dtype=F32, swiglu=False, name="moe_down")
    return y, pos_p, pos_s


def kernel(x_prompt, x_sample, cache_a_kv0, cache_a_kv1, cache_a_kv2, state_b_h, state_b_conv, state_c_conv,
           c_prompt, c_sample, w_mod, b_mod, norm_g, w_in0, qk_g, b_conv_w, b_conv_b, b_wa, b_ba, b_wi, b_bi,
           b_lambda, w_out0, w_in1, c_conv_w, w_out1, w_router, b_router, w_gate, w_up, w_down):
    batch, seq, d = x_prompt.shape
    dec_batch = x_sample.shape[0]
    depth = w_mod.shape[0]
    tp = batch * seq
    width = b_lambda.shape[1]
    pad_s = SAMPLE_ROWS - dec_batch
    caches = (cache_a_kv0, cache_a_kv1, cache_a_kv2)

    def pad_rows(v):
        return jnp.pad(v, ((0, pad_s),) + ((0, 0),) * (v.ndim - 1))

    res_p = x_prompt.reshape(tp, d)
    res_s = pad_rows(x_sample.reshape(dec_batch, d))
    c_all = jnp.concatenate([c_prompt, jnp.zeros((SUBLANES - batch, d), F32), pad_rows(c_sample)], axis=0)
    mod = _mod_call(c_all, w_mod, b_mod)
    slopes = 2.0 ** (-8.0 * np.arange(1, A_HEADS + 1) / A_HEADS)
    slopes8 = jnp.asarray(np.repeat(slopes[:, None], HEAD_DIM, axis=1), dtype=F32)
    w_router_t = w_router.T

    new_kv_p = [[], [], []]
    new_kv_s = [[], [], []]
    bh_p, bconv_p, cconv_p, bh_s, bconv_s, cconv_s = [], [], [], [], [], []

    for layer in range(depth):
        li = layer // 2
        m = mod[layer]
        chunks_p = [m[:batch, k * d:(k + 1) * d].reshape(batch, 1, d) for k in range(6)]
        chunks_s = [m[SUBLANES:, k * d:(k + 1) * d] for k in range(6)]
        sh1p, sc1p, g1p, sh2p, sc2p, g2p = chunks_p
        sh1s, sc1s, g1s, sh2s, sc2s, g2s = chunks_s

        h_p = _norm_call(res_p, norm_g[layer, 0], sc1p, sh1p, per_row=False, tr=256, rows_per_mod=seq)
        h_s = _norm_call(res_s, norm_g[layer, 0], sc1s, sh1s, per_row=True, tr=SAMPLE_ROWS)

        if layer % 2 == 0:
            z_p, z_s = _dense_call(h_p, h_s, w_in0, li)
            ya_p, kn_p = _attn_prompt_call(z_p, qk_g[li, 0], qk_g[li, 1], slopes8.reshape(A_HEADS, 1, HEAD_DIM),
                                           batch=batch, seq=seq)
            for g, (window, _) in enumerate(DILATIONS):
                keep = min(window, seq)
                kk = kn_p[g].reshape(batch, seq, A_HEADS, HEAD_DIM)[:, seq - keep:]
                v0 = g * 3 * A_WIDTH + 2 * A_WIDTH
                vv = z_p[:, v0:v0 + A_WIDTH].reshape(batch, seq, A_HEADS, HEAD_DIM)[:, seq - keep:]
                new_kv_p[g].append(jnp.stack([kk, vv], axis=2))
            yb_p, hl_p = _lru_prompt_call(z_p, b_conv_w[li], b_conv_b[li], b_wa[li], b_ba[li], b_wi[li],
                                          b_bi[li], b_lambda[li], batch=batch, seq=seq)
            bh_p.append(hl_p[:, SUBLANES - 1])
            xb_p = z_p[:, A_QKV:A_QKV + width].reshape(batch, seq, width)
            bconv_p.append(xb_p[:, seq - (B_CONV - 1):])

            ya_s, kvnew_s = _attn_sample_call(z_s, [c[li] for c in caches], qk_g[li, 0], qk_g[li, 1], slopes8,
                                              dec_batch=dec_batch)
            for g in range(N_DIL):
                new_kv_s[g].append(jnp.concatenate([caches[g][li][:, 1:], kvnew_s[:, g][:, None]], axis=1))
            xb_s = z_s[:, A_QKV:A_QKV + width]
            gb_s = z_s[:, A_QKV + width:]
            buf = jnp.transpose(pad_rows(state_b_conv[li]), (1, 0, 2))
            yb_s, hnew_s = _lru_sample_call(xb_s, gb_s, buf, pad_rows(state_b_h[li]), b_conv_w[li], b_conv_b[li],
                                            b_wa[li], b_ba[li], b_wi[li], b_bi[li], b_lambda[li])
            bh_s.append(hnew_s[:dec_batch])
            bconv_s.append(jnp.concatenate([state_b_conv[li][:, 1:], xb_s[:dec_batch, None]], axis=1))

            mix_p = jnp.concatenate([ya_p, yb_p], axis=1)
            mix_s = jnp.concatenate([pad_rows(ya_s.reshape(dec_batch, A_WIDTH)).astype(BF16), yb_s], axis=1)
            w_out = w_out0
        else:
            z_p, z_s = _dense_call(h_p, h_s, w_in1, li)
            mix_p, ut_p = _gconv_prompt_call(z_p, c_conv_w[li], batch=batch, seq=seq)
            cconv_p.append(ut_p[:, SUBLANES - (C_CONV - 1):])
            buf = jnp.transpose(pad_rows(state_c_conv[li]), (1, 0, 2))
            mix_s, u_s = _gconv_sample_call(z_s, buf, c_conv_w[li])
            cconv_s.append(jnp.concatenate([state_c_conv[li][:, 1:], u_s[:dec_batch, None]], axis=1))
            w_out = w_out1

        res_p, res_s = _dense_call(mix_p, mix_s, w_out, li, res=(res_p, g1p, res_s, g1s), rows_per_mod=seq)

        hp, ep, wp, cnt = _norm_route_call(res_p, norm_g[layer, 1], sc2p, sh2p, w_router_t, b_router,
                                           jnp.zeros((N_EXPERTS, LANES), F32),
                                           per_row=False, tr=256, rows_per_mod=seq)
        hs, es, ws, cnt = _norm_route_call(res_s, norm_g[layer, 1], sc2s, sh2s, w_router_t, b_router, cnt,
                                           per_row=True, tr=SAMPLE_ROWS)
        y, pos_p, pos_s = _moe(hp, hs, ep[:4], es[:4], cnt[:, 0].astype(I32), w_gate, w_up, w_down, layer)
        res_p = _combine_call(pos_p, res_p, g2p, wp[:2].T, y, per_row=False, tr=256, rows_per_mod=seq)
        res_s = _combine_call(pos_s, res_s, g2s, ws[:2].T, y, per_row=True, tr=SAMPLE_ROWS)

    y_prompt = res_p.reshape(batch, seq, d)
    y_sample = res_s[:dec_batch].reshape(dec_batch, 1, d)
    st = lambda xs: jnp.stack(xs, axis=0)
    return (y_prompt, y_sample, st(new_kv_p[0]), st(new_kv_p[1]), st(new_kv_p[2]), st(bh_p), st(bconv_p),
            st(cconv_p), st(new_kv_s[0]), st(new_kv_s[1]), st(new_kv_s[2]), st(bh_s), st(bconv_s), st(cconv_s))
```

```python
import functools

import numpy as np
import jax
import jax.numpy as jnp
from jax import lax
from jax.experimental import pallas as pl
from jax.experimental.pallas import tpu as pltpu

F32 = jnp.float32
BF16 = jnp.bfloat16
I32 = jnp.int32
U32 = jnp.uint32

HEAD_DIM = 128
A_HEADS = 8
DILATIONS = ((128, 1), (512, 4), (2048, 16))
N_DIL = len(DILATIONS)
A_WIDTH = A_HEADS * HEAD_DIM
A_QKV = N_DIL * 3 * A_WIDTH
ATTN_SCALE = HEAD_DIM ** -0.5
QB = 128
B_BLOCKS = 8
B_CONV = 4
C_CONV = 3
LRU_C = 8.0
N_EXPERTS = 16
N_GROUPS = 4
EXP_PER_GROUP = N_EXPERTS // N_GROUPS
EPS = 1e-6
SAMPLE_ROWS = 16
SUBLANES = 8
LANES = 128
NEG_INF = float("-inf")

VMEM_LIMIT = 52 * 1024 * 1024


def _params(sem, vmem=VMEM_LIMIT):
    return pltpu.CompilerParams(dimension_semantics=sem, vmem_limit_bytes=vmem)


def _mod_kernel(c_ref, w_ref, b_ref, o_ref):
    c = c_ref[...]
    cs = (c * jax.nn.sigmoid(c)).astype(BF16)
    o_ref[...] = jnp.dot(cs, w_ref[...].astype(BF16), preferred_element_type=F32) + b_ref[...]


def _mod_call(c_all, w_mod, b_mod):
    depth, d, n = w_mod.shape
    rows = c_all.shape[0]
    tn = 512
    return pl.pallas_call(
        _mod_kernel,
        grid=(depth, n // tn),
        in_specs=[
            pl.BlockSpec((rows, d), lambda l, j: (0, 0)),
            pl.BlockSpec((None, d, tn), lambda l, j: (l, 0, j)),
            pl.BlockSpec((None, 1, tn), lambda l, j: (l, 0, j)),
        ],
        out_specs=pl.BlockSpec((None, rows, tn), lambda l, j: (l, 0, j)),
        out_shape=jax.ShapeDtypeStruct((depth, rows, n), F32),
        compiler_params=_params(("arbitrary", "arbitrary")),
        name="mod_proj",
    )(c_all, w_mod, b_mod.reshape(depth, 1, n))


def _modnorm(x, g, sc, sh):
    h = x * lax.rsqrt(jnp.mean(x * x, axis=-1, keepdims=True) + EPS) * g
    return h * (1.0 + sc) + sh


def _norm_kernel(x_ref, g_ref, sc_ref, sh_ref, o_ref):
    o_ref[...] = _modnorm(x_ref[...], g_ref[...], sc_ref[...], sh_ref[...]).astype(o_ref.dtype)


def _route(logits_t, bias_ref):
    score = [jax.nn.sigmoid(logits_t[e:e + 1, :]) for e in range(N_EXPERTS)]
    sel = [score[e] + bias_ref[e] for e in range(N_EXPERTS)]
    gsum = []
    for g in range(N_GROUPS):
        a, b, c, d = sel[4 * g:4 * g + 4]
        hi1, lo1 = jnp.maximum(a, b), jnp.minimum(a, b)
        hi2, lo2 = jnp.maximum(c, d), jnp.minimum(c, d)
        top = jnp.maximum(hi1, hi2)
        second = jnp.maximum(jnp.minimum(hi1, hi2), jnp.maximum(lo1, lo2))
        gsum.append(top + second)
    best = gsum[0]
    grp = jnp.zeros_like(best, dtype=I32)
    for g in range(1, N_GROUPS):
        better = gsum[g] > best
        best = jnp.where(better, gsum[g], best)
        grp = jnp.where(better, g, grp)
    v1 = jnp.full_like(best, NEG_INF)
    e1 = jnp.zeros_like(grp)
    for e in range(N_EXPERTS):
        cand = jnp.where(grp == e // EXP_PER_GROUP, sel[e], NEG_INF)
        better = cand > v1
        v1 = jnp.where(better, cand, v1)
        e1 = jnp.where(better, e, e1)
    v2 = jnp.full_like(best, NEG_INF)
    e2 = jnp.zeros_like(grp)
    for e in range(N_EXPERTS):
        cand = jnp.where((grp == e // EXP_PER_GROUP) & (e1 != e), sel[e], NEG_INF)
        better = cand > v2
        v2 = jnp.where(better, cand, v2)
        e2 = jnp.where(better, e, e2)
    s1 = jnp.zeros_like(best)
    s2 = jnp.zeros_like(best)
    for e in range(N_EXPERTS):
        s1 = jnp.where(e1 == e, score[e], s1)
        s2 = jnp.where(e2 == e, score[e], s2)
    tot = s1 + s2
    return e1, e2, s1 / tot, s2 / tot


def _norm_route_kernel(bias_ref, x_ref, g_ref, sc_ref, sh_ref, wr_ref, cnt0_ref,
                       o_ref, e_ref, w_ref, cnt_ref, run_ref):
    @pl.when(pl.program_id(0) == 0)
    def _():
        run_ref[...] = cnt0_ref[...]

    h = _modnorm(x_ref[...], g_ref[...], sc_ref[...], sh_ref[...])
    o_ref[...] = h
    nt = (((1,), (1,)), ((), ()))
    logits_t = lax.dot_general(wr_ref[...].astype(BF16), h.astype(BF16), nt, preferred_element_type=F32)
    e1, e2, w1, w2 = _route(logits_t, bias_ref)
    n = e1.shape[1]
    expert = lax.broadcasted_iota(I32, (N_EXPERTS, n), 0)
    oh1 = (expert == e1).astype(F32)
    oh2 = (expert == e2).astype(F32)
    before = (lax.broadcasted_iota(I32, (n, n), 0) < lax.broadcasted_iota(I32, (n, n), 1)).astype(BF16)
    pre1 = jnp.dot(oh1.astype(BF16), before, preferred_element_type=F32)
    pre2 = jnp.dot(oh2.astype(BF16), before, preferred_element_type=F32)
    tot1 = jnp.sum(oh1, axis=1, keepdims=True)
    tot2 = jnp.sum(oh2, axis=1, keepdims=True)
    base = run_ref[...][:, :1]
    r1 = jnp.sum(oh1 * (base + pre1), axis=0, keepdims=True)
    r2 = jnp.sum(oh2 * (base + tot1 + pre2), axis=0, keepdims=True)
    run_ref[...] = run_ref[...] + (tot1 + tot2)
    cnt_ref[...] = run_ref[...]
    e_ref[...] = jnp.concatenate([e1, e2, r1.astype(I32), r2.astype(I32), jnp.zeros((SUBLANES - 4, n), I32)], axis=0)
    w_ref[...] = jnp.concatenate([w1, w2, jnp.zeros((SUBLANES - 2, n), F32)], axis=0)


def _mod_specs(per_row, tr, d, rows_per_mod):
    if per_row:
        spec = pl.BlockSpec((tr, d), lambda i, *_: (i, 0))
    else:
        spec = pl.BlockSpec((None, 1, d), lambda i, *_: (i * tr // rows_per_mod, 0, 0))
    return spec


def _norm_call(x, g, sc, sh, *, per_row, tr, rows_per_mod=None):
    rows, d = x.shape
    mspec = _mod_specs(per_row, tr, d, rows_per_mod)
    return pl.pallas_call(
        _norm_kernel,
        grid=(rows // tr,),
        in_specs=[pl.BlockSpec((tr, d), lambda i: (i, 0)),
                  pl.BlockSpec((1, d), lambda i: (0, 0)), mspec, mspec],
        out_specs=pl.BlockSpec((tr, d), lambda i: (i, 0)),
        out_shape=jax.ShapeDtypeStruct((rows, d), BF16),
        compiler_params=_params(("arbitrary",)),
        name="modnorm",
    )(x, g.reshape(1, d), sc, sh)


def _norm_route_call(x, g, sc, sh, w_router_t, b_router, cnt0, *, per_row, tr, rows_per_mod=None):
    rows, d = x.shape
    mspec = _mod_specs(per_row, tr, d, rows_per_mod)
    cspec = pl.BlockSpec((N_EXPERTS, LANES), lambda i, b: (0, 0))
    grid_spec = pltpu.PrefetchScalarGridSpec(
        num_scalar_prefetch=1,
        grid=(rows // tr,),
        in_specs=[pl.BlockSpec((tr, d), lambda i, b: (i, 0)),
                  pl.BlockSpec((1, d), lambda i, b: (0, 0)), mspec, mspec,
                  pl.BlockSpec((N_EXPERTS, d), lambda i, b: (0, 0)), cspec],
        out_specs=[pl.BlockSpec((tr, d), lambda i, b: (i, 0)),
                   pl.BlockSpec((SUBLANES, tr), lambda i, b: (0, i)),
                   pl.BlockSpec((SUBLANES, tr), lambda i, b: (0, i)), cspec],
        scratch_shapes=[pltpu.VMEM((N_EXPERTS, LANES), F32)],
    )
    return pl.pallas_call(
        _norm_route_kernel,
        grid_spec=grid_spec,
        out_shape=[jax.ShapeDtypeStruct((rows, d), F32),
                   jax.ShapeDtypeStruct((SUBLANES, rows), I32),
                   jax.ShapeDtypeStruct((SUBLANES, rows), F32),
                   jax.ShapeDtypeStruct((N_EXPERTS, LANES), F32)],
        compiler_params=_params(("arbitrary",)),
        name="modnorm_route",
    )(b_router, x, g.reshape(1, d), sc, sh, w_router_t, cnt0)


def _dense_kernel(*refs, residual):
    if residual:
        (xp_ref, xs_ref, w_ref, rp_ref, gp_ref, rs_ref, gs_ref, op_ref, os_ref, wb_ref) = refs
    else:
        (xp_ref, xs_ref, w_ref, op_ref, os_ref, wb_ref) = refs

    @pl.when(pl.program_id(1) == 0)
    def _():
        wb_ref[...] = w_ref[...].astype(BF16)
        acc_s = jnp.dot(xs_ref[...], wb_ref[...], preferred_element_type=F32)
        if residual:
            acc_s = rs_ref[...] + gs_ref[...] * acc_s
        os_ref[...] = acc_s.astype(os_ref.dtype)

    acc = jnp.dot(xp_ref[...], wb_ref[...], preferred_element_type=F32)
    if residual:
        acc = rp_ref[...] + gp_ref[...] * acc
    op_ref[...] = acc.astype(op_ref.dtype)


def _dense_call(xp, xs, w, li, *, res=None, rows_per_mod=None, out_dtype=F32, tm=1024, tn=512):
    tp, k = xp.shape
    n = w.shape[2]
    rs = xs.shape[0]
    in_specs = [pl.BlockSpec((tm, k), lambda j, i: (i, 0)),
                pl.BlockSpec((rs, k), lambda j, i: (0, 0)),
                pl.BlockSpec((None, k, tn), lambda j, i: (li, 0, j))]
    args = [xp, xs, w]
    if res is not None:
        in_specs += [pl.BlockSpec((tm, tn), lambda j, i: (i, j)),
                     pl.BlockSpec((None, 1, tn), lambda j, i: (i * tm // rows_per_mod, 0, j)),
                     pl.BlockSpec((rs, tn), lambda j, i: (0, j)),
                     pl.BlockSpec((rs, tn), lambda j, i: (0, j))]
        args += list(res)
    return pl.pallas_call(
        functools.partial(_dense_kernel, residual=res is not None),
        grid=(n // tn, tp // tm),
        in_specs=in_specs,
        out_specs=[pl.BlockSpec((tm, tn), lambda j, i: (i, j)),
                   pl.BlockSpec((rs, tn), lambda j, i: (0, j))],
        out_shape=[jax.ShapeDtypeStruct((tp, n), out_dtype), jax.ShapeDtypeStruct((rs, n), out_dtype)],
        scratch_shapes=[pltpu.VMEM((k, tn), BF16)],
        compiler_params=_params(("arbitrary", "arbitrary")),
        name="dense_proj",
    )(*args)


def _rms_rows(x, g):
    return x * lax.rsqrt(jnp.mean(x * x, axis=-1, keepdims=True) + EPS) * g


def _attn_prompt_kernel(q0, k0, v0, q1, k1, v1, q2, k2, v2, gq_ref, gk_ref, slope_ref,
                        ya_ref, kv0, kv1, kv2, qn_s, m_s, l_s, acc_s, kn0, kn1, kn2, kv_sem, *, seq):
    qkv = ((q0, k0, v0, kn0), (q1, k1, v1, kn1), (q2, k2, v2, kn2))
    kv_out = (kv0, kv1, kv2)
    b_idx = pl.program_id(0)
    h_idx = pl.program_id(1)

    def kv_copy(g, c):
        keep = kv_out[g].shape[1]
        src = (qkv[g][3], qkv[g][2])[c].at[pl.ds(seq - keep, keep)]
        dst = kv_out[g].at[b_idx, pl.ds(0, keep), c, h_idx, pl.ds(0, HEAD_DIM)]
        return pltpu.make_async_copy(src, dst, kv_sem.at[2 * g + c])

    slope = slope_ref[...]
    row = lax.broadcasted_iota(I32, (QB, 2 * QB), 0)
    col = lax.broadcasted_iota(I32, (QB, 2 * QB), 1)
    rel_prev = row + QB - col
    rel_first = (row - col)[:, :QB]
    nt = (((1,), (1,)), ((), ()))

    for g, (window, dil) in enumerate(DILATIONS):
        q_ref, k_ref, v_ref, kn_ref = qkv[g]
        nk = window // dil
        n_blk = seq // dil // QB
        qn_s[...] = _rms_rows(q_ref[...], gq_ref[g])
        kn_ref[...] = _rms_rows(k_ref[...], gk_ref[g])
        kv_copy(g, 0).start()
        kv_copy(g, 1).start()
        sl = slope[:, :1] * float(dil)
        bias_prev = jnp.where((rel_prev >= 0) & (rel_prev <= nk), -sl * rel_prev.astype(F32), NEG_INF)
        bias_first = jnp.where(rel_first >= 0, -sl * rel_first.astype(F32), NEG_INF)

        def rows(start, size, dil=dil):
            return pl.ds(start, size) if dil == 1 else pl.ds(start, size, stride=dil)

        def block(qstart, kstart, first, g=g, kn_ref=kn_ref, v_ref=v_ref, rows=rows,
                  bias_prev=bias_prev, bias_first=bias_first):
            nkeys = QB if first else 2 * QB
            qb = qn_s[rows(qstart, QB), :].astype(BF16)
            kb = kn_ref[rows(kstart, nkeys), :].astype(BF16)
            vb = v_ref[rows(kstart, nkeys), :].astype(BF16)
            s = lax.dot_general(qb, kb, nt, preferred_element_type=F32) * ATTN_SCALE
            s = s + (bias_first if first else bias_prev)
            m_blk = jnp.max(s, axis=-1, keepdims=True)
            p = jnp.exp(s - m_blk)
            l_blk = jnp.sum(p, axis=-1, keepdims=True)
            o_blk = jnp.dot(p.astype(BF16), vb, preferred_element_type=F32)
            qrows = rows(qstart, QB)
            if g == 0:
                m_s[qrows, :] = jnp.broadcast_to(m_blk, (QB, HEAD_DIM))
                l_s[qrows, :] = jnp.broadcast_to(l_blk, (QB, HEAD_DIM))
                acc_s[qrows, :] = o_blk
            else:
                m_old = m_s[qrows, :]
                m_new = jnp.maximum(m_old, m_blk)
                a_old = jnp.exp(m_old - m_new)
                a_blk = jnp.exp(m_blk - m_new)
                m_s[qrows, :] = m_new
                l_s[qrows, :] = l_s[qrows, :] * a_old + l_blk * a_blk
                acc_s[qrows, :] = acc_s[qrows, :] * a_old + o_blk * a_blk

        if dil == 1:
            block(0, 0, True)

            def body(n, carry, block=block):
                qstart = pl.multiple_of(n * QB, QB)
                block(qstart, pl.multiple_of(qstart - QB, QB), False)
                return carry

            lax.fori_loop(1, n_blk, body, 0)
        else:
            def body(r, carry, block=block, n_blk=n_blk, dil=dil):
                block(r, r, True)
                for n in range(1, n_blk):
                    block(r + n * QB * dil, r + (n - 1) * QB * dil, False)
                return carry

            lax.fori_loop(0, dil, body, 0)

    ya_ref[...] = (acc_s[...] / l_s[...]).astype(ya_ref.dtype)
    for g in range(N_DIL):
        kv_copy(g, 0).wait()
        kv_copy(g, 1).wait()


def _attn_prompt_call(z, gq, gk, slopes, *, batch, seq):
    t = batch * seq
    kv_shapes = [jax.ShapeDtypeStruct((batch, min(window, seq), 2, A_HEADS, HEAD_DIM), F32)
                 for window, _ in DILATIONS]

    def col(g, c):
        return lambda b, h: (b, g * 3 * A_HEADS + c * A_HEADS + h)

    in_specs = []
    for g in range(N_DIL):
        for c in range(3):
            in_specs.append(pl.BlockSpec((seq, HEAD_DIM), col(g, c)))
    gspec = pl.BlockSpec((N_DIL, None, 1, HEAD_DIM), lambda b, h: (0, h, 0, 0))
    in_specs += [gspec, gspec, pl.BlockSpec((None, 1, HEAD_DIM), lambda b, h: (h, 0, 0))]
    outs = pl.pallas_call(
        functools.partial(_attn_prompt_kernel, seq=seq),
        grid=(batch, A_HEADS),
        in_specs=in_specs,
        out_specs=[pl.BlockSpec((seq, HEAD_DIM), lambda b, h: (b, h))] + [pl.BlockSpec(memory_space=pl.ANY)] * 3,
        out_shape=[jax.ShapeDtypeStruct((t, A_WIDTH), BF16)] + kv_shapes,
        scratch_shapes=[pltpu.VMEM((seq, HEAD_DIM), F32)] * 7 + [pltpu.SemaphoreType.DMA((2 * N_DIL,))],
        compiler_params=_params(("arbitrary", "arbitrary")),
        name="attn_prompt",
    )(*([z] * 9), gq.reshape(N_DIL, A_HEADS, 1, HEAD_DIM), gk.reshape(N_DIL, A_HEADS, 1, HEAD_DIM), slopes)
    return outs[0], outs[1:]


def _round_bf16(x):
    return x.astype(BF16).astype(F32)


def _attn_sample_kernel(z_ref, c0_ref, c1_ref, c2_ref, gq_ref, gk_ref, slope_ref, ya_ref, new_ref):
    caches = (c0_ref, c1_ref, c2_ref)
    slope = slope_ref[...][:, :1]
    nk = DILATIONS[0][0] // DILATIONS[0][1]
    steps = (nk - lax.broadcasted_iota(I32, (nk, 1, 1), 0)).astype(F32)
    m_run = l_run = acc = None
    for g, (window, dil) in enumerate(DILATIONS):
        base = g * 3 * A_HEADS
        q = _rms_rows(z_ref[base:base + A_HEADS, :], gq_ref[g])
        k = _rms_rows(z_ref[base + A_HEADS:base + 2 * A_HEADS, :], gk_ref[g])
        v = z_ref[base + 2 * A_HEADS:base + 3 * A_HEADS, :]
        new_ref[g, 0] = k
        new_ref[g, 1] = v
        qr = _round_bf16(q)
        ck = _round_bf16(caches[g][:, 0])
        cv = _round_bf16(caches[g][:, 1])
        s_c = jnp.sum(ck * qr[None], axis=-1, keepdims=True) * ATTN_SCALE
        s_c = s_c - (slope * float(dil))[None] * steps
        s_n = jnp.sum(_round_bf16(k) * qr, axis=-1, keepdims=True) * ATTN_SCALE
        m_g = jnp.maximum(jnp.max(s_c, axis=0), s_n)
        p_c = jnp.exp(s_c - m_g[None])
        p_n = jnp.exp(s_n - m_g)
        l_g = jnp.sum(p_c, axis=0) + p_n
        o_g = jnp.sum(_round_bf16(p_c) * cv, axis=0) + _round_bf16(p_n) * _round_bf16(v)
        if g == 0:
            m_run, l_run, acc = m_g, l_g, o_g
        else:
            m_new = jnp.maximum(m_run, m_g)
            a_old = jnp.exp(m_run - m_new)
            a_g = jnp.exp(m_g - m_new)
            l_run = l_run * a_old + l_g * a_g
            acc = acc * a_old + o_g * a_g
            m_run = m_new
    ya_ref[...] = acc / l_run


def _attn_sample_call(z_s, caches, gq, gk, slopes8, *, dec_batch):
    n_colblk = z_s.shape[1] // HEAD_DIM
    z3 = z_s.reshape(z_s.shape[0], n_colblk, HEAD_DIM)
    in_specs = [pl.BlockSpec((None, n_colblk, HEAD_DIM), lambda b: (b, 0, 0))]
    args = [z3]
    for g, (window, dil) in enumerate(DILATIONS):
        nk = window // dil
        c = caches[g].reshape(dec_batch, nk, dil, 2, A_HEADS, HEAD_DIM)
        in_specs.append(pl.BlockSpec((None, nk, None, 2, A_HEADS, HEAD_DIM), lambda b: (b, 0, 0, 0, 0, 0)))
        args.append(c)
    full = lambda shape: pl.BlockSpec(shape, lambda b: (0,) * len(shape))
    in_specs += [full((N_DIL, A_HEADS, HEAD_DIM)), full((N_DIL, A_HEADS, HEAD_DIM)), full((A_HEADS, HEAD_DIM))]
    return pl.pallas_call(
        _attn_sample_kernel,
        grid=(dec_batch,),
        in_specs=in_specs,
        out_specs=[pl.BlockSpec((None, A_HEADS, HEAD_DIM), lambda b: (b, 0, 0)),
                   pl.BlockSpec((None, N_DIL, 2, A_HEADS, HEAD_DIM), lambda b: (b, 0, 0, 0, 0))],
        out_shape=[jax.ShapeDtypeStruct((dec_batch, A_HEADS, HEAD_DIM), F32),
                   jax.ShapeDtypeStruct((dec_batch, N_DIL, 2, A_HEADS, HEAD_DIM), F32)],
        compiler_params=_params(("arbitrary",)),
        name="attn_sample",
    )(*args, gq, gk, slopes8)


def _softplus(x):
    return jnp.maximum(x, 0.0) + jnp.log1p(jnp.exp(-jnp.abs(x)))


def _lru_coeffs(xc, wa, ba, wi, bi, lam):
    xcb = xc.astype(BF16)
    r = jax.nn.sigmoid(jnp.dot(xcb, wa.astype(BF16), preferred_element_type=F32) + ba)
    i = jax.nn.sigmoid(jnp.dot(xcb, wi.astype(BF16), preferred_element_type=F32) + bi)
    log_a = -LRU_C * r * _softplus(-lam)
    a = jnp.exp(log_a)
    t = jnp.tanh(log_a)
    one_minus_a2 = -2.0 * t / (1.0 - t)
    return a, jnp.sqrt(one_minus_a2) * (i * xc)


def _lru_prompt_kernel(x_ref, gate_ref, cw_ref, cb_ref, wa_ref, ba_ref, wi_ref, bi_ref, lam_ref,
                       y_ref, hl_ref, pad_a, pad_b, *, seq):
    blk = x_ref.shape[1]
    off = SUBLANES
    x = x_ref[...]
    pad_a[0:off, :] = jnp.zeros((off, blk), F32)
    pad_a[off:off + seq, :] = x
    xc = pad_a[off - 3:off - 3 + seq, :] * cw_ref[0:1, :]
    xc = xc + pad_a[off - 2:off - 2 + seq, :] * cw_ref[1:2, :]
    xc = xc + pad_a[off - 1:off - 1 + seq, :] * cw_ref[2:3, :]
    xc = xc + x * cw_ref[3:4, :] + cb_ref[...]
    a, b = _lru_coeffs(xc, wa_ref[...], ba_ref[...], wi_ref[...], bi_ref[...], lam_ref[...])

    tpos = lax.broadcasted_iota(I32, (seq, blk), 0) & (SUBLANES - 1)
    pad_b[0:off, :] = jnp.zeros((off, blk), F32)
    for o in (1, 2, 4):
        pad_a[off:off + seq, :] = a
        pad_b[off:off + seq, :] = b
        keep = tpos >= o
        a_prev = jnp.where(keep, pad_a[off - o:off - o + seq, :], 1.0)
        b_prev = jnp.where(keep, pad_b[off - o:off - o + seq, :], 0.0)
        b = a * b_prev + b
        a = a * a_prev
    pad_a[off:off + seq, :] = a
    pad_b[off:off + seq, :] = b

    def tile(i, h_prev):
        r0 = pl.multiple_of(off + i * SUBLANES, SUBLANES)
        h = pad_b[pl.ds(r0, SUBLANES), :] + pad_a[pl.ds(r0, SUBLANES), :] * h_prev
        pad_b[pl.ds(r0, SUBLANES), :] = h
        return jnp.broadcast_to(h[SUBLANES - 1:SUBLANES, :], (SUBLANES, blk))

    lax.fori_loop(0, seq // SUBLANES, tile, jnp.zeros((SUBLANES, blk), F32))
    h = pad_b[off:off + seq, :]
    hl_ref[...] = h[seq - SUBLANES:, :]
    y_ref[...] = (h * jax.nn.gelu(gate_ref[...])).astype(y_ref.dtype)


def _lru_prompt_call(z, cw, cb, wa, ba, wi, bi, lam, *, batch, seq):
    t = batch * seq
    nb, blk, _ = wa.shape
    width = nb * blk
    x0 = A_QKV // blk
    g0 = (A_QKV + width) // blk
    vec = lambda v: v.reshape(1, width)
    vspec = pl.BlockSpec((1, blk), lambda b, n: (0, n))
    return pl.pallas_call(
        functools.partial(_lru_prompt_kernel, seq=seq),
        grid=(batch, nb),
        in_specs=[pl.BlockSpec((seq, blk), lambda b, n: (b, x0 + n)),
                  pl.BlockSpec((seq, blk), lambda b, n: (b, g0 + n)),
                  pl.BlockSpec((B_CONV, blk), lambda b, n: (0, n)), vspec,
                  pl.BlockSpec((None, blk, blk), lambda b, n: (n, 0, 0)), vspec,
                  pl.BlockSpec((None, blk, blk), lambda b, n: (n, 0, 0)), vspec, vspec],
        out_specs=[pl.BlockSpec((seq, blk), lambda b, n: (b, n)),
                   pl.BlockSpec((None, SUBLANES, blk), lambda b, n: (b, 0, n))],
        out_shape=[jax.ShapeDtypeStruct((t, width), BF16),
                   jax.ShapeDtypeStruct((batch, SUBLANES, width), F32)],
        scratch_shapes=[pltpu.VMEM((seq + SUBLANES, blk), F32)] * 2,
        compiler_params=_params(("arbitrary", "arbitrary")),
        name="rglru_prompt",
    )(z, z, cw, vec(cb), wa, vec(ba), wi, vec(bi), vec(lam))


def _lru_sample_kernel(x_ref, gate_ref, buf_ref, h0_ref, cw_ref, cb_ref, wa_ref, ba_ref, wi_ref, bi_ref,
                       lam_ref, y_ref, h_ref):
    nb, blk, _ = wa_ref.shape
    for n in range(nb):
        cs = slice(n * blk, (n + 1) * blk)
        x = x_ref[:, cs]
        xc = buf_ref[0, :, cs] * cw_ref[0:1, cs]
        xc = xc + buf_ref[1, :, cs] * cw_ref[1:2, cs]
        xc = xc + buf_ref[2, :, cs] * cw_ref[2:3, cs]
        xc = xc + x * cw_ref[3:4, cs] + cb_ref[:, cs]
        a, b = _lru_coeffs(xc, wa_ref[n], ba_ref[:, cs], wi_ref[n], bi_ref[:, cs], lam_ref[:, cs])
        h = b + a * h0_ref[:, cs]
        h_ref[:, cs] = h
        y_ref[:, cs] = (h * jax.nn.gelu(gate_ref[:, cs])).astype(y_ref.dtype)


def _lru_sample_call(x, gate, buf, h0, cw, cb, wa, ba, wi, bi, lam):
    rows, width = x.shape
    vec = lambda v: v.reshape(1, width)
    return pl.pallas_call(
        _lru_sample_kernel,
        out_shape=[jax.ShapeDtypeStruct((rows, width), BF16), jax.ShapeDtypeStruct((rows, width), F32)],
        compiler_params=pltpu.CompilerParams(vmem_limit_bytes=VMEM_LIMIT),
        name="rglru_sample",
    )(x, gate, buf, h0, cw, vec(cb), wa, vec(ba), wi, vec(bi), vec(lam))


def _gconv_prompt_kernel(b_ref, c_ref, v_ref, cw_ref, m_ref, ut_ref, pad, *, seq):
    blk = b_ref.shape[1]
    off = SUBLANES
    u = c_ref[...] * v_ref[...]
    pad[0:off, :] = jnp.zeros((off, blk), F32)
    pad[off:off + seq, :] = u
    uc = pad[off - 2:off - 2 + seq, :] * cw_ref[0:1, :]
    uc = uc + pad[off - 1:off - 1 + seq, :] * cw_ref[1:2, :]
    uc = uc + u * cw_ref[2:3, :]
    m_ref[...] = (b_ref[...] * uc).astype(m_ref.dtype)
    ut_ref[...] = u[seq - SUBLANES:, :]


def _gconv_prompt_call(z, cw, *, batch, seq, blk=256):
    t = batch * seq
    d = z.shape[1] // 3
    nblk = d // blk
    return pl.pallas_call(
        functools.partial(_gconv_prompt_kernel, seq=seq),
        grid=(batch, nblk),
        in_specs=[pl.BlockSpec((seq, blk), lambda b, n: (b, n)),
                  pl.BlockSpec((seq, blk), lambda b, n: (b, nblk + n)),
                  pl.BlockSpec((seq, blk), lambda b, n: (b, 2 * nblk + n)),
                  pl.BlockSpec((C_CONV, blk), lambda b, n: (0, n))],
        out_specs=[pl.BlockSpec((seq, blk), lambda b, n: (b, n)),
                   pl.BlockSpec((None, SUBLANES, blk), lambda b, n: (b, 0, n))],
        out_shape=[jax.ShapeDtypeStruct((t, d), BF16), jax.ShapeDtypeStruct((batch, SUBLANES, d), F32)],
        scratch_shapes=[pltpu.VMEM((seq + SUBLANES, blk), F32)],
        compiler_params=_params(("arbitrary", "arbitrary")),
        name="gconv_prompt",
    )(z, z, z, cw)


def _gconv_sample_kernel(z_ref, buf_ref, cw_ref, m_ref, u_ref):
    d = m_ref.shape[1]
    u = z_ref[:, d:2 * d] * z_ref[:, 2 * d:3 * d]
    uc = buf_ref[0] * cw_ref[0:1, :] + buf_ref[1] * cw_ref[1:2, :]
    uc = uc + u * cw_ref[2:3, :]
    m_ref[...] = (z_ref[:, 0:d] * uc).astype(m_ref.dtype)
    u_ref[...] = u


def _gconv_sample_call(z_s, buf, cw):
    rows = z_s.shape[0]
    d = z_s.shape[1] // 3
    return pl.pallas_call(
        _gconv_sample_kernel,
        out_shape=[jax.ShapeDtypeStruct((rows, d), BF16), jax.ShapeDtypeStruct((rows, d), F32)],
        name="gconv_sample",
    )(z_s, buf, cw)


MOE_TM = 512
MOE_HALF = MOE_TM // 2


def _bucket_row(route_ref, start_ref, k, t):
    return start_ref[0, route_ref[k, t]] + route_ref[2 + k, t]


def _bucket_kernel(rp_ref, rs_ref, start_ref, ztile_ref, hp_ref, hs_ref, xs_ref, zero_buf, sem, zsem):
    step = pl.program_id(0)
    tr = hp_ref.shape[0]
    n_s = rs_ref.shape[1]

    def copy_p(t, k):
        row = _bucket_row(rp_ref, start_ref, k, t)
        return pltpu.make_async_copy(hp_ref.at[pl.ds(t, 1)], xs_ref.at[pl.ds(row, 1)], sem)

    def copy_s(t, k):
        row = _bucket_row(rs_ref, start_ref, k, t)
        return pltpu.make_async_copy(hs_ref.at[pl.ds(t, 1)], xs_ref.at[pl.ds(row, 1)], sem)

    def copy_z(j):
        start = pl.multiple_of(ztile_ref[0, j] * MOE_HALF, MOE_HALF)
        return pltpu.make_async_copy(zero_buf, xs_ref.at[pl.ds(start, MOE_HALF)], zsem)

    @pl.when(step == 0)
    def _():
        zero_buf[...] = jnp.zeros_like(zero_buf)
        for j in range(ztile_ref.shape[1]):
            pl.when(ztile_ref[1, j] == 1)(lambda j=j: copy_z(j).start())
        for j in range(ztile_ref.shape[1]):
            pl.when(ztile_ref[1, j] == 1)(lambda j=j: copy_z(j).wait())
        for t in range(n_s):
            copy_s(t, 0).start()
            copy_s(t, 1).start()
        for t in range(n_s):
            copy_s(t, 0).wait()
            copy_s(t, 1).wait()

    def start_p(i, c):
        for j in range(SUBLANES):
            copy_p(i * SUBLANES + j, 0).start(priority=0)
            copy_p(i * SUBLANES + j, 1).start(priority=1)
        return c

    def wait_p(t, c):
        copy_p(t, 0).wait()
        copy_p(t, 1).wait()
        return c

    lax.fori_loop(0, tr // SUBLANES, start_p, 0)
    lax.fori_loop(0, tr, wait_p, 0)


def _bucket_call(rp, rs, row_start, ztiles, hp, hs, p_rows, *, tr=256):
    tp, dh = hp.shape
    n_s = rs.shape[1]
    smem = lambda shape, imap: pl.BlockSpec(shape, imap, memory_space=pltpu.SMEM)
    return pl.pallas_call(
        _bucket_kernel,
        grid=(tp // tr,),
        in_specs=[smem((SUBLANES, tr), lambda i: (0, i)), smem((SUBLANES, n_s), lambda i: (0, 0)),
                  smem(row_start.shape, lambda i: (0, 0)), smem(ztiles.shape, lambda i: (0, 0)),
                  pl.BlockSpec((tr, dh), lambda i: (i, 0)),
                  pl.BlockSpec((n_s, dh), lambda i: (0, 0))],
        out_specs=pl.BlockSpec(memory_space=pl.ANY),
        out_shape=jax.ShapeDtypeStruct((p_rows, dh), hp.dtype),
        scratch_shapes=[pltpu.VMEM((MOE_HALF, dh), hp.dtype), pltpu.SemaphoreType.DMA, pltpu.SemaphoreType.DMA],
        compiler_params=_params(("arbitrary",)),
        name="moe_bucket",
    )(rp, rs, row_start, ztiles, hp, hs)


(S_XTILE, S_OTILE, S_EXPERT, S_WCHUNK, S_OCHUNK, S_FIRST, S_VALID, S_FULL,
 S_HAS_NEXT, S_NEXT_EXPERT, S_NEXT_CHUNK) = range(11)


def _grouped_kernel(s_ref, x_ref, *refs, n_w, layer, tn, swiglu):
    w_hbm = refs[:n_w]
    o_ref = refs[n_w]
    w_f32 = refs[n_w + 1:2 * n_w + 1]
    w_b16 = refs[2 * n_w + 1]
    sem = refs[2 * n_w + 2]
    it = pl.program_id(0)

    def w_copy(k, expert, chunk):
        col = pl.multiple_of(chunk * tn, LANES)
        return pltpu.make_async_copy(w_hbm[k].at[layer, expert, :, pl.ds(col, tn)], w_f32[k], sem.at[k])

    @pl.when(it == 0)
    def _():
        for k in range(n_w):
            w_copy(k, s_ref[S_EXPERT, 0], s_ref[S_WCHUNK, 0]).start()

    @pl.when(s_ref[S_FIRST, it] == 1)
    def _():
        for k in range(n_w):
            w_copy(k, s_ref[S_EXPERT, it], s_ref[S_WCHUNK, it]).wait()
            w_b16[:, k * tn:(k + 1) * tn] = w_f32[k][...].astype(BF16)

        @pl.when(s_ref[S_HAS_NEXT, it] == 1)
        def _():
            for k in range(n_w):
                w_copy(k, s_ref[S_NEXT_EXPERT, it], s_ref[S_NEXT_CHUNK, it]).start()

    def compute(rows):
        acc = jnp.dot(x_ref[0:rows, :].astype(BF16), w_b16[...], preferred_element_type=F32)
        if swiglu:
            gate = acc[:, :tn]
            acc = gate * jax.nn.sigmoid(gate) * acc[:, tn:]
        o_ref[0:rows, :] = acc.astype(o_ref.dtype)

    valid = s_ref[S_VALID, it] == 1
    full = s_ref[S_FULL, it] == 1

    @pl.when(valid & full)
    def _():
        compute(MOE_TM)

    @pl.when(valid & jnp.logical_not(full))
    def _():
        compute(MOE_HALF)
        o_ref[MOE_HALF:, :] = jnp.zeros((MOE_TM - MOE_HALF, tn), o_ref.dtype)

    @pl.when(jnp.logical_not(valid))
    def _():
        o_ref[...] = jnp.zeros_like(o_ref)


def _grouped_call(sched, x, weights, layer, *, tn, out_dtype, swiglu, name):
    p, k = x.shape
    n = weights[0].shape[3]
    n_w = len(weights)
    n_items = sched.shape[1]
    grid_spec = pltpu.PrefetchScalarGridSpec(
        num_scalar_prefetch=1,
        grid=(n_items,),
        in_specs=[pl.BlockSpec((MOE_TM, k), lambda it, s: (s[S_XTILE, it], 0))]
                 + [pl.BlockSpec(memory_space=pl.ANY)] * n_w,
        out_specs=pl.BlockSpec((MOE_TM, tn), lambda it, s: (s[S_OTILE, it], s[S_OCHUNK, it])),
        scratch_shapes=[pltpu.VMEM((k, tn), F32)] * n_w + [pltpu.VMEM((k, n_w * tn), BF16),
                                                           pltpu.SemaphoreType.DMA((n_w,))],
    )
    return pl.pallas_call(
        functools.partial(_grouped_kernel, n_w=n_w, layer=layer, tn=tn, swiglu=swiglu),
        grid_spec=grid_spec,
        out_shape=jax.ShapeDtypeStruct((p, n), out_dtype),
        compiler_params=_params(("arbitrary",)),
        name=name,
    )(sched, x, *weights)


def _combine_kernel(route_ref, start_ref, res_ref, gate_ref, w_ref, y_ref, o_ref, buf, sem, *, tr):
    def copy(t, k):
        row = _bucket_row(route_ref, start_ref, k, t)
        return pltpu.make_async_copy(y_ref.at[pl.ds(row, 1)], buf.at[k, pl.ds(t, 1)], sem)

    def start(i, c):
        for j in range(SUBLANES):
            copy(i * SUBLANES + j, 0).start(priority=0)
            copy(i * SUBLANES + j, 1).start(priority=1)
        return c

    def wait(t, c):
        copy(t, 0).wait()
        copy(t, 1).wait()
        return c

    lax.fori_loop(0, tr // SUBLANES, start, 0)
    lax.fori_loop(0, tr, wait, 0)
    w = jnp.transpose(w_ref[...])
    y = w[:, 0:1] * buf[0] + w[:, 1:2] * buf[1]
    o_ref[...] = res_ref[...] + gate_ref[...] * y


def _combine_call(route, row_start, res, gate, w, y, *, per_row, tr, rows_per_mod=None):
    rows, d = res.shape
    gspec = _mod_specs(per_row, tr, d, rows_per_mod)
    smem = lambda shape, imap: pl.BlockSpec(shape, imap, memory_space=pltpu.SMEM)
    return pl.pallas_call(
        functools.partial(_combine_kernel, tr=tr),
        grid=(rows // tr,),
        in_specs=[smem((SUBLANES, tr), lambda i: (0, i)), smem(row_start.shape, lambda i: (0, 0)),
                  pl.BlockSpec((tr, d), lambda i: (i, 0)), gspec,
                  pl.BlockSpec((SUBLANES, tr), lambda i: (0, i)),
                  pl.BlockSpec(memory_space=pl.ANY)],
        out_specs=pl.BlockSpec((tr, d), lambda i: (i, 0)),
        out_shape=jax.ShapeDtypeStruct((rows, d), F32),
        scratch_shapes=[pltpu.VMEM((2, tr, d), F32), pltpu.SemaphoreType.DMA],
        compiler_params=_params(("arbitrary",)),
        name="moe_combine",
    )(route, row_start, res, gate, w, y)


def _moe_schedule(counts, max_tiles, n_chunks):
    tiles_per_e = (counts + MOE_TM - 1) // MOE_TM
    tile_end = jnp.cumsum(tiles_per_e)
    tile_start = tile_end - tiles_per_e
    n_items = n_chunks * max_tiles
    item_start = tile_start * n_chunks
    item_end = tile_end * n_chunks
    it = jnp.arange(n_items, dtype=I32)
    ex = jnp.minimum(jnp.sum((it[:, None] >= item_end[None, :]).astype(I32), axis=1), N_EXPERTS - 1)
    valid = (it < item_end[-1]).astype(I32)
    local = it - item_start[ex]
    t_e = jnp.maximum(tiles_per_e[ex], 1)
    chunk = local // t_e
    tile = tile_start[ex] + local % t_e
    idx = local % t_e
    first = (idx == 0).astype(I32) * valid
    full = (counts[ex] - idx * MOE_TM > MOE_HALF).astype(I32)
    nxt = jnp.minimum(it + t_e, n_items - 1)
    has_next = first * (it + t_e < item_end[-1]).astype(I32)
    n_valid = item_end[-1]
    last = jnp.maximum(n_valid - 1, 0)
    pick = lambda v: jnp.where(valid == 1, v, v[last])
    q = it - n_valid
    otile = jnp.where(valid == 1, tile, tile_end[-1] + q // n_chunks)
    ochunk = jnp.where(valid == 1, chunk, q % n_chunks)
    return jnp.stack([pick(tile), otile, pick(ex), pick(chunk), ochunk, first, valid, full,
                      has_next, ex[nxt], chunk[nxt]]).astype(I32)


def _moe(hp, hs, rp, rs, counts, w_gate, w_up, w_down, layer):
    n_assign = 2 * (hp.shape[0] + hs.shape[0])
    max_tiles = (n_assign + N_EXPERTS * (MOE_TM - 1)) // MOE_TM
    tiles_per_e = (counts + MOE_TM - 1) // MOE_TM
    tile_end = jnp.cumsum(tiles_per_e)
    row_start = (tile_end - tiles_per_e) * MOE_TM
    half0 = row_start // MOE_HALF
    n_half = 2 * tiles_per_e
    last_real = (counts + MOE_HALF - 1) // MOE_HALF - 1
    unused = 2 * tile_end[-1] + jnp.arange(2 * N_EXPERTS, dtype=I32)
    zhalf = jnp.concatenate([half0 + last_real, half0 + n_half - 1, unused])
    zflag = jnp.concatenate([counts % MOE_HALF != 0, last_real < n_half - 1, unused < 2 * max_tiles])
    ztiles = jnp.stack([jnp.clip(zhalf, 0, 2 * max_tiles - 1), zflag.astype(I32)]).astype(I32)
    gu_chunks, dn_chunks = 3, 2
    row_start = row_start.astype(I32).reshape(1, N_EXPERTS)
    xs = _bucket_call(rp, rs, row_start, ztiles, hp, hs, max_tiles * MOE_TM)
    he = _grouped_call(_moe_schedule(counts, max_tiles, gu_chunks), xs, [w_gate, w_up], layer,
                       tn=w_gate.shape[3] // gu_chunks, out_dtype=BF16, swiglu=True, name="moe_gate_up")
    y = _grouped_call(_moe_schedule(counts, max_tiles, dn_chunks), he, [w_down], layer,
                      tn=w_down.shape[3] // dn_chunks, out_dtype=F32, swiglu=False, name="moe_down")
    return y, row_start


def kernel(x_prompt, x_sample, cache_a_kv0, cache_a_kv1, cache_a_kv2, state_b_h, state_b_conv, state_c_conv,
           c_prompt, c_sample, w_mod, b_mod, norm_g, w_in0, qk_g, b_conv_w, b_conv_b, b_wa, b_ba, b_wi, b_bi,
           b_lambda, w_out0, w_in1, c_conv_w, w_out1, w_router, b_router, w_gate, w_up, w_down):
    batch, seq, d = x_prompt.shape
    dec_batch = x_sample.shape[0]
    depth = w_mod.shape[0]
    tp = batch * seq
    width = b_lambda.shape[1]
    pad_s = SAMPLE_ROWS - dec_batch
    caches = (cache_a_kv0, cache_a_kv1, cache_a_kv2)

    def pad_rows(v):
        return jnp.pad(v, ((0, pad_s),) + ((0, 0),) * (v.ndim - 1))

    res_p = x_prompt.reshape(tp, d)
    res_s = pad_rows(x_sample.reshape(dec_batch, d))
    c_all = jnp.concatenate([c_prompt, jnp.zeros((SUBLANES - batch, d), F32), pad_rows(c_sample)], axis=0)
    mod = _mod_call(c_all, w_mod, b_mod)
    slopes = 2.0 ** (-8.0 * np.arange(1, A_HEADS + 1) / A_HEADS)
    slopes8 = jnp.asarray(np.repeat(slopes[:, None], HEAD_DIM, axis=1), dtype=F32)
    w_router_t = w_router.T

    new_kv_p = [[], [], []]
    new_kv_s = [[], [], []]
    bh_p, bconv_p, cconv_p, bh_s, bconv_s, cconv_s = [], [], [], [], [], []

    for layer in range(depth):
        li = layer // 2
        m = mod[layer]
        chunks_p = [m[:batch, k * d:(k + 1) * d].reshape(batch, 1, d) for k in range(6)]
        chunks_s = [m[SUBLANES:, k * d:(k + 1) * d] for k in range(6)]
        sh1p, sc1p, g1p, sh2p, sc2p, g2p = chunks_p
        sh1s, sc1s, g1s, sh2s, sc2s, g2s = chunks_s

        h_p = _norm_call(res_p, norm_g[layer, 0], sc1p, sh1p, per_row=False, tr=256, rows_per_mod=seq)
        h_s = _norm_call(res_s, norm_g[layer, 0], sc1s, sh1s, per_row=True, tr=SAMPLE_ROWS)

        if layer % 2 == 0:
            z_p, z_s = _dense_call(h_p, h_s, w_in0, li)
            ya_p, kv_p = _attn_prompt_call(z_p, qk_g[li, 0], qk_g[li, 1], slopes8.reshape(A_HEADS, 1, HEAD_DIM),
                                           batch=batch, seq=seq)
            for g in range(N_DIL):
                new_kv_p[g].append(kv_p[g])
            yb_p, hl_p = _lru_prompt_call(z_p, b_conv_w[li], b_conv_b[li], b_wa[li], b_ba[li], b_wi[li],
                                          b_bi[li], b_lambda[li], batch=batch, seq=seq)
            bh_p.append(hl_p[:, SUBLANES - 1])
            xb_p = z_p[:, A_QKV:A_QKV + width].reshape(batch, seq, width)
            bconv_p.append(xb_p[:, seq - (B_CONV - 1):])

            ya_s, kvnew_s = _attn_sample_call(z_s, [c[li] for c in caches], qk_g[li, 0], qk_g[li, 1], slopes8,
                                              dec_batch=dec_batch)
            for g in range(N_DIL):
                new_kv_s[g].append(jnp.concatenate([caches[g][li][:, 1:], kvnew_s[:, g][:, None]], axis=1))
            xb_s = z_s[:, A_QKV:A_QKV + width]
            gb_s = z_s[:, A_QKV + width:]
            buf = jnp.transpose(pad_rows(state_b_conv[li]), (1, 0, 2))
            yb_s, hnew_s = _lru_sample_call(xb_s, gb_s, buf, pad_rows(state_b_h[li]), b_conv_w[li], b_conv_b[li],
                                            b_wa[li], b_ba[li], b_wi[li], b_bi[li], b_lambda[li])
            bh_s.append(hnew_s[:dec_batch])
            bconv_s.append(jnp.concatenate([state_b_conv[li][:, 1:], xb_s[:dec_batch, None]], axis=1))

            mix_p = jnp.concatenate([ya_p, yb_p], axis=1)
            mix_s = jnp.concatenate([pad_rows(ya_s.reshape(dec_batch, A_WIDTH)).astype(BF16), yb_s], axis=1)
            w_out = w_out0
        else:
            z_p, z_s = _dense_call(h_p, h_s, w_in1, li)
            mix_p, ut_p = _gconv_prompt_call(z_p, c_conv_w[li], batch=batch, seq=seq)
            cconv_p.append(ut_p[:, SUBLANES - (C_CONV - 1):])
            buf = jnp.transpose(pad_rows(state_c_conv[li]), (1, 0, 2))
            mix_s, u_s = _gconv_sample_call(z_s, buf, c_conv_w[li])
            cconv_s.append(jnp.concatenate([state_c_conv[li][:, 1:], u_s[:dec_batch, None]], axis=1))
            w_out = w_out1

        res_p, res_s = _dense_call(mix_p, mix_s, w_out, li, res=(res_p, g1p, res_s, g1s), rows_per_mod=seq)

        hp, ep, wp, cnt = _norm_route_call(res_p, norm_g[layer, 1], sc2p, sh2p, w_router_t, b_router,
                                           jnp.zeros((N_EXPERTS, LANES), F32),
                                           per_row=False, tr=256, rows_per_mod=seq)
        hs, es, ws, cnt = _norm_route_call(res_s, norm_g[layer, 1], sc2s, sh2s, w_router_t, b_router, cnt,
                                           per_row=True, tr=SAMPLE_ROWS)
        y, row_start = _moe(hp, hs, ep, es, cnt[:, 0].astype(I32), w_gate, w_up, w_down, layer)
        res_p = _combine_call(ep, row_start, res_p, g2p, wp, y, per_row=False, tr=256, rows_per_mod=seq)
        res_s = _combine_call(es, row_start, res_s, g2s, ws, y, per_row=True, tr=SAMPLE_ROWS)

    y_prompt = res_p.reshape(batch, seq, d)
    y_sample = res_s[:dec_batch].reshape(dec_batch, 1, d)
    st = lambda xs: jnp.stack(xs, axis=0)
    return (y_prompt, y_sample, st(new_kv_p[0]), st(new_kv_p[1]), st(new_kv_p[2]), st(bh_p), st(bconv_p),
            st(cconv_p), st(new_kv_s[0]), st(new_kv_s[1]), st(new_kv_s[2]), st(bh_s), st(bconv_s), st(cconv_s))
```

```python
import functools

import numpy as np
import jax
import jax.numpy as jnp
from jax import lax
from jax.experimental import pallas as pl
from jax.experimental.pallas import tpu as pltpu

F32 = jnp.float32
BF16 = jnp.bfloat16
I32 = jnp.int32
U32 = jnp.uint32

HEAD_DIM = 128
A_HEADS = 8
DILATIONS = ((128, 1), (512, 4), (2048, 16))
N_DIL = len(DILATIONS)
A_WIDTH = A_HEADS * HEAD_DIM
A_QKV = N_DIL * 3 * A_WIDTH
ATTN_SCALE = HEAD_DIM ** -0.5
QB = 128
B_BLOCKS = 8
B_CONV = 4
C_CONV = 3
LRU_C = 8.0
N_EXPERTS = 16
N_GROUPS = 4
EXP_PER_GROUP = N_EXPERTS // N_GROUPS
EPS = 1e-6
SAMPLE_ROWS = 16
SUBLANES = 8
LANES = 128
NEG_INF = float("-inf")

VMEM_LIMIT = 52 * 1024 * 1024


def _params(sem, vmem=VMEM_LIMIT):
    return pltpu.CompilerParams(dimension_semantics=sem, vmem_limit_bytes=vmem)


def _mod_kernel(c_ref, w_ref, b_ref, o_ref):
    c = c_ref[...]
    cs = (c * jax.nn.sigmoid(c)).astype(BF16)
    o_ref[...] = jnp.dot(cs, w_ref[...].astype(BF16), preferred_element_type=F32) + b_ref[...]


def _mod_call(c_all, w_mod, b_mod):
    depth, d, n = w_mod.shape
    rows = c_all.shape[0]
    tn = 512
    return pl.pallas_call(
        _mod_kernel,
        grid=(depth, n // tn),
        in_specs=[
            pl.BlockSpec((rows, d), lambda l, j: (0, 0)),
            pl.BlockSpec((None, d, tn), lambda l, j: (l, 0, j)),
            pl.BlockSpec((None, 1, tn), lambda l, j: (l, 0, j)),
        ],
        out_specs=pl.BlockSpec((None, rows, tn), lambda l, j: (l, 0, j)),
        out_shape=jax.ShapeDtypeStruct((depth, rows, n), F32),
        compiler_params=_params(("arbitrary", "arbitrary")),
        name="mod_proj",
    )(c_all, w_mod, b_mod.reshape(depth, 1, n))


def _modnorm(x, g, sc, sh):
    h = x * lax.rsqrt(jnp.mean(x * x, axis=-1, keepdims=True) + EPS) * g
    return h * (1.0 + sc) + sh


def _norm_kernel(x_ref, g_ref, sc_ref, sh_ref, o_ref):
    o_ref[...] = _modnorm(x_ref[...], g_ref[...], sc_ref[...], sh_ref[...]).astype(o_ref.dtype)


def _route(logits_t, bias_ref):
    score = [jax.nn.sigmoid(logits_t[e:e + 1, :]) for e in range(N_EXPERTS)]
    sel = [score[e] + bias_ref[e] for e in range(N_EXPERTS)]
    gsum = []
    for g in range(N_GROUPS):
        a, b, c, d = sel[4 * g:4 * g + 4]
        hi1, lo1 = jnp.maximum(a, b), jnp.minimum(a, b)
        hi2, lo2 = jnp.maximum(c, d), jnp.minimum(c, d)
        top = jnp.maximum(hi1, hi2)
        second = jnp.maximum(jnp.minimum(hi1, hi2), jnp.maximum(lo1, lo2))
        gsum.append(top + second)
    best = gsum[0]
    grp = jnp.zeros_like(best, dtype=I32)
    for g in range(1, N_GROUPS):
        better = gsum[g] > best
        best = jnp.where(better, gsum[g], best)
        grp = jnp.where(better, g, grp)
    v1 = jnp.full_like(best, NEG_INF)
    e1 = jnp.zeros_like(grp)
    for e in range(N_EXPERTS):
        cand = jnp.where(grp == e // EXP_PER_GROUP, sel[e], NEG_INF)
        better = cand > v1
        v1 = jnp.where(better, cand, v1)
        e1 = jnp.where(better, e, e1)
    v2 = jnp.full_like(best, NEG_INF)
    e2 = jnp.zeros_like(grp)
    for e in range(N_EXPERTS):
        cand = jnp.where((grp == e // EXP_PER_GROUP) & (e1 != e), sel[e], NEG_INF)
        better = cand > v2
        v2 = jnp.where(better, cand, v2)
        e2 = jnp.where(better, e, e2)
    s1 = jnp.zeros_like(best)
    s2 = jnp.zeros_like(best)
    for e in range(N_EXPERTS):
        s1 = jnp.where(e1 == e, score[e], s1)
        s2 = jnp.where(e2 == e, score[e], s2)
    tot = s1 + s2
    return e1, e2, s1 / tot, s2 / tot


def _norm_route_kernel(bias_ref, x_ref, g_ref, sc_ref, sh_ref, wr_ref, cnt0_ref,
                       o_ref, e_ref, w_ref, cnt_ref, run_ref):
    @pl.when(pl.program_id(0) == 0)
    def _():
        run_ref[...] = cnt0_ref[...]

    h = _modnorm(x_ref[...], g_ref[...], sc_ref[...], sh_ref[...])
    o_ref[...] = h
    nt = (((1,), (1,)), ((), ()))
    logits_t = lax.dot_general(wr_ref[...].astype(BF16), h.astype(BF16), nt, preferred_element_type=F32)
    e1, e2, w1, w2 = _route(logits_t, bias_ref)
    n = e1.shape[1]
    expert = lax.broadcasted_iota(I32, (N_EXPERTS, n), 0)
    oh1 = (expert == e1).astype(F32)
    oh2 = (expert == e2).astype(F32)
    before = (lax.broadcasted_iota(I32, (n, n), 0) < lax.broadcasted_iota(I32, (n, n), 1)).astype(BF16)
    pre1 = jnp.dot(oh1.astype(BF16), before, preferred_element_type=F32)
    pre2 = jnp.dot(oh2.astype(BF16), before, preferred_element_type=F32)
    tot1 = jnp.sum(oh1, axis=1, keepdims=True)
    tot2 = jnp.sum(oh2, axis=1, keepdims=True)
    base = run_ref[...][:, :1]
    r1 = jnp.sum(oh1 * (base + pre1), axis=0, keepdims=True)
    r2 = jnp.sum(oh2 * (base + tot1 + pre2), axis=0, keepdims=True)
    run_ref[...] = run_ref[...] + (tot1 + tot2)
    cnt_ref[...] = run_ref[...]
    e_ref[...] = jnp.concatenate([e1, e2, r1.astype(I32), r2.astype(I32), jnp.zeros((SUBLANES - 4, n), I32)], axis=0)
    w_ref[...] = jnp.concatenate([w1, w2, jnp.zeros((SUBLANES - 2, n), F32)], axis=0)


def _mod_specs(per_row, tr, d, rows_per_mod):
    if per_row:
        spec = pl.BlockSpec((tr, d), lambda i, *_: (i, 0))
    else:
        spec = pl.BlockSpec((None, 1, d), lambda i, *_: (i * tr // rows_per_mod, 0, 0))
    return spec


def _norm_call(x, g, sc, sh, *, per_row, tr, rows_per_mod=None):
    rows, d = x.shape
    mspec = _mod_specs(per_row, tr, d, rows_per_mod)
    return pl.pallas_call(
        _norm_kernel,
        grid=(rows // tr,),
        in_specs=[pl.BlockSpec((tr, d), lambda i: (i, 0)),
                  pl.BlockSpec((1, d), lambda i: (0, 0)), mspec, mspec],
        out_specs=pl.BlockSpec((tr, d), lambda i: (i, 0)),
        out_shape=jax.ShapeDtypeStruct((rows, d), BF16),
        compiler_params=_params(("arbitrary",)),
        name="modnorm",
    )(x, g.reshape(1, d), sc, sh)


def _norm_route_call(x, g, sc, sh, w_router_t, b_router, cnt0, *, per_row, tr, rows_per_mod=None):
    rows, d = x.shape
    mspec = _mod_specs(per_row, tr, d, rows_per_mod)
    cspec = pl.BlockSpec((N_EXPERTS, LANES), lambda i, b: (0, 0))
    grid_spec = pltpu.PrefetchScalarGridSpec(
        num_scalar_prefetch=1,
        grid=(rows // tr,),
        in_specs=[pl.BlockSpec((tr, d), lambda i, b: (i, 0)),
                  pl.BlockSpec((1, d), lambda i, b: (0, 0)), mspec, mspec,
                  pl.BlockSpec((N_EXPERTS, d), lambda i, b: (0, 0)), cspec],
        out_specs=[pl.BlockSpec((tr, d), lambda i, b: (i, 0)),
                   pl.BlockSpec((SUBLANES, tr), lambda i, b: (0, i)),
                   pl.BlockSpec((SUBLANES, tr), lambda i, b: (0, i)), cspec],
        scratch_shapes=[pltpu.VMEM((N_EXPERTS, LANES), F32)],
    )
    return pl.pallas_call(
        _norm_route_kernel,
        grid_spec=grid_spec,
        out_shape=[jax.ShapeDtypeStruct((rows, d), F32),
                   jax.ShapeDtypeStruct((SUBLANES, rows), I32),
                   jax.ShapeDtypeStruct((SUBLANES, rows), F32),
                   jax.ShapeDtypeStruct((N_EXPERTS, LANES), F32)],
        compiler_params=_params(("arbitrary",)),
        name="modnorm_route",
    )(b_router, x, g.reshape(1, d), sc, sh, w_router_t, cnt0)


def _dense_kernel(*refs, n_x, residual):
    xp_refs = refs[:n_x]
    if residual:
        (xs_ref, w_ref, rp_ref, gp_ref, rs_ref, gs_ref, op_ref, os_ref, wb_ref) = refs[n_x:]
    else:
        (xs_ref, w_ref, op_ref, os_ref, wb_ref) = refs[n_x:]

    @pl.when(pl.program_id(1) == 0)
    def _():
        wb_ref[...] = w_ref[...].astype(BF16)
        acc_s = jnp.dot(xs_ref[...], wb_ref[...], preferred_element_type=F32)
        if residual:
            acc_s = rs_ref[...] + gs_ref[...] * acc_s
        os_ref[...] = acc_s.astype(os_ref.dtype)

    acc = None
    k0 = 0
    for x_ref in xp_refs:
        kx = x_ref.shape[1]
        part = jnp.dot(x_ref[...], wb_ref[k0:k0 + kx, :], preferred_element_type=F32)
        acc = part if acc is None else acc + part
        k0 += kx
    if residual:
        acc = rp_ref[...] + gp_ref[...] * acc
    op_ref[...] = acc.astype(op_ref.dtype)


def _dense_call(xps, xs, w, li, *, res=None, rows_per_mod=None, out_dtype=F32, tm=1024, tn=512):
    tp = xps[0].shape[0]
    k, n = w.shape[1:]
    rs = xs.shape[0]
    in_specs = [pl.BlockSpec((tm, x.shape[1]), lambda j, i: (i, 0)) for x in xps]
    in_specs += [pl.BlockSpec((rs, k), lambda j, i: (0, 0)),
                 pl.BlockSpec((None, k, tn), lambda j, i: (li, 0, j))]
    args = list(xps) + [xs, w]
    if res is not None:
        in_specs += [pl.BlockSpec((tm, tn), lambda j, i: (i, j)),
                     pl.BlockSpec((None, 1, tn), lambda j, i: (i * tm // rows_per_mod, 0, j)),
                     pl.BlockSpec((rs, tn), lambda j, i: (0, j)),
                     pl.BlockSpec((rs, tn), lambda j, i: (0, j))]
        args += list(res)
    return pl.pallas_call(
        functools.partial(_dense_kernel, n_x=len(xps), residual=res is not None),
        grid=(n // tn, tp // tm),
        in_specs=in_specs,
        out_specs=[pl.BlockSpec((tm, tn), lambda j, i: (i, j)),
                   pl.BlockSpec((rs, tn), lambda j, i: (0, j))],
        out_shape=[jax.ShapeDtypeStruct((tp, n), out_dtype), jax.ShapeDtypeStruct((rs, n), out_dtype)],
        scratch_shapes=[pltpu.VMEM((k, tn), BF16)],
        compiler_params=_params(("arbitrary", "arbitrary")),
        name="dense_proj",
    )(*args)


def _rms_rows(x, g):
    return x * lax.rsqrt(jnp.mean(x * x, axis=-1, keepdims=True) + EPS) * g


def _attn_prompt_kernel(q0, k0, v0, q1, k1, v1, q2, k2, v2, gq_ref, gk_ref, slope_ref,
                        ya_ref, kv0, kv1, kv2, qn_s, m_s, l_s, acc_s, kn0, kn1, kn2, kv_sem, *, seq):
    qkv = ((q0, k0, v0, kn0), (q1, k1, v1, kn1), (q2, k2, v2, kn2))
    kv_out = (kv0, kv1, kv2)
    b_idx = pl.program_id(0)
    h_idx = pl.program_id(1)

    def kv_copy(g, c):
        keep = kv_out[g].shape[1]
        src = (qkv[g][3], qkv[g][2])[c].at[pl.ds(seq - keep, keep)]
        dst = kv_out[g].at[b_idx, pl.ds(0, keep), c, h_idx, pl.ds(0, HEAD_DIM)]
        return pltpu.make_async_copy(src, dst, kv_sem.at[2 * g + c])

    slope = slope_ref[...]
    row = lax.broadcasted_iota(I32, (QB, 2 * QB), 0)
    col = lax.broadcasted_iota(I32, (QB, 2 * QB), 1)
    rel_prev = row + QB - col
    rel_first = (row - col)[:, :QB]
    nt = (((1,), (1,)), ((), ()))

    for g, (window, dil) in enumerate(DILATIONS):
        q_ref, k_ref, v_ref, kn_ref = qkv[g]
        nk = window // dil
        n_blk = seq // dil // QB
        qn_s[...] = _rms_rows(q_ref[...], gq_ref[g])
        kn_ref[...] = _rms_rows(k_ref[...], gk_ref[g])
        kv_copy(g, 0).start()
        kv_copy(g, 1).start()
        sl = slope[:, :1] * float(dil)
        bias_prev = jnp.where((rel_prev >= 0) & (rel_prev <= nk), -sl * rel_prev.astype(F32), NEG_INF)
        bias_first = jnp.where(rel_first >= 0, -sl * rel_first.astype(F32), NEG_INF)

        def rows(start, size, dil=dil):
            return pl.ds(start, size) if dil == 1 else pl.ds(start, size, stride=dil)

        def block(qstart, kstart, first, g=g, kn_ref=kn_ref, v_ref=v_ref, rows=rows,
                  bias_prev=bias_prev, bias_first=bias_first):
            nkeys = QB if first else 2 * QB
            qb = qn_s[rows(qstart, QB), :].astype(BF16)
            kb = kn_ref[rows(kstart, nkeys), :].astype(BF16)
            vb = v_ref[rows(kstart, nkeys), :].astype(BF16)
            s = lax.dot_general(qb, kb, nt, preferred_element_type=F32) * ATTN_SCALE
            s = s + (bias_first if first else bias_prev)
            m_blk = jnp.max(s, axis=-1, keepdims=True)
            p = jnp.exp(s - m_blk)
            l_blk = jnp.sum(p, axis=-1, keepdims=True)
            o_blk = jnp.dot(p.astype(BF16), vb, preferred_element_type=F32)
            qrows = rows(qstart, QB)
            if g == 0:
                m_s[qrows, :] = jnp.broadcast_to(m_blk, (QB, HEAD_DIM))
                l_s[qrows, :] = jnp.broadcast_to(l_blk, (QB, HEAD_DIM))
                acc_s[qrows, :] = o_blk
            else:
                m_old = m_s[qrows, :]
                m_new = jnp.maximum(m_old, m_blk)
                a_old = jnp.exp(m_old - m_new)
                a_blk = jnp.exp(m_blk - m_new)
                m_s[qrows, :] = m_new
                l_s[qrows, :] = l_s[qrows, :] * a_old + l_blk * a_blk
                acc_s[qrows, :] = acc_s[qrows, :] * a_old + o_blk * a_blk

        if dil == 1:
            block(0, 0, True)

            def body(n, carry, block=block):
                qstart = pl.multiple_of(n * QB, QB)
                block(qstart, pl.multiple_of(qstart - QB, QB), False)
                return carry

            lax.fori_loop(1, n_blk, body, 0)
        else:
            def body(r, carry, block=block, n_blk=n_blk, dil=dil):
                block(r, r, True)
                for n in range(1, n_blk):
                    block(r + n * QB * dil, r + (n - 1) * QB * dil, False)
                return carry

            lax.fori_loop(0, dil, body, 0)

    ya_ref[...] = (acc_s[...] / l_s[...]).astype(ya_ref.dtype)
    for g in range(N_DIL):
        kv_copy(g, 0).wait()
        kv_copy(g, 1).wait()


def _attn_prompt_call(z, gq, gk, slopes, *, batch, seq):
    t = batch * seq
    kv_shapes = [jax.ShapeDtypeStruct((batch, min(window, seq), 2, A_HEADS, HEAD_DIM), F32)
                 for window, _ in DILATIONS]

    def col(g, c):
        return lambda b, h: (b, g * 3 * A_HEADS + c * A_HEADS + h)

    in_specs = []
    for g in range(N_DIL):
        for c in range(3):
            in_specs.append(pl.BlockSpec((seq, HEAD_DIM), col(g, c)))
    gspec = pl.BlockSpec((N_DIL, None, 1, HEAD_DIM), lambda b, h: (0, h, 0, 0))
    in_specs += [gspec, gspec, pl.BlockSpec((None, 1, HEAD_DIM), lambda b, h: (h, 0, 0))]
    outs = pl.pallas_call(
        functools.partial(_attn_prompt_kernel, seq=seq),
        grid=(batch, A_HEADS),
        in_specs=in_specs,
        out_specs=[pl.BlockSpec((seq, HEAD_DIM), lambda b, h: (b, h))] + [pl.BlockSpec(memory_space=pl.ANY)] * 3,
        out_shape=[jax.ShapeDtypeStruct((t, A_WIDTH), BF16)] + kv_shapes,
        scratch_shapes=[pltpu.VMEM((seq, HEAD_DIM), F32)] * 7 + [pltpu.SemaphoreType.DMA((2 * N_DIL,))],
        compiler_params=_params(("arbitrary", "arbitrary")),
        name="attn_prompt",
    )(*([z] * 9), gq.reshape(N_DIL, A_HEADS, 1, HEAD_DIM), gk.reshape(N_DIL, A_HEADS, 1, HEAD_DIM), slopes)
    return outs[0], outs[1:]


def _round_bf16(x):
    return x.astype(BF16).astype(F32)


def _attn_sample_kernel(z_ref, c0_ref, c1_ref, c2_ref, gq_ref, gk_ref, slope_ref, ya_ref, new_ref):
    caches = (c0_ref, c1_ref, c2_ref)
    slope = slope_ref[...][:, :1]
    nk = DILATIONS[0][0] // DILATIONS[0][1]
    steps = (nk - lax.broadcasted_iota(I32, (nk, 1, 1), 0)).astype(F32)
    m_run = l_run = acc = None
    for g, (window, dil) in enumerate(DILATIONS):
        base = g * 3 * A_HEADS
        q = _rms_rows(z_ref[base:base + A_HEADS, :], gq_ref[g])
        k = _rms_rows(z_ref[base + A_HEADS:base + 2 * A_HEADS, :], gk_ref[g])
        v = z_ref[base + 2 * A_HEADS:base + 3 * A_HEADS, :]
        new_ref[g, 0] = k
        new_ref[g, 1] = v
        qr = _round_bf16(q)
        ck = _round_bf16(caches[g][:, 0])
        cv = _round_bf16(caches[g][:, 1])
        s_c = jnp.sum(ck * qr[None], axis=-1, keepdims=True) * ATTN_SCALE
        s_c = s_c - (slope * float(dil))[None] * steps
        s_n = jnp.sum(_round_bf16(k) * qr, axis=-1, keepdims=True) * ATTN_SCALE
        m_g = jnp.maximum(jnp.max(s_c, axis=0), s_n)
        p_c = jnp.exp(s_c - m_g[None])
        p_n = jnp.exp(s_n - m_g)
        l_g = jnp.sum(p_c, axis=0) + p_n
        o_g = jnp.sum(_round_bf16(p_c) * cv, axis=0) + _round_bf16(p_n) * _round_bf16(v)
        if g == 0:
            m_run, l_run, acc = m_g, l_g, o_g
        else:
            m_new = jnp.maximum(m_run, m_g)
            a_old = jnp.exp(m_run - m_new)
            a_g = jnp.exp(m_g - m_new)
            l_run = l_run * a_old + l_g * a_g
            acc = acc * a_old + o_g * a_g
            m_run = m_new
    ya_ref[...] = acc / l_run


def _attn_sample_call(z_s, caches, gq, gk, slopes8, *, dec_batch):
    n_colblk = z_s.shape[1] // HEAD_DIM
    z3 = z_s.reshape(z_s.shape[0], n_colblk, HEAD_DIM)
    in_specs = [pl.BlockSpec((None, n_colblk, HEAD_DIM), lambda b: (b, 0, 0))]
    args = [z3]
    for g, (window, dil) in enumerate(DILATIONS):
        nk = window // dil
        c = caches[g].reshape(dec_batch, nk, dil, 2, A_HEADS, HEAD_DIM)
        in_specs.append(pl.BlockSpec((None, nk, None, 2, A_HEADS, HEAD_DIM), lambda b: (b, 0, 0, 0, 0, 0)))
        args.append(c)
    full = lambda shape: pl.BlockSpec(shape, lambda b: (0,) * len(shape))
    in_specs += [full((N_DIL, A_HEADS, HEAD_DIM)), full((N_DIL, A_HEADS, HEAD_DIM)), full((A_HEADS, HEAD_DIM))]
    return pl.pallas_call(
        _attn_sample_kernel,
        grid=(dec_batch,),
        in_specs=in_specs,
        out_specs=[pl.BlockSpec((None, A_HEADS, HEAD_DIM), lambda b: (b, 0, 0)),
                   pl.BlockSpec((None, N_DIL, 2, A_HEADS, HEAD_DIM), lambda b: (b, 0, 0, 0, 0))],
        out_shape=[jax.ShapeDtypeStruct((dec_batch, A_HEADS, HEAD_DIM), F32),
                   jax.ShapeDtypeStruct((dec_batch, N_DIL, 2, A_HEADS, HEAD_DIM), F32)],
        compiler_params=_params(("arbitrary",)),
        name="attn_sample",
    )(*args, gq, gk, slopes8)


def _cache_roll_kernel(cur_ref, nxt_ref, new_ref, o_ref):
    last = pl.program_id(1) == pl.num_programs(1) - 1
    rows = cur_ref.shape[0]
    o_ref[0:rows - 1] = cur_ref[1:rows]
    o_ref[rows - 1:rows] = jnp.where(last, new_ref[...], nxt_ref[...])


def _cache_roll_call(cache, new, *, rows=256):
    b, w = cache.shape[:2]
    rows = min(rows, w)
    tail = cache.shape[2:]
    n_chunks = w // rows
    zeros = (0,) * len(tail)
    return pl.pallas_call(
        _cache_roll_kernel,
        grid=(b, n_chunks),
        in_specs=[pl.BlockSpec((None, rows) + tail, lambda i, j: (i, j) + zeros),
                  pl.BlockSpec((None, 1) + tail, lambda i, j: (i, jnp.minimum((j + 1) * rows, w - 1)) + zeros),
                  pl.BlockSpec((None, 1) + tail, lambda i, j: (i, 0) + zeros)],
        out_specs=pl.BlockSpec((None, rows) + tail, lambda i, j: (i, j) + zeros),
        out_shape=jax.ShapeDtypeStruct(cache.shape, cache.dtype),
        compiler_params=_params(("arbitrary", "arbitrary")),
        name="cache_roll",
    )(cache, cache, new)


def _softplus(x):
    return jnp.maximum(x, 0.0) + jnp.log1p(jnp.exp(-jnp.abs(x)))


def _lru_coeffs(xc, wa, ba, wi, bi, lam):
    xcb = xc.astype(BF16)
    r = jax.nn.sigmoid(jnp.dot(xcb, wa.astype(BF16), preferred_element_type=F32) + ba)
    i = jax.nn.sigmoid(jnp.dot(xcb, wi.astype(BF16), preferred_element_type=F32) + bi)
    log_a = -LRU_C * r * _softplus(-lam)
    a = jnp.exp(log_a)
    t = jnp.tanh(log_a)
    one_minus_a2 = -2.0 * t / (1.0 - t)
    return a, jnp.sqrt(one_minus_a2) * (i * xc)


def _lru_prompt_kernel(x_ref, gate_ref, cw_ref, cb_ref, wa_ref, ba_ref, wi_ref, bi_ref, lam_ref,
                       y_ref, hl_ref, pad_a, pad_b, *, seq):
    blk = x_ref.shape[1]
    off = SUBLANES
    x = x_ref[...]
    pad_a[0:off, :] = jnp.zeros((off, blk), F32)
    pad_a[off:off + seq, :] = x
    xc = pad_a[off - 3:off - 3 + seq, :] * cw_ref[0:1, :]
    xc = xc + pad_a[off - 2:off - 2 + seq, :] * cw_ref[1:2, :]
    xc = xc + pad_a[off - 1:off - 1 + seq, :] * cw_ref[2:3, :]
    xc = xc + x * cw_ref[3:4, :] + cb_ref[...]
    a, b = _lru_coeffs(xc, wa_ref[...], ba_ref[...], wi_ref[...], bi_ref[...], lam_ref[...])

    tpos = lax.broadcasted_iota(I32, (seq, blk), 0) & (SUBLANES - 1)
    pad_b[0:off, :] = jnp.zeros((off, blk), F32)
    for o in (1, 2, 4):
        pad_a[off:off + seq, :] = a
        pad_b[off:off + seq, :] = b
        keep = tpos >= o
        a_prev = jnp.where(keep, pad_a[off - o:off - o + seq, :], 1.0)
        b_prev = jnp.where(keep, pad_b[off - o:off - o + seq, :], 0.0)
        b = a * b_prev + b
        a = a * a_prev
    pad_a[off:off + seq, :] = a
    pad_b[off:off + seq, :] = b

    def tile(i, h_prev):
        r0 = pl.multiple_of(off + i * SUBLANES, SUBLANES)
        h = pad_b[pl.ds(r0, SUBLANES), :] + pad_a[pl.ds(r0, SUBLANES), :] * h_prev
        pad_b[pl.ds(r0, SUBLANES), :] = h
        return jnp.broadcast_to(h[SUBLANES - 1:SUBLANES, :], (SUBLANES, blk))

    lax.fori_loop(0, seq // SUBLANES, tile, jnp.zeros((SUBLANES, blk), F32))
    h = pad_b[off:off + seq, :]
    hl_ref[...] = h[seq - SUBLANES:, :]
    y_ref[...] = (h * jax.nn.gelu(gate_ref[...])).astype(y_ref.dtype)


def _lru_prompt_call(z, cw, cb, wa, ba, wi, bi, lam, *, batch, seq):
    t = batch * seq
    nb, blk, _ = wa.shape
    width = nb * blk
    x0 = A_QKV // blk
    g0 = (A_QKV + width) // blk
    vec = lambda v: v.reshape(1, width)
    vspec = pl.BlockSpec((1, blk), lambda b, n: (0, n))
    return pl.pallas_call(
        functools.partial(_lru_prompt_kernel, seq=seq),
        grid=(batch, nb),
        in_specs=[pl.BlockSpec((seq, blk), lambda b, n: (b, x0 + n)),
                  pl.BlockSpec((seq, blk), lambda b, n: (b, g0 + n)),
                  pl.BlockSpec((B_CONV, blk), lambda b, n: (0, n)), vspec,
                  pl.BlockSpec((None, blk, blk), lambda b, n: (n, 0, 0)), vspec,
                  pl.BlockSpec((None, blk, blk), lambda b, n: (n, 0, 0)), vspec, vspec],
        out_specs=[pl.BlockSpec((seq, blk), lambda b, n: (b, n)),
                   pl.BlockSpec((None, SUBLANES, blk), lambda b, n: (b, 0, n))],
        out_shape=[jax.ShapeDtypeStruct((t, width), BF16),
                   jax.ShapeDtypeStruct((batch, SUBLANES, width), F32)],
        scratch_shapes=[pltpu.VMEM((seq + SUBLANES, blk), F32)] * 2,
        compiler_params=_params(("arbitrary", "arbitrary")),
        name="rglru_prompt",
    )(z, z, cw, vec(cb), wa, vec(ba), wi, vec(bi), vec(lam))


def _lru_sample_kernel(x_ref, gate_ref, buf_ref, h0_ref, cw_ref, cb_ref, wa_ref, ba_ref, wi_ref, bi_ref,
                       lam_ref, y_ref, h_ref):
    nb, blk, _ = wa_ref.shape
    for n in range(nb):
        cs = slice(n * blk, (n + 1) * blk)
        x = x_ref[:, cs]
        xc = buf_ref[0, :, cs] * cw_ref[0:1, cs]
        xc = xc + buf_ref[1, :, cs] * cw_ref[1:2, cs]
        xc = xc + buf_ref[2, :, cs] * cw_ref[2:3, cs]
        xc = xc + x * cw_ref[3:4, cs] + cb_ref[:, cs]
        a, b = _lru_coeffs(xc, wa_ref[n], ba_ref[:, cs], wi_ref[n], bi_ref[:, cs], lam_ref[:, cs])
        h = b + a * h0_ref[:, cs]
        h_ref[:, cs] = h
        y_ref[:, cs] = (h * jax.nn.gelu(gate_ref[:, cs])).astype(y_ref.dtype)


def _lru_sample_call(x, gate, buf, h0, cw, cb, wa, ba, wi, bi, lam):
    rows, width = x.shape
    vec = lambda v: v.reshape(1, width)
    return pl.pallas_call(
        _lru_sample_kernel,
        out_shape=[jax.ShapeDtypeStruct((rows, width), BF16), jax.ShapeDtypeStruct((rows, width), F32)],
        compiler_params=pltpu.CompilerParams(vmem_limit_bytes=VMEM_LIMIT),
        name="rglru_sample",
    )(x, gate, buf, h0, cw, vec(cb), wa, vec(ba), wi, vec(bi), vec(lam))


def _gconv_prompt_kernel(b_ref, c_ref, v_ref, cw_ref, m_ref, ut_ref, pad, *, seq):
    blk = b_ref.shape[1]
    off = SUBLANES
    u = c_ref[...] * v_ref[...]
    pad[0:off, :] = jnp.zeros((off, blk), F32)
    pad[off:off + seq, :] = u
    uc = pad[off - 2:off - 2 + seq, :] * cw_ref[0:1, :]
    uc = uc + pad[off - 1:off - 1 + seq, :] * cw_ref[1:2, :]
    uc = uc + u * cw_ref[2:3, :]
    m_ref[...] = (b_ref[...] * uc).astype(m_ref.dtype)
    ut_ref[...] = u[seq - SUBLANES:, :]


def _gconv_prompt_call(z, cw, *, batch, seq, blk=256):
    t = batch * seq
    d = z.shape[1] // 3
    nblk = d // blk
    return pl.pallas_call(
        functools.partial(_gconv_prompt_kernel, seq=seq),
        grid=(batch, nblk),
        in_specs=[pl.BlockSpec((seq, blk), lambda b, n: (b, n)),
                  pl.BlockSpec((seq, blk), lambda b, n: (b, nblk + n)),
                  pl.BlockSpec((seq, blk), lambda b, n: (b, 2 * nblk + n)),
                  pl.BlockSpec((C_CONV, blk), lambda b, n: (0, n))],
        out_specs=[pl.BlockSpec((seq, blk), lambda b, n: (b, n)),
                   pl.BlockSpec((None, SUBLANES, blk), lambda b, n: (b, 0, n))],
        out_shape=[jax.ShapeDtypeStruct((t, d), BF16), jax.ShapeDtypeStruct((batch, SUBLANES, d), F32)],
        scratch_shapes=[pltpu.VMEM((seq + SUBLANES, blk), F32)],
        compiler_params=_params(("arbitrary", "arbitrary")),
        name="gconv_prompt",
    )(z, z, z, cw)


def _gconv_sample_kernel(z_ref, buf_ref, cw_ref, m_ref, u_ref):
    d = m_ref.shape[1]
    u = z_ref[:, d:2 * d] * z_ref[:, 2 * d:3 * d]
    uc = buf_ref[0] * cw_ref[0:1, :] + buf_ref[1] * cw_ref[1:2, :]
    uc = uc + u * cw_ref[2:3, :]
    m_ref[...] = (z_ref[:, 0:d] * uc).astype(m_ref.dtype)
    u_ref[...] = u


def _gconv_sample_call(z_s, buf, cw):
    rows = z_s.shape[0]
    d = z_s.shape[1] // 3
    return pl.pallas_call(
        _gconv_sample_kernel,
        out_shape=[jax.ShapeDtypeStruct((rows, d), BF16), jax.ShapeDtypeStruct((rows, d), F32)],
        name="gconv_sample",
    )(z_s, buf, cw)


MOE_TM = 512
MOE_HALF = MOE_TM // 2


def _bucket_row(route_ref, start_ref, k, t):
    return start_ref[0, route_ref[k, t]] + route_ref[2 + k, t]


def _bucket_kernel(rp_ref, rs_ref, start_ref, ztile_ref, hp_ref, hs_ref, xs_ref, zero_buf, sem, zsem):
    step = pl.program_id(0)
    tr = hp_ref.shape[0]
    n_s = rs_ref.shape[1]

    def copy_p(t, k):
        row = _bucket_row(rp_ref, start_ref, k, t)
        return pltpu.make_async_copy(hp_ref.at[pl.ds(t, 1)], xs_ref.at[pl.ds(row, 1)], sem)

    def copy_s(t, k):
        row = _bucket_row(rs_ref, start_ref, k, t)
        return pltpu.make_async_copy(hs_ref.at[pl.ds(t, 1)], xs_ref.at[pl.ds(row, 1)], sem)

    def copy_z(j):
        start = pl.multiple_of(ztile_ref[0, j] * MOE_HALF, MOE_HALF)
        return pltpu.make_async_copy(zero_buf, xs_ref.at[pl.ds(start, MOE_HALF)], zsem)

    @pl.when(step == 0)
    def _():
        zero_buf[...] = jnp.zeros_like(zero_buf)
        for j in range(ztile_ref.shape[1]):
            pl.when(ztile_ref[1, j] == 1)(lambda j=j: copy_z(j).start())
        for j in range(ztile_ref.shape[1]):
            pl.when(ztile_ref[1, j] == 1)(lambda j=j: copy_z(j).wait())
        for t in range(n_s):
            copy_s(t, 0).start()
            copy_s(t, 1).start()
        for t in range(n_s):
            copy_s(t, 0).wait()
            copy_s(t, 1).wait()

    def start_p(i, c):
        for j in range(SUBLANES):
            copy_p(i * SUBLANES + j, 0).start(priority=0)
            copy_p(i * SUBLANES + j, 1).start(priority=1)
        return c

    def wait_p(t, c):
        copy_p(t, 0).wait()
        copy_p(t, 1).wait()
        return c

    lax.fori_loop(0, tr // SUBLANES, start_p, 0)
    lax.fori_loop(0, tr, wait_p, 0)


def _bucket_call(rp, rs, row_start, ztiles, hp, hs, p_rows, *, tr=256):
    tp, dh = hp.shape
    n_s = rs.shape[1]
    smem = lambda shape, imap: pl.BlockSpec(shape, imap, memory_space=pltpu.SMEM)
    return pl.pallas_call(
        _bucket_kernel,
        grid=(tp // tr,),
        in_specs=[smem((SUBLANES, tr), lambda i: (0, i)), smem((SUBLANES, n_s), lambda i: (0, 0)),
                  smem(row_start.shape, lambda i: (0, 0)), smem(ztiles.shape, lambda i: (0, 0)),
                  pl.BlockSpec((tr, dh), lambda i: (i, 0)),
                  pl.BlockSpec((n_s, dh), lambda i: (0, 0))],
        out_specs=pl.BlockSpec(memory_space=pl.ANY),
        out_shape=jax.ShapeDtypeStruct((p_rows, dh), hp.dtype),
        scratch_shapes=[pltpu.VMEM((MOE_HALF, dh), hp.dtype), pltpu.SemaphoreType.DMA, pltpu.SemaphoreType.DMA],
        compiler_params=_params(("arbitrary",)),
        name="moe_bucket",
    )(rp, rs, row_start, ztiles, hp, hs)


(S_XTILE, S_OTILE, S_EXPERT, S_WCHUNK, S_OCHUNK, S_FIRST, S_VALID, S_FULL,
 S_HAS_NEXT, S_NEXT_EXPERT, S_NEXT_CHUNK) = range(11)


def _grouped_kernel(s_ref, x_ref, *refs, n_w, layer, tn, swiglu):
    w_hbm = refs[:n_w]
    o_ref = refs[n_w]
    w_f32 = refs[n_w + 1:2 * n_w + 1]
    w_b16 = refs[2 * n_w + 1]
    sem = refs[2 * n_w + 2]
    it = pl.program_id(0)

    def w_copy(k, expert, chunk):
        col = pl.multiple_of(chunk * tn, LANES)
        return pltpu.make_async_copy(w_hbm[k].at[layer, expert, :, pl.ds(col, tn)], w_f32[k], sem.at[k])

    @pl.when(it == 0)
    def _():
        for k in range(n_w):
            w_copy(k, s_ref[S_EXPERT, 0], s_ref[S_WCHUNK, 0]).start()

    @pl.when(s_ref[S_FIRST, it] == 1)
    def _():
        for k in range(n_w):
            w_copy(k, s_ref[S_EXPERT, it], s_ref[S_WCHUNK, it]).wait()
            w_b16[:, k * tn:(k + 1) * tn] = w_f32[k][...].astype(BF16)

        @pl.when(s_ref[S_HAS_NEXT, it] == 1)
        def _():
            for k in range(n_w):
                w_copy(k, s_ref[S_NEXT_EXPERT, it], s_ref[S_NEXT_CHUNK, it]).start()

    def compute(rows):
        acc = jnp.dot(x_ref[0:rows, :].astype(BF16), w_b16[...], preferred_element_type=F32)
        if swiglu:
            gate = acc[:, :tn]
            acc = gate * jax.nn.sigmoid(gate) * acc[:, tn:]
        o_ref[0:rows, :] = acc.astype(o_ref.dtype)

    valid = s_ref[S_VALID, it] == 1
    full = s_ref[S_FULL, it] == 1

    @pl.when(valid & full)
    def _():
        compute(MOE_TM)

    @pl.when(valid & jnp.logical_not(full))
    def _():
        compute(MOE_HALF)
        o_ref[MOE_HALF:, :] = jnp.zeros((MOE_TM - MOE_HALF, tn), o_ref.dtype)

    @pl.when(jnp.logical_not(valid))
    def _():
        o_ref[...] = jnp.zeros_like(o_ref)


def _grouped_call(sched, x, weights, layer, *, tn, out_dtype, swiglu, name):
    p, k = x.shape
    n = weights[0].shape[3]
    n_w = len(weights)
    n_items = sched.shape[1]
    grid_spec = pltpu.PrefetchScalarGridSpec(
        num_scalar_prefetch=1,
        grid=(n_items,),
        in_specs=[pl.BlockSpec((MOE_TM, k), lambda it, s: (s[S_XTILE, it], 0))]
                 + [pl.BlockSpec(memory_space=pl.ANY)] * n_w,
        out_specs=pl.BlockSpec((MOE_TM, tn), lambda it, s: (s[S_OTILE, it], s[S_OCHUNK, it])),
        scratch_shapes=[pltpu.VMEM((k, tn), F32)] * n_w + [pltpu.VMEM((k, n_w * tn), BF16),
                                                           pltpu.SemaphoreType.DMA((n_w,))],
    )
    return pl.pallas_call(
        functools.partial(_grouped_kernel, n_w=n_w, layer=layer, tn=tn, swiglu=swiglu),
        grid_spec=grid_spec,
        out_shape=jax.ShapeDtypeStruct((p, n), out_dtype),
        compiler_params=_params(("arbitrary",)),
        name=name,
    )(sched, x, *weights)


def _combine_kernel(route_ref, start_ref, res_ref, gate_ref, w_ref, y_ref, o_ref, buf, sem, *, tr):
    def copy(t, k):
        row = _bucket_row(route_ref, start_ref, k, t)
        return pltpu.make_async_copy(y_ref.at[pl.ds(row, 1)], buf.at[k, pl.ds(t, 1)], sem)

    def start(i, c):
        for j in range(SUBLANES):
            copy(i * SUBLANES + j, 0).start(priority=0)
            copy(i * SUBLANES + j, 1).start(priority=1)
        return c

    def wait(t, c):
        copy(t, 0).wait()
        copy(t, 1).wait()
        return c

    lax.fori_loop(0, tr // SUBLANES, start, 0)
    lax.fori_loop(0, tr, wait, 0)
    w = jnp.transpose(w_ref[...])
    y = w[:, 0:1] * buf[0] + w[:, 1:2] * buf[1]
    o_ref[...] = res_ref[...] + gate_ref[...] * y


def _combine_call(route, row_start, res, gate, w, y, *, per_row, tr, rows_per_mod=None):
    rows, d = res.shape
    gspec = _mod_specs(per_row, tr, d, rows_per_mod)
    smem = lambda shape, imap: pl.BlockSpec(shape, imap, memory_space=pltpu.SMEM)
    return pl.pallas_call(
        functools.partial(_combine_kernel, tr=tr),
        grid=(rows // tr,),
        in_specs=[smem((SUBLANES, tr), lambda i: (0, i)), smem(row_start.shape, lambda i: (0, 0)),
                  pl.BlockSpec((tr, d), lambda i: (i, 0)), gspec,
                  pl.BlockSpec((SUBLANES, tr), lambda i: (0, i)),
                  pl.BlockSpec(memory_space=pl.ANY)],
        out_specs=pl.BlockSpec((tr, d), lambda i: (i, 0)),
        out_shape=jax.ShapeDtypeStruct((rows, d), F32),
        scratch_shapes=[pltpu.VMEM((2, tr, d), F32), pltpu.SemaphoreType.DMA],
        compiler_params=_params(("arbitrary",)),
        name="moe_combine",
    )(route, row_start, res, gate, w, y)


def _moe_schedule(counts, max_tiles, n_chunks):
    e_ids = jnp.arange(N_EXPERTS, dtype=I32)
    tiles_per_e = (counts + MOE_TM - 1) // MOE_TM
    tile_end = jnp.cumsum(tiles_per_e)
    tile_start = tile_end - tiles_per_e
    used = tile_end[-1]
    n_valid = used * n_chunks
    nonempty = tiles_per_e > 0
    later = (e_ids[None, :] > e_ids[:, None]) & nonempty[None, :]
    next_e = jnp.min(jnp.where(later, e_ids[None, :], N_EXPERTS - 1), axis=1)
    last_e = jnp.max(jnp.where(nonempty, e_ids, 0))
    it = jnp.arange(n_chunks * max_tiles, dtype=I32)
    ex = jnp.minimum(jnp.sum((it[:, None] >= (tile_end * n_chunks)[None, :]).astype(I32), axis=1), N_EXPERTS - 1)
    onehot = (ex[:, None] == e_ids[None, :]).astype(I32)
    look = lambda v: jnp.sum(onehot * v.astype(I32)[None, :], axis=1)
    t_e = jnp.maximum(look(tiles_per_e), 1)
    t0 = look(tile_start)
    local = it - t0 * n_chunks
    chunk = local // t_e
    idx = local % t_e
    valid = it < n_valid
    first = (idx == 0) & valid
    full = look(counts) - idx * MOE_TM > MOE_HALF
    has_next = first & (it + t_e < n_valid)
    last_chunk = chunk == n_chunks - 1
    next_expert = jnp.where(last_chunk, look(next_e), ex)
    next_chunk = jnp.where(last_chunk, 0, chunk + 1)
    q = it - n_valid
    xtile = jnp.where(valid, t0 + idx, used - 1)
    otile = jnp.where(valid, t0 + idx, used + q // n_chunks)
    ochunk = jnp.where(valid, chunk, q % n_chunks)
    return jnp.stack([xtile, otile, jnp.where(valid, ex, last_e), jnp.where(valid, chunk, n_chunks - 1), ochunk,
                      first, valid, full, has_next, next_expert, next_chunk]).astype(I32)


def _moe(hp, hs, rp, rs, counts, w_gate, w_up, w_down, layer):
    n_assign = 2 * (hp.shape[0] + hs.shape[0])
    max_tiles = (n_assign + N_EXPERTS * (MOE_TM - 1)) // MOE_TM
    tiles_per_e = (counts + MOE_TM - 1) // MOE_TM
    tile_end = jnp.cumsum(tiles_per_e)
    row_start = (tile_end - tiles_per_e) * MOE_TM
    half0 = row_start // MOE_HALF
    n_half = 2 * tiles_per_e
    last_real = (counts + MOE_HALF - 1) // MOE_HALF - 1
    unused = 2 * tile_end[-1] + jnp.arange(2 * N_EXPERTS, dtype=I32)
    zhalf = jnp.concatenate([half0 + last_real, half0 + n_half - 1, unused])
    zflag = jnp.concatenate([counts % MOE_HALF != 0, last_real < n_half - 1, unused < 2 * max_tiles])
    ztiles = jnp.stack([jnp.clip(zhalf, 0, 2 * max_tiles - 1), zflag.astype(I32)]).astype(I32)
    gu_chunks, dn_chunks = 3, 2
    row_start = row_start.astype(I32).reshape(1, N_EXPERTS)
    xs = _bucket_call(rp, rs, row_start, ztiles, hp, hs, max_tiles * MOE_TM)
    he = _grouped_call(_moe_schedule(counts, max_tiles, gu_chunks), xs, [w_gate, w_up], layer,
                       tn=w_gate.shape[3] // gu_chunks, out_dtype=BF16, swiglu=True, name="moe_gate_up")
    y = _grouped_call(_moe_schedule(counts, max_tiles, dn_chunks), he, [w_down], layer,
                      tn=w_down.shape[3] // dn_chunks, out_dtype=F32, swiglu=False, name="moe_down")
    return y, row_start


def kernel(x_prompt, x_sample, cache_a_kv0, cache_a_kv1, cache_a_kv2, state_b_h, state_b_conv, state_c_conv,
           c_prompt, c_sample, w_mod, b_mod, norm_g, w_in0, qk_g, b_conv_w, b_conv_b, b_wa, b_ba, b_wi, b_bi,
           b_lambda, w_out0, w_in1, c_conv_w, w_out1, w_router, b_router, w_gate, w_up, w_down):
    batch, seq, d = x_prompt.shape
    dec_batch = x_sample.shape[0]
    depth = w_mod.shape[0]
    tp = batch * seq
    width = b_lambda.shape[1]
    pad_s = SAMPLE_ROWS - dec_batch
    caches = (cache_a_kv0, cache_a_kv1, cache_a_kv2)

    def pad_rows(v):
        return jnp.pad(v, ((0, pad_s),) + ((0, 0),) * (v.ndim - 1))

    res_p = x_prompt.reshape(tp, d)
    res_s = pad_rows(x_sample.reshape(dec_batch, d))
    c_all = jnp.concatenate([c_prompt, jnp.zeros((SUBLANES - batch, d), F32), pad_rows(c_sample)], axis=0)
    mod = _mod_call(c_all, w_mod, b_mod)
    slopes = 2.0 ** (-8.0 * np.arange(1, A_HEADS + 1) / A_HEADS)
    slopes8 = jnp.asarray(np.repeat(slopes[:, None], HEAD_DIM, axis=1), dtype=F32)
    w_router_t = w_router.T

    new_kv_p = [[], [], []]
    new_kv_s = [[], [], []]
    bh_p, bconv_p, cconv_p, bh_s, bconv_s, cconv_s = [], [], [], [], [], []

    for layer in range(depth):
        li = layer // 2
        m = mod[layer]
        chunks_p = [m[:batch, k * d:(k + 1) * d].reshape(batch, 1, d) for k in range(6)]
        chunks_s = [m[SUBLANES:, k * d:(k + 1) * d] for k in range(6)]
        sh1p, sc1p, g1p, sh2p, sc2p, g2p = chunks_p
        sh1s, sc1s, g1s, sh2s, sc2s, g2s = chunks_s

        h_p = _norm_call(res_p, norm_g[layer, 0], sc1p, sh1p, per_row=False, tr=256, rows_per_mod=seq)
        h_s = _norm_call(res_s, norm_g[layer, 0], sc1s, sh1s, per_row=True, tr=SAMPLE_ROWS)

        if layer % 2 == 0:
            z_p, z_s = _dense_call([h_p], h_s, w_in0, li)
            ya_p, kv_p = _attn_prompt_call(z_p, qk_g[li, 0], qk_g[li, 1], slopes8.reshape(A_HEADS, 1, HEAD_DIM),
                                           batch=batch, seq=seq)
            for g in range(N_DIL):
                new_kv_p[g].append(kv_p[g])
            yb_p, hl_p = _lru_prompt_call(z_p, b_conv_w[li], b_conv_b[li], b_wa[li], b_ba[li], b_wi[li],
                                          b_bi[li], b_lambda[li], batch=batch, seq=seq)
            bh_p.append(hl_p[:, SUBLANES - 1])
            bconv_p.append(z_p.reshape(batch, seq, -1)[:, seq - (B_CONV - 1):, A_QKV:A_QKV + width])

            ya_s, kvnew_s = _attn_sample_call(z_s, [c[li] for c in caches], qk_g[li, 0], qk_g[li, 1], slopes8,
                                              dec_batch=dec_batch)
            for g in range(N_DIL):
                new_kv_s[g].append(_cache_roll_call(caches[g][li], kvnew_s[:, g][:, None]))
            xb_s = z_s[:, A_QKV:A_QKV + width]
            gb_s = z_s[:, A_QKV + width:]
            buf = jnp.transpose(pad_rows(state_b_conv[li]), (1, 0, 2))
            yb_s, hnew_s = _lru_sample_call(xb_s, gb_s, buf, pad_rows(state_b_h[li]), b_conv_w[li], b_conv_b[li],
                                            b_wa[li], b_ba[li], b_wi[li], b_bi[li], b_lambda[li])
            bh_s.append(hnew_s[:dec_batch])
            bconv_s.append(jnp.concatenate([state_b_conv[li][:, 1:], xb_s[:dec_batch, None]], axis=1))

            mix_p = [ya_p, yb_p]
            mix_s = jnp.concatenate([pad_rows(ya_s.reshape(dec_batch, A_WIDTH)).astype(BF16), yb_s], axis=1)
            w_out = w_out0
        else:
            z_p, z_s = _dense_call([h_p], h_s, w_in1, li)
            m_p, ut_p = _gconv_prompt_call(z_p, c_conv_w[li], batch=batch, seq=seq)
            mix_p = [m_p]
            cconv_p.append(ut_p[:, SUBLANES - (C_CONV - 1):])
            buf = jnp.transpose(pad_rows(state_c_conv[li]), (1, 0, 2))
            mix_s, u_s = _gconv_sample_call(z_s, buf, c_conv_w[li])
            cconv_s.append(jnp.concatenate([state_c_conv[li][:, 1:], u_s[:dec_batch, None]], axis=1))
            w_out = w_out1

        res_p, res_s = _dense_call(mix_p, mix_s, w_out, li, res=(res_p, g1p, res_s, g1s), rows_per_mod=seq)

        hp, ep, wp, cnt = _norm_route_call(res_p, norm_g[layer, 1], sc2p, sh2p, w_router_t, b_router,
                                           jnp.zeros((N_EXPERTS, LANES), F32),
                                           per_row=False, tr=256, rows_per_mod=seq)
        hs, es, ws, cnt = _norm_route_call(res_s, norm_g[layer, 1], sc2s, sh2s, w_router_t, b_router, cnt,
                                           per_row=True, tr=SAMPLE_ROWS)
        y, row_start = _moe(hp, hs, ep, es, cnt[:, 0].astype(I32), w_gate, w_up, w_down, layer)
        res_p = _combine_call(ep, row_start, res_p, g2p, wp, y, per_row=False, tr=256, rows_per_mod=seq)
        res_s = _combine_call(es, row_start, res_s, g2s, ws, y, per_row=True, tr=SAMPLE_ROWS)

    y_prompt = res_p.reshape(batch, seq, d)
    y_sample = res_s[:dec_batch].reshape(dec_batch, 1, d)
    st = lambda xs: jnp.stack(xs, axis=0)
    return (y_prompt, y_sample, st(new_kv_p[0]), st(new_kv_p[1]), st(new_kv_p[2]), st(bh_p), st(bconv_p),
            st(cconv_p), st(new_kv_s[0]), st(new_kv_s[1]), st(new_kv_s[2]), st(bh_s), st(bconv_s), st(cconv_s))
```

```python
import functools

import numpy as np
import jax
import jax.numpy as jnp
from jax import lax
from jax.experimental import pallas as pl
from jax.experimental.pallas import tpu as pltpu

F32 = jnp.float32
BF16 = jnp.bfloat16
I32 = jnp.int32
U32 = jnp.uint32

HEAD_DIM = 128
A_HEADS = 8
DILATIONS = ((128, 1), (512, 4), (2048, 16))
N_DIL = len(DILATIONS)
A_WIDTH = A_HEADS * HEAD_DIM
A_QKV = N_DIL * 3 * A_WIDTH
ATTN_SCALE = HEAD_DIM ** -0.5
QB = 128
B_BLOCKS = 8
B_CONV = 4
C_CONV = 3
LRU_C = 8.0
N_EXPERTS = 16
N_GROUPS = 4
EXP_PER_GROUP = N_EXPERTS // N_GROUPS
EPS = 1e-6
SAMPLE_ROWS = 16
SUBLANES = 8
LANES = 128
NEG_INF = float("-inf")

VMEM_LIMIT = 52 * 1024 * 1024


def _params(sem, vmem=VMEM_LIMIT):
    return pltpu.CompilerParams(dimension_semantics=sem, vmem_limit_bytes=vmem)


def _mod_kernel(c_ref, w_ref, b_ref, o_ref):
    c = c_ref[...]
    cs = (c * jax.nn.sigmoid(c)).astype(BF16)
    o_ref[...] = jnp.dot(cs, w_ref[...].astype(BF16), preferred_element_type=F32) + b_ref[...]


def _mod_call(c_all, w_mod, b_mod):
    depth, d, n = w_mod.shape
    rows = c_all.shape[0]
    tn = 512
    return pl.pallas_call(
        _mod_kernel,
        grid=(depth, n // tn),
        in_specs=[
            pl.BlockSpec((rows, d), lambda l, j: (0, 0)),
            pl.BlockSpec((None, d, tn), lambda l, j: (l, 0, j)),
            pl.BlockSpec((None, 1, tn), lambda l, j: (l, 0, j)),
        ],
        out_specs=pl.BlockSpec((None, rows, tn), lambda l, j: (l, 0, j)),
        out_shape=jax.ShapeDtypeStruct((depth, rows, n), F32),
        compiler_params=_params(("arbitrary", "arbitrary")),
        name="mod_proj",
    )(c_all, w_mod, b_mod.reshape(depth, 1, n))


def _modnorm(x, g, sc, sh):
    h = x * lax.rsqrt(jnp.mean(x * x, axis=-1, keepdims=True) + EPS) * g
    return h * (1.0 + sc) + sh


def _norm_kernel(x_ref, g_ref, sc_ref, sh_ref, o_ref):
    o_ref[...] = _modnorm(x_ref[...], g_ref[...], sc_ref[...], sh_ref[...]).astype(o_ref.dtype)


def _route(logits_t, bias_ref):
    score = [jax.nn.sigmoid(logits_t[e:e + 1, :]) for e in range(N_EXPERTS)]
    sel = [score[e] + bias_ref[e] for e in range(N_EXPERTS)]
    gsum = []
    for g in range(N_GROUPS):
        a, b, c, d = sel[4 * g:4 * g + 4]
        hi1, lo1 = jnp.maximum(a, b), jnp.minimum(a, b)
        hi2, lo2 = jnp.maximum(c, d), jnp.minimum(c, d)
        top = jnp.maximum(hi1, hi2)
        second = jnp.maximum(jnp.minimum(hi1, hi2), jnp.maximum(lo1, lo2))
        gsum.append(top + second)
    best = gsum[0]
    grp = jnp.zeros_like(best, dtype=I32)
    for g in range(1, N_GROUPS):
        better = gsum[g] > best
        best = jnp.where(better, gsum[g], best)
        grp = jnp.where(better, g, grp)
    v1 = jnp.full_like(best, NEG_INF)
    e1 = jnp.zeros_like(grp)
    for e in range(N_EXPERTS):
        cand = jnp.where(grp == e // EXP_PER_GROUP, sel[e], NEG_INF)
        better = cand > v1
        v1 = jnp.where(better, cand, v1)
        e1 = jnp.where(better, e, e1)
    v2 = jnp.full_like(best, NEG_INF)
    e2 = jnp.zeros_like(grp)
    for e in range(N_EXPERTS):
        cand = jnp.where((grp == e // EXP_PER_GROUP) & (e1 != e), sel[e], NEG_INF)
        better = cand > v2
        v2 = jnp.where(better, cand, v2)
        e2 = jnp.where(better, e, e2)
    s1 = jnp.zeros_like(best)
    s2 = jnp.zeros_like(best)
    for e in range(N_EXPERTS):
        s1 = jnp.where(e1 == e, score[e], s1)
        s2 = jnp.where(e2 == e, score[e], s2)
    tot = s1 + s2
    return e1, e2, s1 / tot, s2 / tot


def _norm_route_kernel(bias_ref, x_ref, g_ref, sc_ref, sh_ref, wr_ref, cnt0_ref,
                       o_ref, e_ref, w_ref, cnt_ref, run_ref):
    @pl.when(pl.program_id(0) == 0)
    def _():
        run_ref[...] = cnt0_ref[...]

    h = _modnorm(x_ref[...], g_ref[...], sc_ref[...], sh_ref[...])
    o_ref[...] = h
    nt = (((1,), (1,)), ((), ()))
    logits_t = lax.dot_general(wr_ref[...].astype(BF16), h.astype(BF16), nt, preferred_element_type=F32)
    e1, e2, w1, w2 = _route(logits_t, bias_ref)
    n = e1.shape[1]
    expert = lax.broadcasted_iota(I32, (N_EXPERTS, n), 0)
    oh1 = (expert == e1).astype(F32)
    oh2 = (expert == e2).astype(F32)
    before = (lax.broadcasted_iota(I32, (n, n), 0) < lax.broadcasted_iota(I32, (n, n), 1)).astype(BF16)
    pre1 = jnp.dot(oh1.astype(BF16), before, preferred_element_type=F32)
    pre2 = jnp.dot(oh2.astype(BF16), before, preferred_element_type=F32)
    tot1 = jnp.sum(oh1, axis=1, keepdims=True)
    tot2 = jnp.sum(oh2, axis=1, keepdims=True)
    base = run_ref[...][:, :1]
    r1 = jnp.sum(oh1 * (base + pre1), axis=0, keepdims=True)
    r2 = jnp.sum(oh2 * (base + tot1 + pre2), axis=0, keepdims=True)
    run_ref[...] = run_ref[...] + (tot1 + tot2)
    cnt_ref[...] = run_ref[...]
    e_ref[...] = jnp.concatenate([e1, e2, r1.astype(I32), r2.astype(I32), jnp.zeros((SUBLANES - 4, n), I32)], axis=0)
    w_ref[...] = jnp.concatenate([w1, w2, jnp.zeros((SUBLANES - 2, n), F32)], axis=0)


def _mod_specs(per_row, tr, d, rows_per_mod):
    if per_row:
        spec = pl.BlockSpec((tr, d), lambda i, *_: (i, 0))
    else:
        spec = pl.BlockSpec((None, 1, d), lambda i, *_: (i * tr // rows_per_mod, 0, 0))
    return spec


def _norm_call(x, g, sc, sh, *, per_row, tr, rows_per_mod=None):
    rows, d = x.shape
    mspec = _mod_specs(per_row, tr, d, rows_per_mod)
    return pl.pallas_call(
        _norm_kernel,
        grid=(rows // tr,),
        in_specs=[pl.BlockSpec((tr, d), lambda i: (i, 0)),
                  pl.BlockSpec((1, d), lambda i: (0, 0)), mspec, mspec],
        out_specs=pl.BlockSpec((tr, d), lambda i: (i, 0)),
        out_shape=jax.ShapeDtypeStruct((rows, d), BF16),
        compiler_params=_params(("arbitrary",)),
        name="modnorm",
    )(x, g.reshape(1, d), sc, sh)


def _norm_route_call(x, g, sc, sh, w_router_t, b_router, cnt0, *, per_row, tr, rows_per_mod=None):
    rows, d = x.shape
    mspec = _mod_specs(per_row, tr, d, rows_per_mod)
    cspec = pl.BlockSpec((N_EXPERTS, LANES), lambda i, b: (0, 0))
    grid_spec = pltpu.PrefetchScalarGridSpec(
        num_scalar_prefetch=1,
        grid=(rows // tr,),
        in_specs=[pl.BlockSpec((tr, d), lambda i, b: (i, 0)),
                  pl.BlockSpec((1, d), lambda i, b: (0, 0)), mspec, mspec,
                  pl.BlockSpec((N_EXPERTS, d), lambda i, b: (0, 0)), cspec],
        out_specs=[pl.BlockSpec((tr, d), lambda i, b: (i, 0)),
                   pl.BlockSpec((SUBLANES, tr), lambda i, b: (0, i)),
                   pl.BlockSpec((SUBLANES, tr), lambda i, b: (0, i)), cspec],
        scratch_shapes=[pltpu.VMEM((N_EXPERTS, LANES), F32)],
    )
    return pl.pallas_call(
        _norm_route_kernel,
        grid_spec=grid_spec,
        out_shape=[jax.ShapeDtypeStruct((rows, d), F32),
                   jax.ShapeDtypeStruct((SUBLANES, rows), I32),
                   jax.ShapeDtypeStruct((SUBLANES, rows), F32),
                   jax.ShapeDtypeStruct((N_EXPERTS, LANES), F32)],
        compiler_params=_params(("arbitrary",)),
        name="modnorm_route",
    )(b_router, x, g.reshape(1, d), sc, sh, w_router_t, cnt0)


def _dense_kernel(*refs, n_x, residual):
    xp_refs = refs[:n_x]
    if residual:
        (xs_ref, w_ref, rp_ref, gp_ref, rs_ref, gs_ref, op_ref, os_ref, wb_ref) = refs[n_x:]
    else:
        (xs_ref, w_ref, op_ref, os_ref, wb_ref) = refs[n_x:]

    @pl.when(pl.program_id(1) == 0)
    def _():
        wb_ref[...] = w_ref[...].astype(BF16)
        acc_s = jnp.dot(xs_ref[...], wb_ref[...], preferred_element_type=F32)
        if residual:
            acc_s = rs_ref[...] + gs_ref[...] * acc_s
        os_ref[...] = acc_s.astype(os_ref.dtype)

    acc = None
    k0 = 0
    for x_ref in xp_refs:
        kx = x_ref.shape[1]
        part = jnp.dot(x_ref[...], wb_ref[k0:k0 + kx, :], preferred_element_type=F32)
        acc = part if acc is None else acc + part
        k0 += kx
    if residual:
        acc = rp_ref[...] + gp_ref[...] * acc
    op_ref[...] = acc.astype(op_ref.dtype)


def _dense_call(xps, xs, w, li, *, res=None, rows_per_mod=None, out_dtype=F32, tm=1024, tn=512):
    tp = xps[0].shape[0]
    k, n = w.shape[1:]
    rs = xs.shape[0]
    in_specs = [pl.BlockSpec((tm, x.shape[1]), lambda j, i: (i, 0)) for x in xps]
    in_specs += [pl.BlockSpec((rs, k), lambda j, i: (0, 0)),
                 pl.BlockSpec((None, k, tn), lambda j, i: (li, 0, j))]
    args = list(xps) + [xs, w]
    if res is not None:
        in_specs += [pl.BlockSpec((tm, tn), lambda j, i: (i, j)),
                     pl.BlockSpec((None, 1, tn), lambda j, i: (i * tm // rows_per_mod, 0, j)),
                     pl.BlockSpec((rs, tn), lambda j, i: (0, j)),
                     pl.BlockSpec((rs, tn), lambda j, i: (0, j))]
        args += list(res)
    return pl.pallas_call(
        functools.partial(_dense_kernel, n_x=len(xps), residual=res is not None),
        grid=(n // tn, tp // tm),
        in_specs=in_specs,
        out_specs=[pl.BlockSpec((tm, tn), lambda j, i: (i, j)),
                   pl.BlockSpec((rs, tn), lambda j, i: (0, j))],
        out_shape=[jax.ShapeDtypeStruct((tp, n), out_dtype), jax.ShapeDtypeStruct((rs, n), out_dtype)],
        scratch_shapes=[pltpu.VMEM((k, tn), BF16)],
        compiler_params=_params(("arbitrary", "arbitrary")),
        name="dense_proj",
    )(*args)


def _rms_rows(x, g):
    return x * lax.rsqrt(jnp.mean(x * x, axis=-1, keepdims=True) + EPS) * g


def _attn_prompt_kernel(q0, k0, v0, q1, k1, v1, q2, k2, v2, gq_ref, gk_ref, slope_ref,
                        ya_ref, kv0, kv1, kv2, qn_s, m_s, l_s, acc_s, kn0, kn1, kn2, kv_sem, *, seq):
    qkv = ((q0, k0, v0, kn0), (q1, k1, v1, kn1), (q2, k2, v2, kn2))
    kv_out = (kv0, kv1, kv2)
    b_idx = pl.program_id(0)
    h_idx = pl.program_id(1)

    def kv_copy(g, c):
        keep = kv_out[g].shape[1]
        src = (qkv[g][3], qkv[g][2])[c].at[pl.ds(seq - keep, keep)]
        dst = kv_out[g].at[b_idx, pl.ds(0, keep), c, h_idx, pl.ds(0, HEAD_DIM)]
        return pltpu.make_async_copy(src, dst, kv_sem.at[2 * g + c])

    slope = slope_ref[...]
    row = lax.broadcasted_iota(I32, (QB, 2 * QB), 0)
    col = lax.broadcasted_iota(I32, (QB, 2 * QB), 1)
    rel_prev = row + QB - col
    rel_first = (row - col)[:, :QB]
    nt = (((1,), (1,)), ((), ()))

    for g, (window, dil) in enumerate(DILATIONS):
        q_ref, k_ref, v_ref, kn_ref = qkv[g]
        nk = window // dil
        n_blk = seq // dil // QB
        qn_s[...] = _rms_rows(q_ref[...], gq_ref[g])
        kn_ref[...] = _rms_rows(k_ref[...], gk_ref[g])
        kv_copy(g, 0).start()
        kv_copy(g, 1).start()
        sl = slope[:, :1] * float(dil)
        bias_prev = jnp.where((rel_prev >= 0) & (rel_prev <= nk), -sl * rel_prev.astype(F32), NEG_INF)
        bias_first = jnp.where(rel_first >= 0, -sl * rel_first.astype(F32), NEG_INF)

        def rows(start, size, dil=dil):
            return pl.ds(start, size) if dil == 1 else pl.ds(start, size, stride=dil)

        def block(qstart, kstart, first, g=g, kn_ref=kn_ref, v_ref=v_ref, rows=rows,
                  bias_prev=bias_prev, bias_first=bias_first):
            nkeys = QB if first else 2 * QB
            qb = qn_s[rows(qstart, QB), :].astype(BF16)
            kb = kn_ref[rows(kstart, nkeys), :].astype(BF16)
            vb = v_ref[rows(kstart, nkeys), :].astype(BF16)
            s = lax.dot_general(qb, kb, nt, preferred_element_type=F32) * ATTN_SCALE
            s = s + (bias_first if first else bias_prev)
            m_blk = jnp.max(s, axis=-1, keepdims=True)
            p = jnp.exp(s - m_blk)
            l_blk = jnp.sum(p, axis=-1, keepdims=True)
            o_blk = jnp.dot(p.astype(BF16), vb, preferred_element_type=F32)
            qrows = rows(qstart, QB)
            if g == 0:
                m_s[qrows, :] = jnp.broadcast_to(m_blk, (QB, HEAD_DIM))
                l_s[qrows, :] = jnp.broadcast_to(l_blk, (QB, HEAD_DIM))
                acc_s[qrows, :] = o_blk
            else:
                m_old = m_s[qrows, :]
                m_new = jnp.maximum(m_old, m_blk)
                a_old = jnp.exp(m_old - m_new)
                a_blk = jnp.exp(m_blk - m_new)
                m_s[qrows, :] = m_new
                l_s[qrows, :] = l_s[qrows, :] * a_old + l_blk * a_blk
                acc_s[qrows, :] = acc_s[qrows, :] * a_old + o_blk * a_blk

        if dil == 1:
            block(0, 0, True)

            def body(n, carry, block=block):
                qstart = pl.multiple_of(n * QB, QB)
                block(qstart, pl.multiple_of(qstart - QB, QB), False)
                return carry

            lax.fori_loop(1, n_blk, body, 0)
        else:
            def body(r, carry, block=block, n_blk=n_blk, dil=dil):
                block(r, r, True)
                for n in range(1, n_blk):
                    block(r + n * QB * dil, r + (n - 1) * QB * dil, False)
                return carry

            lax.fori_loop(0, dil, body, 0)

    ya_ref[...] = (acc_s[...] / l_s[...]).astype(ya_ref.dtype)
    for g in range(N_DIL):
        kv_copy(g, 0).wait()
        kv_copy(g, 1).wait()


def _attn_prompt_call(z, gq, gk, slopes, *, batch, seq):
    t = batch * seq
    kv_shapes = [jax.ShapeDtypeStruct((batch, min(window, seq), 2, A_HEADS, HEAD_DIM), F32)
                 for window, _ in DILATIONS]

    def col(g, c):
        return lambda b, h: (b, g * 3 * A_HEADS + c * A_HEADS + h)

    in_specs = []
    for g in range(N_DIL):
        for c in range(3):
            in_specs.append(pl.BlockSpec((seq, HEAD_DIM), col(g, c)))
    gspec = pl.BlockSpec((N_DIL, None, 1, HEAD_DIM), lambda b, h: (0, h, 0, 0))
    in_specs += [gspec, gspec, pl.BlockSpec((None, 1, HEAD_DIM), lambda b, h: (h, 0, 0))]
    outs = pl.pallas_call(
        functools.partial(_attn_prompt_kernel, seq=seq),
        grid=(batch, A_HEADS),
        in_specs=in_specs,
        out_specs=[pl.BlockSpec((seq, HEAD_DIM), lambda b, h: (b, h))] + [pl.BlockSpec(memory_space=pl.ANY)] * 3,
        out_shape=[jax.ShapeDtypeStruct((t, A_WIDTH), BF16)] + kv_shapes,
        scratch_shapes=[pltpu.VMEM((seq, HEAD_DIM), F32)] * 7 + [pltpu.SemaphoreType.DMA((2 * N_DIL,))],
        compiler_params=_params(("arbitrary", "arbitrary")),
        name="attn_prompt",
    )(*([z] * 9), gq.reshape(N_DIL, A_HEADS, 1, HEAD_DIM), gk.reshape(N_DIL, A_HEADS, 1, HEAD_DIM), slopes)
    return outs[0], outs[1:]


def _round_bf16(x):
    return x.astype(BF16).astype(F32)


def _attn_sample_kernel(z_ref, c0_ref, c1_ref, c2_ref, gq_ref, gk_ref, slope_ref, ya_ref, new_ref):
    caches = (c0_ref, c1_ref, c2_ref)
    slope = slope_ref[...][:, :1]
    nk = DILATIONS[0][0] // DILATIONS[0][1]
    steps = (nk - lax.broadcasted_iota(I32, (nk, 1, 1), 0)).astype(F32)
    m_run = l_run = acc = None
    for g, (window, dil) in enumerate(DILATIONS):
        base = g * 3 * A_HEADS
        q = _rms_rows(z_ref[base:base + A_HEADS, :], gq_ref[g])
        k = _rms_rows(z_ref[base + A_HEADS:base + 2 * A_HEADS, :], gk_ref[g])
        v = z_ref[base + 2 * A_HEADS:base + 3 * A_HEADS, :]
        new_ref[g, 0] = k
        new_ref[g, 1] = v
        qr = _round_bf16(q)
        ck = _round_bf16(caches[g][:, 0])
        cv = _round_bf16(caches[g][:, 1])
        s_c = jnp.sum(ck * qr[None], axis=-1, keepdims=True) * ATTN_SCALE
        s_c = s_c - (slope * float(dil))[None] * steps
        s_n = jnp.sum(_round_bf16(k) * qr, axis=-1, keepdims=True) * ATTN_SCALE
        m_g = jnp.maximum(jnp.max(s_c, axis=0), s_n)
        p_c = jnp.exp(s_c - m_g[None])
        p_n = jnp.exp(s_n - m_g)
        l_g = jnp.sum(p_c, axis=0) + p_n
        o_g = jnp.sum(_round_bf16(p_c) * cv, axis=0) + _round_bf16(p_n) * _round_bf16(v)
        if g == 0:
            m_run, l_run, acc = m_g, l_g, o_g
        else:
            m_new = jnp.maximum(m_run, m_g)
            a_old = jnp.exp(m_run - m_new)
            a_g = jnp.exp(m_g - m_new)
            l_run = l_run * a_old + l_g * a_g
            acc = acc * a_old + o_g * a_g
            m_run = m_new
    ya_ref[...] = acc / l_run


def _attn_sample_call(z_s, caches, gq, gk, slopes8, *, dec_batch):
    n_colblk = z_s.shape[1] // HEAD_DIM
    z3 = z_s.reshape(z_s.shape[0], n_colblk, HEAD_DIM)
    in_specs = [pl.BlockSpec((None, n_colblk, HEAD_DIM), lambda b: (b, 0, 0))]
    args = [z3]
    for g, (window, dil) in enumerate(DILATIONS):
        nk = window // dil
        c = caches[g].reshape(dec_batch, nk, dil, 2, A_HEADS, HEAD_DIM)
        in_specs.append(pl.BlockSpec((None, nk, None, 2, A_HEADS, HEAD_DIM), lambda b: (b, 0, 0, 0, 0, 0)))
        args.append(c)
    full = lambda shape: pl.BlockSpec(shape, lambda b: (0,) * len(shape))
    in_specs += [full((N_DIL, A_HEADS, HEAD_DIM)), full((N_DIL, A_HEADS, HEAD_DIM)), full((A_HEADS, HEAD_DIM))]
    return pl.pallas_call(
        _attn_sample_kernel,
        grid=(dec_batch,),
        in_specs=in_specs,
        out_specs=[pl.BlockSpec((None, A_HEADS, HEAD_DIM), lambda b: (b, 0, 0)),
                   pl.BlockSpec((None, N_DIL, 2, A_HEADS, HEAD_DIM), lambda b: (b, 0, 0, 0, 0))],
        out_shape=[jax.ShapeDtypeStruct((dec_batch, A_HEADS, HEAD_DIM), F32),
                   jax.ShapeDtypeStruct((dec_batch, N_DIL, 2, A_HEADS, HEAD_DIM), F32)],
        compiler_params=_params(("arbitrary",)),
        name="attn_sample",
    )(*args, gq, gk, slopes8)


def _cache_roll_kernel(cur_ref, nxt_ref, new_ref, o_ref):
    last = pl.program_id(1) == pl.num_programs(1) - 1
    rows = cur_ref.shape[0]
    o_ref[0:rows - 1] = cur_ref[1:rows]
    o_ref[rows - 1:rows] = jnp.where(last, new_ref[...], nxt_ref[...])


def _cache_roll_call(cache, new, *, rows=256):
    b, w = cache.shape[:2]
    rows = min(rows, w)
    tail = cache.shape[2:]
    n_chunks = w // rows
    zeros = (0,) * len(tail)
    return pl.pallas_call(
        _cache_roll_kernel,
        grid=(b, n_chunks),
        in_specs=[pl.BlockSpec((None, rows) + tail, lambda i, j: (i, j) + zeros),
                  pl.BlockSpec((None, 1) + tail, lambda i, j: (i, jnp.minimum((j + 1) * rows, w - 1)) + zeros),
                  pl.BlockSpec((None, 1) + tail, lambda i, j: (i, 0) + zeros)],
        out_specs=pl.BlockSpec((None, rows) + tail, lambda i, j: (i, j) + zeros),
        out_shape=jax.ShapeDtypeStruct(cache.shape, cache.dtype),
        compiler_params=_params(("arbitrary", "arbitrary")),
        name="cache_roll",
    )(cache, cache, new)


def _softplus(x):
    return jnp.maximum(x, 0.0) + jnp.log1p(jnp.exp(-jnp.abs(x)))


def _lru_coeffs(xc, wa, ba, wi, bi, lam):
    xcb = xc.astype(BF16)
    r = jax.nn.sigmoid(jnp.dot(xcb, wa.astype(BF16), preferred_element_type=F32) + ba)
    i = jax.nn.sigmoid(jnp.dot(xcb, wi.astype(BF16), preferred_element_type=F32) + bi)
    log_a = -LRU_C * r * _softplus(-lam)
    a = jnp.exp(log_a)
    t = jnp.tanh(log_a)
    one_minus_a2 = -2.0 * t / (1.0 - t)
    return a, jnp.sqrt(one_minus_a2) * (i * xc)


def _lru_prompt_kernel(x_ref, gate_ref, cw_ref, cb_ref, wa_ref, ba_ref, wi_ref, bi_ref, lam_ref,
                       y_ref, hl_ref, pad_a, pad_b, *, seq):
    blk = x_ref.shape[1]
    off = SUBLANES
    x = x_ref[...]
    pad_a[0:off, :] = jnp.zeros((off, blk), F32)
    pad_a[off:off + seq, :] = x
    xc = pad_a[off - 3:off - 3 + seq, :] * cw_ref[0:1, :]
    xc = xc + pad_a[off - 2:off - 2 + seq, :] * cw_ref[1:2, :]
    xc = xc + pad_a[off - 1:off - 1 + seq, :] * cw_ref[2:3, :]
    xc = xc + x * cw_ref[3:4, :] + cb_ref[...]
    a, b = _lru_coeffs(xc, wa_ref[...], ba_ref[...], wi_ref[...], bi_ref[...], lam_ref[...])

    tpos = lax.broadcasted_iota(I32, (seq, blk), 0) & (SUBLANES - 1)
    pad_b[0:off, :] = jnp.zeros((off, blk), F32)
    for o in (1, 2, 4):
        pad_a[off:off + seq, :] = a
        pad_b[off:off + seq, :] = b
        keep = tpos >= o
        a_prev = jnp.where(keep, pad_a[off - o:off - o + seq, :], 1.0)
        b_prev = jnp.where(keep, pad_b[off - o:off - o + seq, :], 0.0)
        b = a * b_prev + b
        a = a * a_prev
    pad_a[off:off + seq, :] = a
    pad_b[off:off + seq, :] = b

    def tile(i, h_prev):
        r0 = pl.multiple_of(off + i * SUBLANES, SUBLANES)
        h = pad_b[pl.ds(r0, SUBLANES), :] + pad_a[pl.ds(r0, SUBLANES), :] * h_prev
        pad_b[pl.ds(r0, SUBLANES), :] = h
        return jnp.broadcast_to(h[SUBLANES - 1:SUBLANES, :], (SUBLANES, blk))

    lax.fori_loop(0, seq // SUBLANES, tile, jnp.zeros((SUBLANES, blk), F32))
    h = pad_b[off:off + seq, :]
    hl_ref[...] = h[seq - SUBLANES:, :]
    y_ref[...] = (h * jax.nn.gelu(gate_ref[...])).astype(y_ref.dtype)


def _lru_prompt_call(z, cw, cb, wa, ba, wi, bi, lam, *, batch, seq):
    t = batch * seq
    nb, blk, _ = wa.shape
    width = nb * blk
    x0 = A_QKV // blk
    g0 = (A_QKV + width) // blk
    vec = lambda v: v.reshape(1, width)
    vspec = pl.BlockSpec((1, blk), lambda b, n: (0, n))
    return pl.pallas_call(
        functools.partial(_lru_prompt_kernel, seq=seq),
        grid=(batch, nb),
        in_specs=[pl.BlockSpec((seq, blk), lambda b, n: (b, x0 + n)),
                  pl.BlockSpec((seq, blk), lambda b, n: (b, g0 + n)),
                  pl.BlockSpec((B_CONV, blk), lambda b, n: (0, n)), vspec,
                  pl.BlockSpec((None, blk, blk), lambda b, n: (n, 0, 0)), vspec,
                  pl.BlockSpec((None, blk, blk), lambda b, n: (n, 0, 0)), vspec, vspec],
        out_specs=[pl.BlockSpec((seq, blk), lambda b, n: (b, n)),
                   pl.BlockSpec((None, SUBLANES, blk), lambda b, n: (b, 0, n))],
        out_shape=[jax.ShapeDtypeStruct((t, width), BF16),
                   jax.ShapeDtypeStruct((batch, SUBLANES, width), F32)],
        scratch_shapes=[pltpu.VMEM((seq + SUBLANES, blk), F32)] * 2,
        compiler_params=_params(("arbitrary", "arbitrary")),
        name="rglru_prompt",
    )(z, z, cw, vec(cb), wa, vec(ba), wi, vec(bi), vec(lam))


def _lru_sample_kernel(x_ref, gate_ref, buf_ref, h0_ref, cw_ref, cb_ref, wa_ref, ba_ref, wi_ref, bi_ref,
                       lam_ref, y_ref, h_ref):
    nb, blk, _ = wa_ref.shape
    for n in range(nb):
        cs = slice(n * blk, (n + 1) * blk)
        x = x_ref[:, cs]
        xc = buf_ref[0, :, cs] * cw_ref[0:1, cs]
        xc = xc + buf_ref[1, :, cs] * cw_ref[1:2, cs]
        xc = xc + buf_ref[2, :, cs] * cw_ref[2:3, cs]
        xc = xc + x * cw_ref[3:4, cs] + cb_ref[:, cs]
        a, b = _lru_coeffs(xc, wa_ref[n], ba_ref[:, cs], wi_ref[n], bi_ref[:, cs], lam_ref[:, cs])
        h = b + a * h0_ref[:, cs]
        h_ref[:, cs] = h
        y_ref[:, cs] = (h * jax.nn.gelu(gate_ref[:, cs])).astype(y_ref.dtype)


def _lru_sample_call(x, gate, buf, h0, cw, cb, wa, ba, wi, bi, lam):
    rows, width = x.shape
    vec = lambda v: v.reshape(1, width)
    return pl.pallas_call(
        _lru_sample_kernel,
        out_shape=[jax.ShapeDtypeStruct((rows, width), BF16), jax.ShapeDtypeStruct((rows, width), F32)],
        compiler_params=pltpu.CompilerParams(vmem_limit_bytes=VMEM_LIMIT),
        name="rglru_sample",
    )(x, gate, buf, h0, cw, vec(cb), wa, vec(ba), wi, vec(bi), vec(lam))


def _proj_gconv_kernel(xp_ref, xs_ref, w_hbm, cw_ref, buf_ref, m_ref, ut_ref, ms_ref, us_ref,
                       w_f32, w_cat, pad, sem, *, li, tiles_per_seq):
    j = pl.program_id(0)
    i = pl.program_id(1)
    tm = xp_ref.shape[0]
    blk = cw_ref.shape[1]
    n_piece = 3
    d_model = w_hbm.shape[2] // n_piece

    def w_copy(k, jj):
        col = pl.multiple_of(k * d_model + jj * blk, blk)
        return pltpu.make_async_copy(w_hbm.at[li, :, pl.ds(col, blk)], w_f32.at[k], sem.at[k])

    @pl.when((j == 0) & (i == 0))
    def _():
        for k in range(n_piece):
            w_copy(k, 0).start()

    @pl.when(i == 0)
    def _():
        for k in range(n_piece):
            w_copy(k, j).wait()
            w_cat[:, k * blk:(k + 1) * blk] = w_f32[k].astype(BF16)

        @pl.when(j + 1 < pl.num_programs(0))
        def _():
            for k in range(n_piece):
                w_copy(k, j + 1).start()

        acc_s = jnp.dot(xs_ref[...], w_cat[...], preferred_element_type=F32)
        u_s = acc_s[:, blk:2 * blk] * acc_s[:, 2 * blk:]
        uc_s = buf_ref[0] * cw_ref[0:1, :] + buf_ref[1] * cw_ref[1:2, :]
        uc_s = uc_s + u_s * cw_ref[2:3, :]
        ms_ref[...] = (acc_s[:, :blk] * uc_s).astype(ms_ref.dtype)
        us_ref[...] = u_s

    @pl.when(i % tiles_per_seq == 0)
    def _():
        pad[0:SUBLANES, :] = jnp.zeros((SUBLANES, blk), F32)

    acc = jnp.dot(xp_ref[...], w_cat[...], preferred_element_type=F32)
    u = acc[:, blk:2 * blk] * acc[:, 2 * blk:]
    off = SUBLANES
    pad[off:off + tm, :] = u
    uc = pad[off - 2:off - 2 + tm, :] * cw_ref[0:1, :]
    uc = uc + pad[off - 1:off - 1 + tm, :] * cw_ref[1:2, :]
    uc = uc + u * cw_ref[2:3, :]
    m_ref[...] = (acc[:, :blk] * uc).astype(m_ref.dtype)
    tail = u[tm - SUBLANES:, :]
    ut_ref[...] = tail
    pad[0:SUBLANES, :] = tail


def _proj_gconv_call(xp, xs, w, li, cw, buf_s, *, batch, seq, tm=1024, blk=256):
    t, k = xp.shape
    rs = xs.shape[0]
    d = w.shape[2] // 3
    nblk = d // blk
    tiles_per_seq = seq // tm
    return pl.pallas_call(
        functools.partial(_proj_gconv_kernel, li=li, tiles_per_seq=tiles_per_seq),
        grid=(nblk, t // tm),
        in_specs=[pl.BlockSpec((tm, k), lambda j, i: (i, 0)),
                  pl.BlockSpec((rs, k), lambda j, i: (0, 0)),
                  pl.BlockSpec(memory_space=pl.ANY),
                  pl.BlockSpec((C_CONV, blk), lambda j, i: (0, j)),
                  pl.BlockSpec((C_CONV - 1, rs, blk), lambda j, i: (0, 0, j))],
        out_specs=[pl.BlockSpec((tm, blk), lambda j, i: (i, j)),
                   pl.BlockSpec((None, SUBLANES, blk), lambda j, i: (i // tiles_per_seq, 0, j)),
                   pl.BlockSpec((rs, blk), lambda j, i: (0, j)),
                   pl.BlockSpec((rs, blk), lambda j, i: (0, j))],
        out_shape=[jax.ShapeDtypeStruct((t, d), BF16), jax.ShapeDtypeStruct((batch, SUBLANES, d), F32),
                   jax.ShapeDtypeStruct((rs, d), BF16), jax.ShapeDtypeStruct((rs, d), F32)],
        scratch_shapes=[pltpu.VMEM((3, k, blk), F32), pltpu.VMEM((k, 3 * blk), BF16),
                        pltpu.VMEM((tm + SUBLANES, blk), F32), pltpu.SemaphoreType.DMA((3,))],
        compiler_params=_params(("arbitrary", "arbitrary")),
        name="proj_gconv",
    )(xp, xs, w, cw, buf_s)


MOE_TM = 512
MOE_HALF = MOE_TM // 2


def _bucket_row(route_ref, start_ref, k, t):
    return start_ref[0, route_ref[k, t]] + route_ref[2 + k, t]


def _bucket_kernel(rp_ref, rs_ref, start_ref, ztile_ref, hp_ref, hs_ref, xs_ref, zero_buf, sem, zsem):
    step = pl.program_id(0)
    tr = hp_ref.shape[0]
    n_s = rs_ref.shape[1]

    def copy_p(t, k):
        row = _bucket_row(rp_ref, start_ref, k, t)
        return pltpu.make_async_copy(hp_ref.at[pl.ds(t, 1)], xs_ref.at[pl.ds(row, 1)], sem)

    def copy_s(t, k):
        row = _bucket_row(rs_ref, start_ref, k, t)
        return pltpu.make_async_copy(hs_ref.at[pl.ds(t, 1)], xs_ref.at[pl.ds(row, 1)], sem)

    def copy_z(j):
        start = pl.multiple_of(ztile_ref[0, j] * MOE_HALF, MOE_HALF)
        return pltpu.make_async_copy(zero_buf, xs_ref.at[pl.ds(start, MOE_HALF)], zsem)

    @pl.when(step == 0)
    def _():
        zero_buf[...] = jnp.zeros_like(zero_buf)
        for j in range(ztile_ref.shape[1]):
            pl.when(ztile_ref[1, j] == 1)(lambda j=j: copy_z(j).start())
        for j in range(ztile_ref.shape[1]):
            pl.when(ztile_ref[1, j] == 1)(lambda j=j: copy_z(j).wait())
        for t in range(n_s):
            copy_s(t, 0).start()
            copy_s(t, 1).start()
        for t in range(n_s):
            copy_s(t, 0).wait()
            copy_s(t, 1).wait()

    def start_p(i, c):
        for j in range(SUBLANES):
            copy_p(i * SUBLANES + j, 0).start(priority=0)
            copy_p(i * SUBLANES + j, 1).start(priority=1)
        return c

    def wait_p(t, c):
        copy_p(t, 0).wait()
        copy_p(t, 1).wait()
        return c

    lax.fori_loop(0, tr // SUBLANES, start_p, 0)
    lax.fori_loop(0, tr, wait_p, 0)


def _bucket_call(rp, rs, row_start, ztiles, hp, hs, p_rows, *, tr=256):
    tp, dh = hp.shape
    n_s = rs.shape[1]
    smem = lambda shape, imap: pl.BlockSpec(shape, imap, memory_space=pltpu.SMEM)
    return pl.pallas_call(
        _bucket_kernel,
        grid=(tp // tr,),
        in_specs=[smem((SUBLANES, tr), lambda i: (0, i)), smem((SUBLANES, n_s), lambda i: (0, 0)),
                  smem(row_start.shape, lambda i: (0, 0)), smem(ztiles.shape, lambda i: (0, 0)),
                  pl.BlockSpec((tr, dh), lambda i: (i, 0)),
                  pl.BlockSpec((n_s, dh), lambda i: (0, 0))],
        out_specs=pl.BlockSpec(memory_space=pl.ANY),
        out_shape=jax.ShapeDtypeStruct((p_rows, dh), hp.dtype),
        scratch_shapes=[pltpu.VMEM((MOE_HALF, dh), hp.dtype), pltpu.SemaphoreType.DMA, pltpu.SemaphoreType.DMA],
        compiler_params=_params(("arbitrary",)),
        name="moe_bucket",
    )(rp, rs, row_start, ztiles, hp, hs)


(S_XTILE, S_OTILE, S_EXPERT, S_WCHUNK, S_OCHUNK, S_FIRST, S_VALID, S_FULL,
 S_HAS_NEXT, S_NEXT_EXPERT, S_NEXT_CHUNK) = range(11)


def _grouped_kernel(s_ref, x_ref, *refs, n_w, layer, tn, swiglu):
    w_hbm = refs[:n_w]
    o_ref = refs[n_w]
    w_f32 = refs[n_w + 1:2 * n_w + 1]
    w_b16 = refs[2 * n_w + 1]
    sem = refs[2 * n_w + 2]
    it = pl.program_id(0)

    def w_copy(k, expert, chunk):
        col = pl.multiple_of(chunk * tn, LANES)
        return pltpu.make_async_copy(w_hbm[k].at[layer, expert, :, pl.ds(col, tn)], w_f32[k], sem.at[k])

    @pl.when(it == 0)
    def _():
        for k in range(n_w):
            w_copy(k, s_ref[S_EXPERT, 0], s_ref[S_WCHUNK, 0]).start()

    @pl.when(s_ref[S_FIRST, it] == 1)
    def _():
        for k in range(n_w):
            w_copy(k, s_ref[S_EXPERT, it], s_ref[S_WCHUNK, it]).wait()
            w_b16[:, k * tn:(k + 1) * tn] = w_f32[k][...].astype(BF16)

        @pl.when(s_ref[S_HAS_NEXT, it] == 1)
        def _():
            for k in range(n_w):
                w_copy(k, s_ref[S_NEXT_EXPERT, it], s_ref[S_NEXT_CHUNK, it]).start()

    def compute(rows):
        acc = jnp.dot(x_ref[0:rows, :].astype(BF16), w_b16[...], preferred_element_type=F32)
        if swiglu:
            gate = acc[:, :tn]
            acc = gate * jax.nn.sigmoid(gate) * acc[:, tn:]
        o_ref[0:rows, :] = acc.astype(o_ref.dtype)

    valid = s_ref[S_VALID, it] == 1
    full = s_ref[S_FULL, it] == 1

    @pl.when(valid & full)
    def _():
        compute(MOE_TM)

    @pl.when(valid & jnp.logical_not(full))
    def _():
        compute(MOE_HALF)
        o_ref[MOE_HALF:, :] = jnp.zeros((MOE_TM - MOE_HALF, tn), o_ref.dtype)

    @pl.when(jnp.logical_not(valid))
    def _():
        o_ref[...] = jnp.zeros_like(o_ref)


def _grouped_call(sched, x, weights, layer, *, tn, out_dtype, swiglu, name):
    p, k = x.shape
    n = weights[0].shape[3]
    n_w = len(weights)
    n_items = sched.shape[1]
    grid_spec = pltpu.PrefetchScalarGridSpec(
        num_scalar_prefetch=1,
        grid=(n_items,),
        in_specs=[pl.BlockSpec((MOE_TM, k), lambda it, s: (s[S_XTILE, it], 0))]
                 + [pl.BlockSpec(memory_space=pl.ANY)] * n_w,
        out_specs=pl.BlockSpec((MOE_TM, tn), lambda it, s: (s[S_OTILE, it], s[S_OCHUNK, it])),
        scratch_shapes=[pltpu.VMEM((k, tn), F32)] * n_w + [pltpu.VMEM((k, n_w * tn), BF16),
                                                           pltpu.SemaphoreType.DMA((n_w,))],
    )
    return pl.pallas_call(
        functools.partial(_grouped_kernel, n_w=n_w, layer=layer, tn=tn, swiglu=swiglu),
        grid_spec=grid_spec,
        out_shape=jax.ShapeDtypeStruct((p, n), out_dtype),
        compiler_params=_params(("arbitrary",)),
        name=name,
    )(sched, x, *weights)


def _combine_kernel(route_ref, next_route_ref, start_ref, res_ref, gate_ref, w_ref, y_ref, o_ref, buf, sem, *, tr):
    step = pl.program_id(0)
    slot = step % 2

    def copy(table, s, t, k):
        row = _bucket_row(table, start_ref, k, t)
        return pltpu.make_async_copy(y_ref.at[pl.ds(row, 1)], buf.at[s, k, pl.ds(t, 1)], sem.at[s])

    def issue(table, s):
        def body(i, c):
            for j in range(SUBLANES):
                copy(table, s, i * SUBLANES + j, 0).start(priority=0)
                copy(table, s, i * SUBLANES + j, 1).start(priority=1)
            return c
        lax.fori_loop(0, tr // SUBLANES, body, 0)

    @pl.when(step == 0)
    def _():
        issue(route_ref, 0)

    @pl.when(step + 1 < pl.num_programs(0))
    def _():
        issue(next_route_ref, 1 - slot)

    def wait(t, c):
        copy(route_ref, slot, t, 0).wait()
        copy(route_ref, slot, t, 1).wait()
        return c

    lax.fori_loop(0, tr, wait, 0)
    w = jnp.transpose(w_ref[...])
    y = w[:, 0:1] * buf[slot, 0] + w[:, 1:2] * buf[slot, 1]
    o_ref[...] = res_ref[...] + gate_ref[...] * y


def _combine_call(route, row_start, res, gate, w, y, *, per_row, tr, rows_per_mod=None):
    rows, d = res.shape
    gspec = _mod_specs(per_row, tr, d, rows_per_mod)
    smem = lambda shape, imap: pl.BlockSpec(shape, imap, memory_space=pltpu.SMEM)
    n_tiles = rows // tr
    return pl.pallas_call(
        functools.partial(_combine_kernel, tr=tr),
        grid=(n_tiles,),
        in_specs=[smem((SUBLANES, tr), lambda i: (0, i)),
                  smem((SUBLANES, tr), lambda i: (0, jnp.minimum(i + 1, n_tiles - 1))),
                  smem(row_start.shape, lambda i: (0, 0)),
                  pl.BlockSpec((tr, d), lambda i: (i, 0)), gspec,
                  pl.BlockSpec((SUBLANES, tr), lambda i: (0, i)),
                  pl.BlockSpec(memory_space=pl.ANY)],
        out_specs=pl.BlockSpec((tr, d), lambda i: (i, 0)),
        out_shape=jax.ShapeDtypeStruct((rows, d), F32),
        scratch_shapes=[pltpu.VMEM((2, 2, tr, d), F32), pltpu.SemaphoreType.DMA((2,))],
        compiler_params=_params(("arbitrary",)),
        name="moe_combine",
    )(route, route, row_start, res, gate, w, y)


def _moe_schedule(counts, max_tiles, n_chunks):
    e_ids = jnp.arange(N_EXPERTS, dtype=I32)
    tiles_per_e = (counts + MOE_TM - 1) // MOE_TM
    tile_end = jnp.cumsum(tiles_per_e)
    tile_start = tile_end - tiles_per_e
    used = tile_end[-1]
    n_valid = used * n_chunks
    nonempty = tiles_per_e > 0
    later = (e_ids[None, :] > e_ids[:, None]) & nonempty[None, :]
    next_e = jnp.min(jnp.where(later, e_ids[None, :], N_EXPERTS - 1), axis=1)
    last_e = jnp.max(jnp.where(nonempty, e_ids, 0))
    it = jnp.arange(n_chunks * max_tiles, dtype=I32)
    ex = jnp.minimum(jnp.sum((it[:, None] >= (tile_end * n_chunks)[None, :]).astype(I32), axis=1), N_EXPERTS - 1)
    onehot = (ex[:, None] == e_ids[None, :]).astype(I32)
    look = lambda v: jnp.sum(onehot * v.astype(I32)[None, :], axis=1)
    t_e = jnp.maximum(look(tiles_per_e), 1)
    t0 = look(tile_start)
    local = it - t0 * n_chunks
    chunk = local // t_e
    idx = local % t_e
    valid = it < n_valid
    first = (idx == 0) & valid
    full = look(counts) - idx * MOE_TM > MOE_HALF
    has_next = first & (it + t_e < n_valid)
    last_chunk = chunk == n_chunks - 1
    next_expert = jnp.where(last_chunk, look(next_e), ex)
    next_chunk = jnp.where(last_chunk, 0, chunk + 1)
    q = it - n_valid
    xtile = jnp.where(valid, t0 + idx, used - 1)
    otile = jnp.where(valid, t0 + idx, used + q // n_chunks)
    ochunk = jnp.where(valid, chunk, q % n_chunks)
    return jnp.stack([xtile, otile, jnp.where(valid, ex, last_e), jnp.where(valid, chunk, n_chunks - 1), ochunk,
                      first, valid, full, has_next, next_expert, next_chunk]).astype(I32)


def _moe(hp, hs, rp, rs, counts, w_gate, w_up, w_down, layer):
    n_assign = 2 * (hp.shape[0] + hs.shape[0])
    max_tiles = (n_assign + N_EXPERTS * (MOE_TM - 1)) // MOE_TM
    tiles_per_e = (counts + MOE_TM - 1) // MOE_TM
    tile_end = jnp.cumsum(tiles_per_e)
    row_start = (tile_end - tiles_per_e) * MOE_TM
    half0 = row_start // MOE_HALF
    n_half = 2 * tiles_per_e
    last_real = (counts + MOE_HALF - 1) // MOE_HALF - 1
    unused = 2 * tile_end[-1] + jnp.arange(2 * N_EXPERTS, dtype=I32)
    zhalf = jnp.concatenate([half0 + last_real, half0 + n_half - 1, unused])
    zflag = jnp.concatenate([counts % MOE_HALF != 0, last_real < n_half - 1, unused < 2 * max_tiles])
    ztiles = jnp.stack([jnp.clip(zhalf, 0, 2 * max_tiles - 1), zflag.astype(I32)]).astype(I32)
    gu_chunks, dn_chunks = 3, 2
    row_start = row_start.astype(I32).reshape(1, N_EXPERTS)
    xs = _bucket_call(rp, rs, row_start, ztiles, hp, hs, max_tiles * MOE_TM)
    he = _grouped_call(_moe_schedule(counts, max_tiles, gu_chunks), xs, [w_gate, w_up], layer,
                       tn=w_gate.shape[3] // gu_chunks, out_dtype=BF16, swiglu=True, name="moe_gate_up")
    y = _grouped_call(_moe_schedule(counts, max_tiles, dn_chunks), he, [w_down], layer,
                      tn=w_down.shape[3] // dn_chunks, out_dtype=F32, swiglu=False, name="moe_down")
    return y, row_start


def kernel(x_prompt, x_sample, cache_a_kv0, cache_a_kv1, cache_a_kv2, state_b_h, state_b_conv, state_c_conv,
           c_prompt, c_sample, w_mod, b_mod, norm_g, w_in0, qk_g, b_conv_w, b_conv_b, b_wa, b_ba, b_wi, b_bi,
           b_lambda, w_out0, w_in1, c_conv_w, w_out1, w_router, b_router, w_gate, w_up, w_down):
    batch, seq, d = x_prompt.shape
    dec_batch = x_sample.shape[0]
    depth = w_mod.shape[0]
    tp = batch * seq
    width = b_lambda.shape[1]
    pad_s = SAMPLE_ROWS - dec_batch
    caches = (cache_a_kv0, cache_a_kv1, cache_a_kv2)

    def pad_rows(v):
        return jnp.pad(v, ((0, pad_s),) + ((0, 0),) * (v.ndim - 1))

    res_p = x_prompt.reshape(tp, d)
    res_s = pad_rows(x_sample.reshape(dec_batch, d))
    c_all = jnp.concatenate([c_prompt, jnp.zeros((SUBLANES - batch, d), F32), pad_rows(c_sample)], axis=0)
    mod = _mod_call(c_all, w_mod, b_mod)
    slopes = 2.0 ** (-8.0 * np.arange(1, A_HEADS + 1) / A_HEADS)
    slopes8 = jnp.asarray(np.repeat(slopes[:, None], HEAD_DIM, axis=1), dtype=F32)
    w_router_t = w_router.T

    new_kv_p = [[], [], []]
    new_kv_s = [[], [], []]
    bh_p, bconv_p, cconv_p, bh_s, bconv_s, cconv_s = [], [], [], [], [], []

    for layer in range(depth):
        li = layer // 2
        m = mod[layer]
        chunks_p = [m[:batch, k * d:(k + 1) * d].reshape(batch, 1, d) for k in range(6)]
        chunks_s = [m[SUBLANES:, k * d:(k + 1) * d] for k in range(6)]
        sh1p, sc1p, g1p, sh2p, sc2p, g2p = chunks_p
        sh1s, sc1s, g1s, sh2s, sc2s, g2s = chunks_s

        h_p = _norm_call(res_p, norm_g[layer, 0], sc1p, sh1p, per_row=False, tr=256, rows_per_mod=seq)
        h_s = _norm_call(res_s, norm_g[layer, 0], sc1s, sh1s, per_row=True, tr=SAMPLE_ROWS)

        if layer % 2 == 0:
            z_p, z_s = _dense_call([h_p], h_s, w_in0, li)
            ya_p, kv_p = _attn_prompt_call(z_p, qk_g[li, 0], qk_g[li, 1], slopes8.reshape(A_HEADS, 1, HEAD_DIM),
                                           batch=batch, seq=seq)
            for g in range(N_DIL):
                new_kv_p[g].append(kv_p[g])
            yb_p, hl_p = _lru_prompt_call(z_p, b_conv_w[li], b_conv_b[li], b_wa[li], b_ba[li], b_wi[li],
                                          b_bi[li], b_lambda[li], batch=batch, seq=seq)
            bh_p.append(hl_p[:, SUBLANES - 1])
            bconv_p.append(z_p.reshape(batch, seq, -1)[:, seq - (B_CONV - 1):, A_QKV:A_QKV + width])

            ya_s, kvnew_s = _attn_sample_call(z_s, [c[li] for c in caches], qk_g[li, 0], qk_g[li, 1], slopes8,
                                              dec_batch=dec_batch)
            for g in range(N_DIL):
                new_kv_s[g].append(_cache_roll_call(caches[g][li], kvnew_s[:, g][:, None]))
            xb_s = z_s[:, A_QKV:A_QKV + width]
            gb_s = z_s[:, A_QKV + width:]
            buf = jnp.transpose(pad_rows(state_b_conv[li]), (1, 0, 2))
            yb_s, hnew_s = _lru_sample_call(xb_s, gb_s, buf, pad_rows(state_b_h[li]), b_conv_w[li], b_conv_b[li],
                                            b_wa[li], b_ba[li], b_wi[li], b_bi[li], b_lambda[li])
            bh_s.append(hnew_s[:dec_batch])
            bconv_s.append(jnp.concatenate([state_b_conv[li][:, 1:], xb_s[:dec_batch, None]], axis=1))

            mix_p = [ya_p, yb_p]
            mix_s = jnp.concatenate([pad_rows(ya_s.reshape(dec_batch, A_WIDTH)).astype(BF16), yb_s], axis=1)
            w_out = w_out0
        else:
            buf = jnp.transpose(pad_rows(state_c_conv[li]), (1, 0, 2))
            m_p, ut_p, mix_s, u_s = _proj_gconv_call(h_p, h_s, w_in1, li, c_conv_w[li], buf, batch=batch, seq=seq)
            mix_p = [m_p]
            cconv_p.append(ut_p[:, SUBLANES - (C_CONV - 1):])
            cconv_s.append(jnp.concatenate([state_c_conv[li][:, 1:], u_s[:dec_batch, None]], axis=1))
            w_out = w_out1

        res_p, res_s = _dense_call(mix_p, mix_s, w_out, li, res=(res_p, g1p, res_s, g1s), rows_per_mod=seq)

        hp, ep, wp, cnt = _norm_route_call(res_p, norm_g[layer, 1], sc2p, sh2p, w_router_t, b_router,
                                           jnp.zeros((N_EXPERTS, LANES), F32),
                                           per_row=False, tr=256, rows_per_mod=seq)
        hs, es, ws, cnt = _norm_route_call(res_s, norm_g[layer, 1], sc2s, sh2s, w_router_t, b_router, cnt,
                                           per_row=True, tr=SAMPLE_ROWS)
        y, row_start = _moe(hp, hs, ep, es, cnt[:, 0].astype(I32), w_gate, w_up, w_down, layer)
        res_p = _combine_call(ep, row_start, res_p, g2p, wp, y, per_row=False, tr=256, rows_per_mod=seq)
        res_s = _combine_call(es, row_start, res_s, g2s, ws, y, per_row=True, tr=SAMPLE_ROWS)

    y_prompt = res_p.reshape(batch, seq, d)
    y_sample = res_s[:dec_batch].reshape(dec_batch, 1, d)
    st = lambda xs: jnp.stack(xs, axis=0)
    return (y_prompt, y_sample, st(new_kv_p[0]), st(new_kv_p[1]), st(new_kv_p[2]), st(bh_p), st(bconv_p),
            st(cconv_p), st(new_kv_s[0]), st(new_kv_s[1]), st(new_kv_s[2]), st(bh_s), st(bconv_s), st(cconv_s))
```

```python
import functools

import numpy as np
import jax
import jax.numpy as jnp
from jax import lax
from jax.experimental import pallas as pl
from jax.experimental.pallas import tpu as pltpu

F32 = jnp.float32
BF16 = jnp.bfloat16
I32 = jnp.int32
U32 = jnp.uint32

HEAD_DIM = 128
A_HEADS = 8
DILATIONS = ((128, 1), (512, 4), (2048, 16))
N_DIL = len(DILATIONS)
A_WIDTH = A_HEADS * HEAD_DIM
A_QKV = N_DIL * 3 * A_WIDTH
ATTN_SCALE = HEAD_DIM ** -0.5
QB = 128
B_BLOCKS = 8
B_CONV = 4
C_CONV = 3
LRU_C = 8.0
N_EXPERTS = 16
N_GROUPS = 4
EXP_PER_GROUP = N_EXPERTS // N_GROUPS
EPS = 1e-6
SAMPLE_ROWS = 16
SUBLANES = 8
LANES = 128
NEG_INF = float("-inf")

VMEM_LIMIT = 52 * 1024 * 1024
WIDE_VMEM_LIMIT = 58 * 1024 * 1024


def _params(sem, vmem=VMEM_LIMIT):
    return pltpu.CompilerParams(dimension_semantics=sem, vmem_limit_bytes=vmem)


def _mod_kernel(c_ref, w_ref, b_ref, o_ref):
    c = c_ref[...]
    cs = (c * jax.nn.sigmoid(c)).astype(BF16)
    o_ref[...] = jnp.dot(cs, w_ref[...].astype(BF16), preferred_element_type=F32) + b_ref[...]


def _mod_call(c_all, w_mod, b_mod):
    depth, d, n = w_mod.shape
    rows = c_all.shape[0]
    tn = 512
    return pl.pallas_call(
        _mod_kernel,
        grid=(depth, n // tn),
        in_specs=[
            pl.BlockSpec((rows, d), lambda l, j: (0, 0)),
            pl.BlockSpec((None, d, tn), lambda l, j: (l, 0, j)),
            pl.BlockSpec((None, 1, tn), lambda l, j: (l, 0, j)),
        ],
        out_specs=pl.BlockSpec((None, rows, tn), lambda l, j: (l, 0, j)),
        out_shape=jax.ShapeDtypeStruct((depth, rows, n), F32),
        compiler_params=_params(("arbitrary", "arbitrary")),
        name="mod_proj",
    )(c_all, w_mod, b_mod.reshape(depth, 1, n))


def _modnorm(x, g, sc, sh):
    h = x * lax.rsqrt(jnp.mean(x * x, axis=-1, keepdims=True) + EPS) * g
    return h * (1.0 + sc) + sh


def _norm_kernel(x_ref, g_ref, sc_ref, sh_ref, o_ref):
    o_ref[...] = _modnorm(x_ref[...], g_ref[...], sc_ref[...], sh_ref[...]).astype(o_ref.dtype)


def _route(logits_t, bias_ref):
    score = [jax.nn.sigmoid(logits_t[e:e + 1, :]) for e in range(N_EXPERTS)]
    sel = [score[e] + bias_ref[e] for e in range(N_EXPERTS)]
    gsum = []
    for g in range(N_GROUPS):
        a, b, c, d = sel[4 * g:4 * g + 4]
        hi1, lo1 = jnp.maximum(a, b), jnp.minimum(a, b)
        hi2, lo2 = jnp.maximum(c, d), jnp.minimum(c, d)
        top = jnp.maximum(hi1, hi2)
        second = jnp.maximum(jnp.minimum(hi1, hi2), jnp.maximum(lo1, lo2))
        gsum.append(top + second)
    best = gsum[0]
    grp = jnp.zeros_like(best, dtype=I32)
    for g in range(1, N_GROUPS):
        better = gsum[g] > best
        best = jnp.where(better, gsum[g], best)
        grp = jnp.where(better, g, grp)
    v1 = jnp.full_like(best, NEG_INF)
    e1 = jnp.zeros_like(grp)
    for e in range(N_EXPERTS):
        cand = jnp.where(grp == e // EXP_PER_GROUP, sel[e], NEG_INF)
        better = cand > v1
        v1 = jnp.where(better, cand, v1)
        e1 = jnp.where(better, e, e1)
    v2 = jnp.full_like(best, NEG_INF)
    e2 = jnp.zeros_like(grp)
    for e in range(N_EXPERTS):
        cand = jnp.where((grp == e // EXP_PER_GROUP) & (e1 != e), sel[e], NEG_INF)
        better = cand > v2
        v2 = jnp.where(better, cand, v2)
        e2 = jnp.where(better, e, e2)
    s1 = jnp.zeros_like(best)
    s2 = jnp.zeros_like(best)
    for e in range(N_EXPERTS):
        s1 = jnp.where(e1 == e, score[e], s1)
        s2 = jnp.where(e2 == e, score[e], s2)
    tot = s1 + s2
    return e1, e2, s1 / tot, s2 / tot


def _norm_route_kernel(bias_ref, x_ref, g_ref, sc_ref, sh_ref, wr_ref, cnt0_ref,
                       o_ref, e_ref, w_ref, cnt_ref, run_ref):
    @pl.when(pl.program_id(0) == 0)
    def _():
        run_ref[...] = cnt0_ref[...]

    h = _modnorm(x_ref[...], g_ref[...], sc_ref[...], sh_ref[...])
    o_ref[...] = h
    nt = (((1,), (1,)), ((), ()))
    logits_t = lax.dot_general(wr_ref[...].astype(BF16), h.astype(BF16), nt, preferred_element_type=F32)
    e1, e2, w1, w2 = _route(logits_t, bias_ref)
    n = e1.shape[1]
    expert = lax.broadcasted_iota(I32, (N_EXPERTS, n), 0)
    oh1 = (expert == e1).astype(F32)
    oh2 = (expert == e2).astype(F32)
    before = (lax.broadcasted_iota(I32, (n, n), 0) < lax.broadcasted_iota(I32, (n, n), 1)).astype(BF16)
    pre1 = jnp.dot(oh1.astype(BF16), before, preferred_element_type=F32)
    pre2 = jnp.dot(oh2.astype(BF16), before, preferred_element_type=F32)
    tot1 = jnp.sum(oh1, axis=1, keepdims=True)
    tot2 = jnp.sum(oh2, axis=1, keepdims=True)
    base = run_ref[...][:, :1]
    r1 = jnp.sum(oh1 * (base + pre1), axis=0, keepdims=True)
    r2 = jnp.sum(oh2 * (base + tot1 + pre2), axis=0, keepdims=True)
    run_ref[...] = run_ref[...] + (tot1 + tot2)
    cnt_ref[...] = run_ref[...]
    e_ref[...] = jnp.concatenate([e1, e2, r1.astype(I32), r2.astype(I32), jnp.zeros((SUBLANES - 4, n), I32)], axis=0)
    w_ref[...] = jnp.concatenate([w1, w2, jnp.zeros((SUBLANES - 2, n), F32)], axis=0)


def _mod_specs(per_row, tr, d, rows_per_mod):
    if per_row:
        spec = pl.BlockSpec((tr, d), lambda i, *_: (i, 0))
    else:
        spec = pl.BlockSpec((None, 1, d), lambda i, *_: (i * tr // rows_per_mod, 0, 0))
    return spec


def _norm_call(x, g, sc, sh, *, per_row, tr, rows_per_mod=None):
    rows, d = x.shape
    mspec = _mod_specs(per_row, tr, d, rows_per_mod)
    return pl.pallas_call(
        _norm_kernel,
        grid=(rows // tr,),
        in_specs=[pl.BlockSpec((tr, d), lambda i: (i, 0)),
                  pl.BlockSpec((1, d), lambda i: (0, 0)), mspec, mspec],
        out_specs=pl.BlockSpec((tr, d), lambda i: (i, 0)),
        out_shape=jax.ShapeDtypeStruct((rows, d), BF16),
        compiler_params=_params(("arbitrary",)),
        name="modnorm",
    )(x, g.reshape(1, d), sc, sh)


def _norm_route_call(x, g, sc, sh, w_router_t, b_router, cnt0, *, per_row, tr, rows_per_mod=None):
    rows, d = x.shape
    mspec = _mod_specs(per_row, tr, d, rows_per_mod)
    cspec = pl.BlockSpec((N_EXPERTS, LANES), lambda i, b: (0, 0))
    grid_spec = pltpu.PrefetchScalarGridSpec(
        num_scalar_prefetch=1,
        grid=(rows // tr,),
        in_specs=[pl.BlockSpec((tr, d), lambda i, b: (i, 0)),
                  pl.BlockSpec((1, d), lambda i, b: (0, 0)), mspec, mspec,
                  pl.BlockSpec((N_EXPERTS, d), lambda i, b: (0, 0)), cspec],
        out_specs=[pl.BlockSpec((tr, d), lambda i, b: (i, 0)),
                   pl.BlockSpec((SUBLANES, tr), lambda i, b: (0, i)),
                   pl.BlockSpec((SUBLANES, tr), lambda i, b: (0, i)), cspec],
        scratch_shapes=[pltpu.VMEM((N_EXPERTS, LANES), F32)],
    )
    return pl.pallas_call(
        _norm_route_kernel,
        grid_spec=grid_spec,
        out_shape=[jax.ShapeDtypeStruct((rows, d), F32),
                   jax.ShapeDtypeStruct((SUBLANES, rows), I32),
                   jax.ShapeDtypeStruct((SUBLANES, rows), F32),
                   jax.ShapeDtypeStruct((N_EXPERTS, LANES), F32)],
        compiler_params=_params(("arbitrary",)),
        name="modnorm_route",
    )(b_router, x, g.reshape(1, d), sc, sh, w_router_t, cnt0)


def _dense_kernel(*refs, n_x, residual):
    xp_refs = refs[:n_x]
    if residual:
        (xs_ref, w_ref, rp_ref, gp_ref, rs_ref, gs_ref, op_ref, os_ref, wb_ref) = refs[n_x:]
    else:
        (xs_ref, w_ref, op_ref, os_ref, wb_ref) = refs[n_x:]

    @pl.when(pl.program_id(1) == 0)
    def _():
        wb_ref[...] = w_ref[...].astype(BF16)
        acc_s = jnp.dot(xs_ref[...], wb_ref[...], preferred_element_type=F32)
        if residual:
            acc_s = rs_ref[...] + gs_ref[...] * acc_s
        os_ref[...] = acc_s.astype(os_ref.dtype)

    acc = None
    k0 = 0
    for x_ref in xp_refs:
        kx = x_ref.shape[1]
        part = jnp.dot(x_ref[...], wb_ref[k0:k0 + kx, :], preferred_element_type=F32)
        acc = part if acc is None else acc + part
        k0 += kx
    if residual:
        acc = rp_ref[...] + gp_ref[...] * acc
    op_ref[...] = acc.astype(op_ref.dtype)


def _dense_call(xps, xs, w, li, *, res=None, rows_per_mod=None, out_dtype=F32, tm=1024, tn=512):
    tp = xps[0].shape[0]
    k, n = w.shape[1:]
    rs = xs.shape[0]
    in_specs = [pl.BlockSpec((tm, x.shape[1]), lambda j, i: (i, 0)) for x in xps]
    in_specs += [pl.BlockSpec((rs, k), lambda j, i: (0, 0)),
                 pl.BlockSpec((None, k, tn), lambda j, i: (li, 0, j))]
    args = list(xps) + [xs, w]
    if res is not None:
        in_specs += [pl.BlockSpec((tm, tn), lambda j, i: (i, j)),
                     pl.BlockSpec((None, 1, tn), lambda j, i: (i * tm // rows_per_mod, 0, j)),
                     pl.BlockSpec((rs, tn), lambda j, i: (0, j)),
                     pl.BlockSpec((rs, tn), lambda j, i: (0, j))]
        args += list(res)
    return pl.pallas_call(
        functools.partial(_dense_kernel, n_x=len(xps), residual=res is not None),
        grid=(n // tn, tp // tm),
        in_specs=in_specs,
        out_specs=[pl.BlockSpec((tm, tn), lambda j, i: (i, j)),
                   pl.BlockSpec((rs, tn), lambda j, i: (0, j))],
        out_shape=[jax.ShapeDtypeStruct((tp, n), out_dtype), jax.ShapeDtypeStruct((rs, n), out_dtype)],
        scratch_shapes=[pltpu.VMEM((k, tn), BF16)],
        compiler_params=_params(("arbitrary", "arbitrary")),
        name="dense_proj",
    )(*args)


def _dense_wide_kernel(*refs, n_x, residual, li):
    xp_refs = refs[:n_x]
    if residual:
        (xs_ref, w_hbm, rp_ref, gp_ref, rs_ref, gs_ref, op_ref, os_ref, w_f32, w_b16, sem) = refs[n_x:]
    else:
        (xs_ref, w_hbm, op_ref, os_ref, w_f32, w_b16, sem) = refs[n_x:]
    j = pl.program_id(0)
    i = pl.program_id(1)
    tn = w_b16.shape[1]

    def w_copy(jj):
        col = pl.multiple_of(jj * tn, tn)
        return pltpu.make_async_copy(w_hbm.at[li, :, pl.ds(col, tn)], w_f32, sem)

    @pl.when((j == 0) & (i == 0))
    def _():
        w_copy(0).start()

    @pl.when(i == 0)
    def _():
        w_copy(j).wait()
        w_b16[...] = w_f32[...].astype(BF16)

        @pl.when(j + 1 < pl.num_programs(0))
        def _():
            w_copy(j + 1).start()

        acc_s = jnp.dot(xs_ref[...], w_b16[...], preferred_element_type=F32)
        if residual:
            acc_s = rs_ref[...] + gs_ref[...] * acc_s
        os_ref[...] = acc_s.astype(os_ref.dtype)

    acc = None
    k0 = 0
    for x_ref in xp_refs:
        kx = x_ref.shape[1]
        part = jnp.dot(x_ref[...], w_b16[k0:k0 + kx, :], preferred_element_type=F32)
        acc = part if acc is None else acc + part
        k0 += kx
    if residual:
        acc = rp_ref[...] + gp_ref[...] * acc
    op_ref[...] = acc.astype(op_ref.dtype)


def _dense_wide_call(xps, xs, w, li, *, res=None, rows_per_mod=None, out_dtype=F32, tm=1024, tn=1024,
                     vmem=VMEM_LIMIT):
    tp = xps[0].shape[0]
    k, n = w.shape[1:]
    rs = xs.shape[0]
    in_specs = [pl.BlockSpec((tm, x.shape[1]), lambda j, i: (i, 0)) for x in xps]
    in_specs += [pl.BlockSpec((rs, k), lambda j, i: (0, 0)), pl.BlockSpec(memory_space=pl.ANY)]
    args = list(xps) + [xs, w]
    if res is not None:
        in_specs += [pl.BlockSpec((tm, tn), lambda j, i: (i, j)),
                     pl.BlockSpec((None, 1, tn), lambda j, i: (i * tm // rows_per_mod, 0, j)),
                     pl.BlockSpec((rs, tn), lambda j, i: (0, j)),
                     pl.BlockSpec((rs, tn), lambda j, i: (0, j))]
        args += list(res)
    return pl.pallas_call(
        functools.partial(_dense_wide_kernel, n_x=len(xps), residual=res is not None, li=li),
        grid=(n // tn, tp // tm),
        in_specs=in_specs,
        out_specs=[pl.BlockSpec((tm, tn), lambda j, i: (i, j)),
                   pl.BlockSpec((rs, tn), lambda j, i: (0, j))],
        out_shape=[jax.ShapeDtypeStruct((tp, n), out_dtype), jax.ShapeDtypeStruct((rs, n), out_dtype)],
        scratch_shapes=[pltpu.VMEM((k, tn), F32), pltpu.VMEM((k, tn), BF16), pltpu.SemaphoreType.DMA],
        compiler_params=_params(("arbitrary", "arbitrary"), vmem),
        name="dense_wide",
    )(*args)


def _rms_rows(x, g):
    return x * lax.rsqrt(jnp.mean(x * x, axis=-1, keepdims=True) + EPS) * g


def _attn_prompt_kernel(q0, k0, v0, q1, k1, v1, q2, k2, v2, gq_ref, gk_ref, slope_ref,
                        ya_ref, kv0, kv1, kv2, qn_s, m_s, l_s, acc_s, kn0, kn1, kn2, kv_sem, *, seq):
    qkv = ((q0, k0, v0, kn0), (q1, k1, v1, kn1), (q2, k2, v2, kn2))
    kv_out = (kv0, kv1, kv2)
    b_idx = pl.program_id(0)
    h_idx = pl.program_id(1)

    def kv_copy(g, c):
        keep = kv_out[g].shape[1]
        src = (qkv[g][3], qkv[g][2])[c].at[pl.ds(seq - keep, keep)]
        dst = kv_out[g].at[b_idx, pl.ds(0, keep), c, h_idx, pl.ds(0, HEAD_DIM)]
        return pltpu.make_async_copy(src, dst, kv_sem.at[2 * g + c])

    slope = slope_ref[...]
    row = lax.broadcasted_iota(I32, (QB, 2 * QB), 0)
    col = lax.broadcasted_iota(I32, (QB, 2 * QB), 1)
    rel_prev = row + QB - col
    rel_first = (row - col)[:, :QB]
    nt = (((1,), (1,)), ((), ()))

    for g, (window, dil) in enumerate(DILATIONS):
        q_ref, k_ref, v_ref, kn_ref = qkv[g]
        nk = window // dil
        n_blk = seq // dil // QB
        qn_s[...] = _rms_rows(q_ref[...], gq_ref[g])
        kn_ref[...] = _rms_rows(k_ref[...], gk_ref[g])
        kv_copy(g, 0).start()
        kv_copy(g, 1).start()
        sl = slope[:, :1] * float(dil)
        bias_prev = jnp.where((rel_prev >= 0) & (rel_prev <= nk), -sl * rel_prev.astype(F32), NEG_INF)
        bias_first = jnp.where(rel_first >= 0, -sl * rel_first.astype(F32), NEG_INF)

        def rows(start, size, dil=dil):
            return pl.ds(start, size) if dil == 1 else pl.ds(start, size, stride=dil)

        def block(qstart, kstart, first, g=g, kn_ref=kn_ref, v_ref=v_ref, rows=rows,
                  bias_prev=bias_prev, bias_first=bias_first):
            nkeys = QB if first else 2 * QB
            qb = qn_s[rows(qstart, QB), :].astype(BF16)
            kb = kn_ref[rows(kstart, nkeys), :].astype(BF16)
            vb = v_ref[rows(kstart, nkeys), :].astype(BF16)
            s = lax.dot_general(qb, kb, nt, preferred_element_type=F32) * ATTN_SCALE
            s = s + (bias_first if first else bias_prev)
            m_blk = jnp.max(s, axis=-1, keepdims=True)
            p = jnp.exp(s - m_blk)
            l_blk = jnp.sum(p, axis=-1, keepdims=True)
            o_blk = jnp.dot(p.astype(BF16), vb, preferred_element_type=F32)
            qrows = rows(qstart, QB)
            if g == 0:
                m_s[qrows, :] = jnp.broadcast_to(m_blk, (QB, HEAD_DIM))
                l_s[qrows, :] = jnp.broadcast_to(l_blk, (QB, HEAD_DIM))
                acc_s[qrows, :] = o_blk
            else:
                m_old = m_s[qrows, :]
                m_new = jnp.maximum(m_old, m_blk)
                a_old = jnp.exp(m_old - m_new)
                a_blk = jnp.exp(m_blk - m_new)
                m_s[qrows, :] = m_new
                l_s[qrows, :] = l_s[qrows, :] * a_old + l_blk * a_blk
                acc_s[qrows, :] = acc_s[qrows, :] * a_old + o_blk * a_blk

        if dil == 1:
            block(0, 0, True)

            def body(n, carry, block=block):
                qstart = pl.multiple_of(n * QB, QB)
                block(qstart, pl.multiple_of(qstart - QB, QB), False)
                return carry

            lax.fori_loop(1, n_blk, body, 0)
        else:
            def body(r, carry, block=block, n_blk=n_blk, dil=dil):
                block(r, r, True)
                for n in range(1, n_blk):
                    block(r + n * QB * dil, r + (n - 1) * QB * dil, False)
                return carry

            lax.fori_loop(0, dil, body, 0)

    ya_ref[...] = (acc_s[...] / l_s[...]).astype(ya_ref.dtype)
    for g in range(N_DIL):
        kv_copy(g, 0).wait()
        kv_copy(g, 1).wait()


def _attn_prompt_call(z, gq, gk, slopes, *, batch, seq):
    t = batch * seq
    kv_shapes = [jax.ShapeDtypeStruct((batch, min(window, seq), 2, A_HEADS, HEAD_DIM), F32)
                 for window, _ in DILATIONS]

    def col(g, c):
        return lambda b, h: (b, g * 3 * A_HEADS + c * A_HEADS + h)

    in_specs = []
    for g in range(N_DIL):
        for c in range(3):
            in_specs.append(pl.BlockSpec((seq, HEAD_DIM), col(g, c)))
    gspec = pl.BlockSpec((N_DIL, None, 1, HEAD_DIM), lambda b, h: (0, h, 0, 0))
    in_specs += [gspec, gspec, pl.BlockSpec((None, 1, HEAD_DIM), lambda b, h: (h, 0, 0))]
    outs = pl.pallas_call(
        functools.partial(_attn_prompt_kernel, seq=seq),
        grid=(batch, A_HEADS),
        in_specs=in_specs,
        out_specs=[pl.BlockSpec((seq, HEAD_DIM), lambda b, h: (b, h))] + [pl.BlockSpec(memory_space=pl.ANY)] * 3,
        out_shape=[jax.ShapeDtypeStruct((t, A_WIDTH), BF16)] + kv_shapes,
        scratch_shapes=[pltpu.VMEM((seq, HEAD_DIM), F32)] * 7 + [pltpu.SemaphoreType.DMA((2 * N_DIL,))],
        compiler_params=_params(("arbitrary", "arbitrary")),
        name="attn_prompt",
    )(*([z] * 9), gq.reshape(N_DIL, A_HEADS, 1, HEAD_DIM), gk.reshape(N_DIL, A_HEADS, 1, HEAD_DIM), slopes)
    return outs[0], outs[1:]


def _round_bf16(x):
    return x.astype(BF16).astype(F32)


def _attn_sample_kernel(z_ref, c0_ref, c1_ref, c2_ref, gq_ref, gk_ref, slope_ref, ya_ref, new_ref):
    caches = (c0_ref, c1_ref, c2_ref)
    slope = slope_ref[...][:, :1]
    nk = DILATIONS[0][0] // DILATIONS[0][1]
    steps = (nk - lax.broadcasted_iota(I32, (nk, 1, 1), 0)).astype(F32)
    m_run = l_run = acc = None
    for g, (window, dil) in enumerate(DILATIONS):
        base = g * 3 * A_HEADS
        q = _rms_rows(z_ref[base:base + A_HEADS, :], gq_ref[g])
        k = _rms_rows(z_ref[base + A_HEADS:base + 2 * A_HEADS, :], gk_ref[g])
        v = z_ref[base + 2 * A_HEADS:base + 3 * A_HEADS, :]
        new_ref[g, 0] = k
        new_ref[g, 1] = v
        qr = _round_bf16(q)
        ck = _round_bf16(caches[g][:, 0])
        cv = _round_bf16(caches[g][:, 1])
        s_c = jnp.sum(ck * qr[None], axis=-1, keepdims=True) * ATTN_SCALE
        s_c = s_c - (slope * float(dil))[None] * steps
        s_n = jnp.sum(_round_bf16(k) * qr, axis=-1, keepdims=True) * ATTN_SCALE
        m_g = jnp.maximum(jnp.max(s_c, axis=0), s_n)
        p_c = jnp.exp(s_c - m_g[None])
        p_n = jnp.exp(s_n - m_g)
        l_g = jnp.sum(p_c, axis=0) + p_n
        o_g = jnp.sum(_round_bf16(p_c) * cv, axis=0) + _round_bf16(p_n) * _round_bf16(v)
        if g == 0:
            m_run, l_run, acc = m_g, l_g, o_g
        else:
            m_new = jnp.maximum(m_run, m_g)
            a_old = jnp.exp(m_run - m_new)
            a_g = jnp.exp(m_g - m_new)
            l_run = l_run * a_old + l_g * a_g
            acc = acc * a_old + o_g * a_g
            m_run = m_new
    ya_ref[...] = acc / l_run


def _attn_sample_call(z_s, caches, gq, gk, slopes8, *, dec_batch):
    n_colblk = z_s.shape[1] // HEAD_DIM
    z3 = z_s.reshape(z_s.shape[0], n_colblk, HEAD_DIM)
    in_specs = [pl.BlockSpec((None, n_colblk, HEAD_DIM), lambda b: (b, 0, 0))]
    args = [z3]
    for g, (window, dil) in enumerate(DILATIONS):
        nk = window // dil
        c = caches[g].reshape(dec_batch, nk, dil, 2, A_HEADS, HEAD_DIM)
        in_specs.append(pl.BlockSpec((None, nk, None, 2, A_HEADS, HEAD_DIM), lambda b: (b, 0, 0, 0, 0, 0)))
        args.append(c)
    full = lambda shape: pl.BlockSpec(shape, lambda b: (0,) * len(shape))
    in_specs += [full((N_DIL, A_HEADS, HEAD_DIM)), full((N_DIL, A_HEADS, HEAD_DIM)), full((A_HEADS, HEAD_DIM))]
    return pl.pallas_call(
        _attn_sample_kernel,
        grid=(dec_batch,),
        in_specs=in_specs,
        out_specs=[pl.BlockSpec((None, A_HEADS, HEAD_DIM), lambda b: (b, 0, 0)),
                   pl.BlockSpec((None, N_DIL, 2, A_HEADS, HEAD_DIM), lambda b: (b, 0, 0, 0, 0))],
        out_shape=[jax.ShapeDtypeStruct((dec_batch, A_HEADS, HEAD_DIM), F32),
                   jax.ShapeDtypeStruct((dec_batch, N_DIL, 2, A_HEADS, HEAD_DIM), F32)],
        compiler_params=_params(("arbitrary",)),
        name="attn_sample",
    )(*args, gq, gk, slopes8)


def _cache_roll_kernel(cur_ref, nxt_ref, new_ref, o_ref):
    last = pl.program_id(1) == pl.num_programs(1) - 1
    rows = cur_ref.shape[0]
    o_ref[0:rows - 1] = cur_ref[1:rows]
    o_ref[rows - 1:rows] = jnp.where(last, new_ref[...], nxt_ref[...])


def _cache_roll_call(cache, new, *, rows=256):
    b, w = cache.shape[:2]
    rows = min(rows, w)
    tail = cache.shape[2:]
    n_chunks = w // rows
    zeros = (0,) * len(tail)
    return pl.pallas_call(
        _cache_roll_kernel,
        grid=(b, n_chunks),
        in_specs=[pl.BlockSpec((None, rows) + tail, lambda i, j: (i, j) + zeros),
                  pl.BlockSpec((None, 1) + tail, lambda i, j: (i, jnp.minimum((j + 1) * rows, w - 1)) + zeros),
                  pl.BlockSpec((None, 1) + tail, lambda i, j: (i, 0) + zeros)],
        out_specs=pl.BlockSpec((None, rows) + tail, lambda i, j: (i, j) + zeros),
        out_shape=jax.ShapeDtypeStruct(cache.shape, cache.dtype),
        compiler_params=_params(("arbitrary", "arbitrary")),
        name="cache_roll",
    )(cache, cache, new)


def _softplus(x):
    return jnp.maximum(x, 0.0) + jnp.log1p(jnp.exp(-jnp.abs(x)))


def _lru_coeffs(xc, wa, ba, wi, bi, lam):
    xcb = xc.astype(BF16)
    r = jax.nn.sigmoid(jnp.dot(xcb, wa.astype(BF16), preferred_element_type=F32) + ba)
    i = jax.nn.sigmoid(jnp.dot(xcb, wi.astype(BF16), preferred_element_type=F32) + bi)
    log_a = -LRU_C * r * _softplus(-lam)
    a = jnp.exp(log_a)
    t = jnp.tanh(log_a)
    one_minus_a2 = -2.0 * t / (1.0 - t)
    return a, jnp.sqrt(one_minus_a2) * (i * xc)


def _lru_prompt_kernel(x_ref, gate_ref, cw_ref, cb_ref, wa_ref, ba_ref, wi_ref, bi_ref, lam_ref,
                       y_ref, hl_ref, pad_a, pad_b, *, seq):
    blk = x_ref.shape[1]
    off = SUBLANES
    x = x_ref[...]
    pad_a[0:off, :] = jnp.zeros((off, blk), F32)
    pad_a[off:off + seq, :] = x
    xc = pad_a[off - 3:off - 3 + seq, :] * cw_ref[0:1, :]
    xc = xc + pad_a[off - 2:off - 2 + seq, :] * cw_ref[1:2, :]
    xc = xc + pad_a[off - 1:off - 1 + seq, :] * cw_ref[2:3, :]
    xc = xc + x * cw_ref[3:4, :] + cb_ref[...]
    a, b = _lru_coeffs(xc, wa_ref[...], ba_ref[...], wi_ref[...], bi_ref[...], lam_ref[...])

    tpos = lax.broadcasted_iota(I32, (seq, blk), 0) & (SUBLANES - 1)
    pad_b[0:off, :] = jnp.zeros((off, blk), F32)
    for o in (1, 2, 4):
        pad_a[off:off + seq, :] = a
        pad_b[off:off + seq, :] = b
        keep = tpos >= o
        a_prev = jnp.where(keep, pad_a[off - o:off - o + seq, :], 1.0)
        b_prev = jnp.where(keep, pad_b[off - o:off - o + seq, :], 0.0)
        b = a * b_prev + b
        a = a * a_prev
    pad_a[off:off + seq, :] = a
    pad_b[off:off + seq, :] = b

    def tile(i, h_prev):
        r0 = pl.multiple_of(off + i * SUBLANES, SUBLANES)
        h = pad_b[pl.ds(r0, SUBLANES), :] + pad_a[pl.ds(r0, SUBLANES), :] * h_prev
        pad_b[pl.ds(r0, SUBLANES), :] = h
        return jnp.broadcast_to(h[SUBLANES - 1:SUBLANES, :], (SUBLANES, blk))

    lax.fori_loop(0, seq // SUBLANES, tile, jnp.zeros((SUBLANES, blk), F32))
    h = pad_b[off:off + seq, :]
    hl_ref[...] = h[seq - SUBLANES:, :]
    y_ref[...] = (h * jax.nn.gelu(gate_ref[...])).astype(y_ref.dtype)


def _lru_prompt_call(z, cw, cb, wa, ba, wi, bi, lam, *, batch, seq):
    t = batch * seq
    nb, blk, _ = wa.shape
    width = nb * blk
    x0 = A_QKV // blk
    g0 = (A_QKV + width) // blk
    vec = lambda v: v.reshape(1, width)
    vspec = pl.BlockSpec((1, blk), lambda b, n: (0, n))
    return pl.pallas_call(
        functools.partial(_lru_prompt_kernel, seq=seq),
        grid=(batch, nb),
        in_specs=[pl.BlockSpec((seq, blk), lambda b, n: (b, x0 + n)),
                  pl.BlockSpec((seq, blk), lambda b, n: (b, g0 + n)),
                  pl.BlockSpec((B_CONV, blk), lambda b, n: (0, n)), vspec,
                  pl.BlockSpec((None, blk, blk), lambda b, n: (n, 0, 0)), vspec,
                  pl.BlockSpec((None, blk, blk), lambda b, n: (n, 0, 0)), vspec, vspec],
        out_specs=[pl.BlockSpec((seq, blk), lambda b, n: (b, n)),
                   pl.BlockSpec((None, SUBLANES, blk), lambda b, n: (b, 0, n))],
        out_shape=[jax.ShapeDtypeStruct((t, width), BF16),
                   jax.ShapeDtypeStruct((batch, SUBLANES, width), F32)],
        scratch_shapes=[pltpu.VMEM((seq + SUBLANES, blk), F32)] * 2,
        compiler_params=_params(("arbitrary", "arbitrary")),
        name="rglru_prompt",
    )(z, z, cw, vec(cb), wa, vec(ba), wi, vec(bi), vec(lam))


def _lru_sample_kernel(x_ref, gate_ref, buf_ref, h0_ref, cw_ref, cb_ref, wa_ref, ba_ref, wi_ref, bi_ref,
                       lam_ref, y_ref, h_ref):
    nb, blk, _ = wa_ref.shape
    for n in range(nb):
        cs = slice(n * blk, (n + 1) * blk)
        x = x_ref[:, cs]
        xc = buf_ref[0, :, cs] * cw_ref[0:1, cs]
        xc = xc + buf_ref[1, :, cs] * cw_ref[1:2, cs]
        xc = xc + buf_ref[2, :, cs] * cw_ref[2:3, cs]
        xc = xc + x * cw_ref[3:4, cs] + cb_ref[:, cs]
        a, b = _lru_coeffs(xc, wa_ref[n], ba_ref[:, cs], wi_ref[n], bi_ref[:, cs], lam_ref[:, cs])
        h = b + a * h0_ref[:, cs]
        h_ref[:, cs] = h
        y_ref[:, cs] = (h * jax.nn.gelu(gate_ref[:, cs])).astype(y_ref.dtype)


def _lru_sample_call(x, gate, buf, h0, cw, cb, wa, ba, wi, bi, lam):
    rows, width = x.shape
    vec = lambda v: v.reshape(1, width)
    return pl.pallas_call(
        _lru_sample_kernel,
        out_shape=[jax.ShapeDtypeStruct((rows, width), BF16), jax.ShapeDtypeStruct((rows, width), F32)],
        compiler_params=pltpu.CompilerParams(vmem_limit_bytes=VMEM_LIMIT),
        name="rglru_sample",
    )(x, gate, buf, h0, cw, vec(cb), wa, vec(ba), wi, vec(bi), vec(lam))


def _proj_gconv_kernel(xp_ref, xs_ref, w_hbm, cw_ref, buf_ref, m_ref, ut_ref, ms_ref, us_ref,
                       w_f32, w_cat, pad, sem, *, li, tiles_per_seq):
    j = pl.program_id(0)
    i = pl.program_id(1)
    tm = xp_ref.shape[0]
    blk = cw_ref.shape[1]
    n_piece = 3
    d_model = w_hbm.shape[2] // n_piece

    def w_copy(k, jj):
        col = pl.multiple_of(k * d_model + jj * blk, blk)
        return pltpu.make_async_copy(w_hbm.at[li, :, pl.ds(col, blk)], w_f32.at[k], sem.at[k])

    @pl.when((j == 0) & (i == 0))
    def _():
        for k in range(n_piece):
            w_copy(k, 0).start()

    @pl.when(i == 0)
    def _():
        for k in range(n_piece):
            w_copy(k, j).wait()
            w_cat[:, k * blk:(k + 1) * blk] = w_f32[k].astype(BF16)

        @pl.when(j + 1 < pl.num_programs(0))
        def _():
            for k in range(n_piece):
                w_copy(k, j + 1).start()

        acc_s = jnp.dot(xs_ref[...], w_cat[...], preferred_element_type=F32)
        u_s = acc_s[:, blk:2 * blk] * acc_s[:, 2 * blk:]
        uc_s = buf_ref[0] * cw_ref[0:1, :] + buf_ref[1] * cw_ref[1:2, :]
        uc_s = uc_s + u_s * cw_ref[2:3, :]
        ms_ref[...] = (acc_s[:, :blk] * uc_s).astype(ms_ref.dtype)
        us_ref[...] = u_s

    @pl.when(i % tiles_per_seq == 0)
    def _():
        pad[0:SUBLANES, :] = jnp.zeros((SUBLANES, blk), F32)

    acc = jnp.dot(xp_ref[...], w_cat[...], preferred_element_type=F32)
    u = acc[:, blk:2 * blk] * acc[:, 2 * blk:]
    off = SUBLANES
    pad[off:off + tm, :] = u
    uc = pad[off - 2:off - 2 + tm, :] * cw_ref[0:1, :]
    uc = uc + pad[off - 1:off - 1 + tm, :] * cw_ref[1:2, :]
    uc = uc + u * cw_ref[2:3, :]
    m_ref[...] = (acc[:, :blk] * uc).astype(m_ref.dtype)
    tail = u[tm - SUBLANES:, :]
    ut_ref[...] = tail
    pad[0:SUBLANES, :] = tail


def _proj_gconv_call(xp, xs, w, li, cw, buf_s, *, batch, seq, tm=1024, blk=256):
    t, k = xp.shape
    rs = xs.shape[0]
    d = w.shape[2] // 3
    nblk = d // blk
    tiles_per_seq = seq // tm
    return pl.pallas_call(
        functools.partial(_proj_gconv_kernel, li=li, tiles_per_seq=tiles_per_seq),
        grid=(nblk, t // tm),
        in_specs=[pl.BlockSpec((tm, k), lambda j, i: (i, 0)),
                  pl.BlockSpec((rs, k), lambda j, i: (0, 0)),
                  pl.BlockSpec(memory_space=pl.ANY),
                  pl.BlockSpec((C_CONV, blk), lambda j, i: (0, j)),
                  pl.BlockSpec((C_CONV - 1, rs, blk), lambda j, i: (0, 0, j))],
        out_specs=[pl.BlockSpec((tm, blk), lambda j, i: (i, j)),
                   pl.BlockSpec((None, SUBLANES, blk), lambda j, i: (i // tiles_per_seq, 0, j)),
                   pl.BlockSpec((rs, blk), lambda j, i: (0, j)),
                   pl.BlockSpec((rs, blk), lambda j, i: (0, j))],
        out_shape=[jax.ShapeDtypeStruct((t, d), BF16), jax.ShapeDtypeStruct((batch, SUBLANES, d), F32),
                   jax.ShapeDtypeStruct((rs, d), BF16), jax.ShapeDtypeStruct((rs, d), F32)],
        scratch_shapes=[pltpu.VMEM((3, k, blk), F32), pltpu.VMEM((k, 3 * blk), BF16),
                        pltpu.VMEM((tm + SUBLANES, blk), F32), pltpu.SemaphoreType.DMA((3,))],
        compiler_params=_params(("arbitrary", "arbitrary")),
        name="proj_gconv",
    )(xp, xs, w, cw, buf_s)


MOE_TM = 512
MOE_HALF = MOE_TM // 2


def _bucket_row(route_ref, start_ref, k, t):
    return start_ref[0, route_ref[k, t]] + route_ref[2 + k, t]


def _bucket_kernel(rp_ref, rs_ref, start_ref, ztile_ref, hp_ref, hs_ref, xs_ref, zero_buf, sem, zsem):
    step = pl.program_id(0)
    tr = hp_ref.shape[0]
    n_s = rs_ref.shape[1]

    def copy_p(t, k):
        row = _bucket_row(rp_ref, start_ref, k, t)
        return pltpu.make_async_copy(hp_ref.at[pl.ds(t, 1)], xs_ref.at[pl.ds(row, 1)], sem)

    def copy_s(t, k):
        row = _bucket_row(rs_ref, start_ref, k, t)
        return pltpu.make_async_copy(hs_ref.at[pl.ds(t, 1)], xs_ref.at[pl.ds(row, 1)], sem)

    def copy_z(j):
        start = pl.multiple_of(ztile_ref[0, j] * MOE_HALF, MOE_HALF)
        return pltpu.make_async_copy(zero_buf, xs_ref.at[pl.ds(start, MOE_HALF)], zsem)

    @pl.when(step == 0)
    def _():
        zero_buf[...] = jnp.zeros_like(zero_buf)
        for j in range(ztile_ref.shape[1]):
            pl.when(ztile_ref[1, j] == 1)(lambda j=j: copy_z(j).start())
        for j in range(ztile_ref.shape[1]):
            pl.when(ztile_ref[1, j] == 1)(lambda j=j: copy_z(j).wait())
        for t in range(n_s):
            copy_s(t, 0).start()
            copy_s(t, 1).start()
        for t in range(n_s):
            copy_s(t, 0).wait()
            copy_s(t, 1).wait()

    def start_p(i, c):
        for j in range(SUBLANES):
            copy_p(i * SUBLANES + j, 0).start(priority=0)
            copy_p(i * SUBLANES + j, 1).start(priority=1)
        return c

    def wait_p(t, c):
        copy_p(t, 0).wait()
        copy_p(t, 1).wait()
        return c

    lax.fori_loop(0, tr // SUBLANES, start_p, 0)
    lax.fori_loop(0, tr, wait_p, 0)


def _bucket_call(rp, rs, row_start, ztiles, hp, hs, p_rows, *, tr=256):
    tp, dh = hp.shape
    n_s = rs.shape[1]
    smem = lambda shape, imap: pl.BlockSpec(shape, imap, memory_space=pltpu.SMEM)
    return pl.pallas_call(
        _bucket_kernel,
        grid=(tp // tr,),
        in_specs=[smem((SUBLANES, tr), lambda i: (0, i)), smem((SUBLANES, n_s), lambda i: (0, 0)),
                  smem(row_start.shape, lambda i: (0, 0)), smem(ztiles.shape, lambda i: (0, 0)),
                  pl.BlockSpec((tr, dh), lambda i: (i, 0)),
                  pl.BlockSpec((n_s, dh), lambda i: (0, 0))],
        out_specs=pl.BlockSpec(memory_space=pl.ANY),
        out_shape=jax.ShapeDtypeStruct((p_rows, dh), hp.dtype),
        scratch_shapes=[pltpu.VMEM((MOE_HALF, dh), hp.dtype), pltpu.SemaphoreType.DMA, pltpu.SemaphoreType.DMA],
        compiler_params=_params(("arbitrary",)),
        name="moe_bucket",
    )(rp, rs, row_start, ztiles, hp, hs)


(S_XTILE, S_OTILE, S_EXPERT, S_WCHUNK, S_OCHUNK, S_FIRST, S_VALID, S_FULL,
 S_HAS_NEXT, S_NEXT_EXPERT, S_NEXT_CHUNK) = range(11)


def _grouped_kernel(s_ref, x_ref, *refs, n_w, layer, tn, swiglu):
    w_hbm = refs[:n_w]
    o_ref = refs[n_w]
    w_f32 = refs[n_w + 1:2 * n_w + 1]
    w_b16 = refs[2 * n_w + 1]
    sem = refs[2 * n_w + 2]
    it = pl.program_id(0)

    def w_copy(k, expert, chunk):
        col = pl.multiple_of(chunk * tn, LANES)
        return pltpu.make_async_copy(w_hbm[k].at[layer, expert, :, pl.ds(col, tn)], w_f32[k], sem.at[k])

    @pl.when(it == 0)
    def _():
        for k in range(n_w):
            w_copy(k, s_ref[S_EXPERT, 0], s_ref[S_WCHUNK, 0]).start()

    @pl.when(s_ref[S_FIRST, it] == 1)
    def _():
        for k in range(n_w):
            w_copy(k, s_ref[S_EXPERT, it], s_ref[S_WCHUNK, it]).wait()
            w_b16[:, k * tn:(k + 1) * tn] = w_f32[k][...].astype(BF16)

        @pl.when(s_ref[S_HAS_NEXT, it] == 1)
        def _():
            for k in range(n_w):
                w_copy(k, s_ref[S_NEXT_EXPERT, it], s_ref[S_NEXT_CHUNK, it]).start()

    def compute(rows):
        acc = jnp.dot(x_ref[0:rows, :].astype(BF16), w_b16[...], preferred_element_type=F32)
        if swiglu:
            gate = acc[:, :tn]
            acc = gate * jax.nn.sigmoid(gate) * acc[:, tn:]
        o_ref[0:rows, :] = acc.astype(o_ref.dtype)

    valid = s_ref[S_VALID, it] == 1
    full = s_ref[S_FULL, it] == 1

    @pl.when(valid & full)
    def _():
        compute(MOE_TM)

    @pl.when(valid & jnp.logical_not(full))
    def _():
        compute(MOE_HALF)
        o_ref[MOE_HALF:, :] = jnp.zeros((MOE_TM - MOE_HALF, tn), o_ref.dtype)

    @pl.when(jnp.logical_not(valid))
    def _():
        o_ref[...] = jnp.zeros_like(o_ref)


def _grouped_call(sched, x, weights, layer, *, tn, out_dtype, swiglu, name):
    p, k = x.shape
    n = weights[0].shape[3]
    n_w = len(weights)
    n_items = sched.shape[1]
    grid_spec = pltpu.PrefetchScalarGridSpec(
        num_scalar_prefetch=1,
        grid=(n_items,),
        in_specs=[pl.BlockSpec((MOE_TM, k), lambda it, s: (s[S_XTILE, it], 0))]
                 + [pl.BlockSpec(memory_space=pl.ANY)] * n_w,
        out_specs=pl.BlockSpec((MOE_TM, tn), lambda it, s: (s[S_OTILE, it], s[S_OCHUNK, it])),
        scratch_shapes=[pltpu.VMEM((k, tn), F32)] * n_w + [pltpu.VMEM((k, n_w * tn), BF16),
                                                           pltpu.SemaphoreType.DMA((n_w,))],
    )
    return pl.pallas_call(
        functools.partial(_grouped_kernel, n_w=n_w, layer=layer, tn=tn, swiglu=swiglu),
        grid_spec=grid_spec,
        out_shape=jax.ShapeDtypeStruct((p, n), out_dtype),
        compiler_params=_params(("arbitrary",)),
        name=name,
    )(sched, x, *weights)


def _combine_kernel(route_ref, next_route_ref, start_ref, res_ref, gate_ref, w_ref, y_ref, o_ref, buf, sem, *, tr):
    step = pl.program_id(0)
    slot = step % 2

    def copy(table, s, t, k):
        row = _bucket_row(table, start_ref, k, t)
        return pltpu.make_async_copy(y_ref.at[pl.ds(row, 1)], buf.at[s, k, pl.ds(t, 1)], sem.at[s])

    def issue(table, s):
        def body(i, c):
            for j in range(SUBLANES):
                copy(table, s, i * SUBLANES + j, 0).start(priority=0)
                copy(table, s, i * SUBLANES + j, 1).start(priority=1)
            return c
        lax.fori_loop(0, tr // SUBLANES, body, 0)

    @pl.when(step == 0)
    def _():
        issue(route_ref, 0)

    @pl.when(step + 1 < pl.num_programs(0))
    def _():
        issue(next_route_ref, 1 - slot)

    def wait(t, c):
        copy(route_ref, slot, t, 0).wait()
        copy(route_ref, slot, t, 1).wait()
        return c

    lax.fori_loop(0, tr, wait, 0)
    w = jnp.transpose(w_ref[...])
    y = w[:, 0:1] * buf[slot, 0] + w[:, 1:2] * buf[slot, 1]
    o_ref[...] = res_ref[...] + gate_ref[...] * y


def _combine_call(route, row_start, res, gate, w, y, *, per_row, tr, rows_per_mod=None):
    rows, d = res.shape
    gspec = _mod_specs(per_row, tr, d, rows_per_mod)
    smem = lambda shape, imap: pl.BlockSpec(shape, imap, memory_space=pltpu.SMEM)
    n_tiles = rows // tr
    return pl.pallas_call(
        functools.partial(_combine_kernel, tr=tr),
        grid=(n_tiles,),
        in_specs=[smem((SUBLANES, tr), lambda i: (0, i)),
                  smem((SUBLANES, tr), lambda i: (0, jnp.minimum(i + 1, n_tiles - 1))),
                  smem(row_start.shape, lambda i: (0, 0)),
                  pl.BlockSpec((tr, d), lambda i: (i, 0)), gspec,
                  pl.BlockSpec((SUBLANES, tr), lambda i: (0, i)),
                  pl.BlockSpec(memory_space=pl.ANY)],
        out_specs=pl.BlockSpec((tr, d), lambda i: (i, 0)),
        out_shape=jax.ShapeDtypeStruct((rows, d), F32),
        scratch_shapes=[pltpu.VMEM((2, 2, tr, d), F32), pltpu.SemaphoreType.DMA((2,))],
        compiler_params=_params(("arbitrary",)),
        name="moe_combine",
    )(route, route, row_start, res, gate, w, y)


def _moe_schedule(counts, max_tiles, n_chunks):
    e_ids = jnp.arange(N_EXPERTS, dtype=I32)
    tiles_per_e = (counts + MOE_TM - 1) // MOE_TM
    tile_end = jnp.cumsum(tiles_per_e)
    tile_start = tile_end - tiles_per_e
    used = tile_end[-1]
    n_valid = used * n_chunks
    nonempty = tiles_per_e > 0
    later = (e_ids[None, :] > e_ids[:, None]) & nonempty[None, :]
    next_e = jnp.min(jnp.where(later, e_ids[None, :], N_EXPERTS - 1), axis=1)
    last_e = jnp.max(jnp.where(nonempty, e_ids, 0))
    it = jnp.arange(n_chunks * max_tiles, dtype=I32)
    ex = jnp.minimum(jnp.sum((it[:, None] >= (tile_end * n_chunks)[None, :]).astype(I32), axis=1), N_EXPERTS - 1)
    onehot = (ex[:, None] == e_ids[None, :]).astype(I32)
    look = lambda v: jnp.sum(onehot * v.astype(I32)[None, :], axis=1)
    t_e = jnp.maximum(look(tiles_per_e), 1)
    t0 = look(tile_start)
    local = it - t0 * n_chunks
    chunk = local // t_e
    idx = local % t_e
    valid = it < n_valid
    first = (idx == 0) & valid
    full = look(counts) - idx * MOE_TM > MOE_HALF
    has_next = first & (it + t_e < n_valid)
    last_chunk = chunk == n_chunks - 1
    next_expert = jnp.where(last_chunk, look(next_e), ex)
    next_chunk = jnp.where(last_chunk, 0, chunk + 1)
    q = it - n_valid
    xtile = jnp.where(valid, t0 + idx, used - 1)
    otile = jnp.where(valid, t0 + idx, used + q // n_chunks)
    ochunk = jnp.where(valid, chunk, q % n_chunks)
    return jnp.stack([xtile, otile, jnp.where(valid, ex, last_e), jnp.where(valid, chunk, n_chunks - 1), ochunk,
                      first, valid, full, has_next, next_expert, next_chunk]).astype(I32)


def _moe(hp, hs, rp, rs, counts, w_gate, w_up, w_down, layer):
    n_assign = 2 * (hp.shape[0] + hs.shape[0])
    max_tiles = (n_assign + N_EXPERTS * (MOE_TM - 1)) // MOE_TM
    tiles_per_e = (counts + MOE_TM - 1) // MOE_TM
    tile_end = jnp.cumsum(tiles_per_e)
    row_start = (tile_end - tiles_per_e) * MOE_TM
    half0 = row_start // MOE_HALF
    n_half = 2 * tiles_per_e
    last_real = (counts + MOE_HALF - 1) // MOE_HALF - 1
    unused = 2 * tile_end[-1] + jnp.arange(2 * N_EXPERTS, dtype=I32)
    zhalf = jnp.concatenate([half0 + last_real, half0 + n_half - 1, unused])
    zflag = jnp.concatenate([counts % MOE_HALF != 0, last_real < n_half - 1, unused < 2 * max_tiles])
    ztiles = jnp.stack([jnp.clip(zhalf, 0, 2 * max_tiles - 1), zflag.astype(I32)]).astype(I32)
    gu_chunks, dn_chunks = 3, 2
    row_start = row_start.astype(I32).reshape(1, N_EXPERTS)
    xs = _bucket_call(rp, rs, row_start, ztiles, hp, hs, max_tiles * MOE_TM)
    he = _grouped_call(_moe_schedule(counts, max_tiles, gu_chunks), xs, [w_gate, w_up], layer,
                       tn=w_gate.shape[3] // gu_chunks, out_dtype=BF16, swiglu=True, name="moe_gate_up")
    y = _grouped_call(_moe_schedule(counts, max_tiles, dn_chunks), he, [w_down], layer,
                      tn=w_down.shape[3] // dn_chunks, out_dtype=F32, swiglu=False, name="moe_down")
    return y, row_start


def kernel(x_prompt, x_sample, cache_a_kv0, cache_a_kv1, cache_a_kv2, state_b_h, state_b_conv, state_c_conv,
           c_prompt, c_sample, w_mod, b_mod, norm_g, w_in0, qk_g, b_conv_w, b_conv_b, b_wa, b_ba, b_wi, b_bi,
           b_lambda, w_out0, w_in1, c_conv_w, w_out1, w_router, b_router, w_gate, w_up, w_down):
    batch, seq, d = x_prompt.shape
    dec_batch = x_sample.shape[0]
    depth = w_mod.shape[0]
    tp = batch * seq
    width = b_lambda.shape[1]
    pad_s = SAMPLE_ROWS - dec_batch
    caches = (cache_a_kv0, cache_a_kv1, cache_a_kv2)

    def pad_rows(v):
        return jnp.pad(v, ((0, pad_s),) + ((0, 0),) * (v.ndim - 1))

    res_p = x_prompt.reshape(tp, d)
    res_s = pad_rows(x_sample.reshape(dec_batch, d))
    c_all = jnp.concatenate([c_prompt, jnp.zeros((SUBLANES - batch, d), F32), pad_rows(c_sample)], axis=0)
    mod = _mod_call(c_all, w_mod, b_mod)
    slopes = 2.0 ** (-8.0 * np.arange(1, A_HEADS + 1) / A_HEADS)
    slopes8 = jnp.asarray(np.repeat(slopes[:, None], HEAD_DIM, axis=1), dtype=F32)
    w_router_t = w_router.T

    new_kv_p = [[], [], []]
    new_kv_s = [[], [], []]
    bh_p, bconv_p, cconv_p, bh_s, bconv_s, cconv_s = [], [], [], [], [], []

    for layer in range(depth):
        li = layer // 2
        m = mod[layer]
        chunks_p = [m[:batch, k * d:(k + 1) * d].reshape(batch, 1, d) for k in range(6)]
        chunks_s = [m[SUBLANES:, k * d:(k + 1) * d] for k in range(6)]
        sh1p, sc1p, g1p, sh2p, sc2p, g2p = chunks_p
        sh1s, sc1s, g1s, sh2s, sc2s, g2s = chunks_s

        h_p = _norm_call(res_p, norm_g[layer, 0], sc1p, sh1p, per_row=False, tr=256, rows_per_mod=seq)
        h_s = _norm_call(res_s, norm_g[layer, 0], sc1s, sh1s, per_row=True, tr=SAMPLE_ROWS)

        if layer % 2 == 0:
            z_p, z_s = _dense_wide_call([h_p], h_s, w_in0, li, vmem=WIDE_VMEM_LIMIT)
            ya_p, kv_p = _attn_prompt_call(z_p, qk_g[li, 0], qk_g[li, 1], slopes8.reshape(A_HEADS, 1, HEAD_DIM),
                                           batch=batch, seq=seq)
            for g in range(N_DIL):
                new_kv_p[g].append(kv_p[g])
            yb_p, hl_p = _lru_prompt_call(z_p, b_conv_w[li], b_conv_b[li], b_wa[li], b_ba[li], b_wi[li],
                                          b_bi[li], b_lambda[li], batch=batch, seq=seq)
            bh_p.append(hl_p[:, SUBLANES - 1])
            bconv_p.append(z_p.reshape(batch, seq, -1)[:, seq - (B_CONV - 1):, A_QKV:A_QKV + width])

            ya_s, kvnew_s = _attn_sample_call(z_s, [c[li] for c in caches], qk_g[li, 0], qk_g[li, 1], slopes8,
                                              dec_batch=dec_batch)
            for g in range(N_DIL):
                new_kv_s[g].append(_cache_roll_call(caches[g][li], kvnew_s[:, g][:, None]))
            xb_s = z_s[:, A_QKV:A_QKV + width]
            gb_s = z_s[:, A_QKV + width:]
            buf = jnp.transpose(pad_rows(state_b_conv[li]), (1, 0, 2))
            yb_s, hnew_s = _lru_sample_call(xb_s, gb_s, buf, pad_rows(state_b_h[li]), b_conv_w[li], b_conv_b[li],
                                            b_wa[li], b_ba[li], b_wi[li], b_bi[li], b_lambda[li])
            bh_s.append(hnew_s[:dec_batch])
            bconv_s.append(jnp.concatenate([state_b_conv[li][:, 1:], xb_s[:dec_batch, None]], axis=1))

            mix_p = [ya_p, yb_p]
            mix_s = jnp.concatenate([pad_rows(ya_s.reshape(dec_batch, A_WIDTH)).astype(BF16), yb_s], axis=1)
            w_out = w_out0
        else:
            buf = jnp.transpose(pad_rows(state_c_conv[li]), (1, 0, 2))
            m_p, ut_p, mix_s, u_s = _proj_gconv_call(h_p, h_s, w_in1, li, c_conv_w[li], buf, batch=batch, seq=seq)
            mix_p = [m_p]
            cconv_p.append(ut_p[:, SUBLANES - (C_CONV - 1):])
            cconv_s.append(jnp.concatenate([state_c_conv[li][:, 1:], u_s[:dec_batch, None]], axis=1))
            w_out = w_out1

        res_p, res_s = _dense_wide_call(mix_p, mix_s, w_out, li, res=(res_p, g1p, res_s, g1s), rows_per_mod=seq,
                                        tm=512)

        hp, ep, wp, cnt = _norm_route_call(res_p, norm_g[layer, 1], sc2p, sh2p, w_router_t, b_router,
                                           jnp.zeros((N_EXPERTS, LANES), F32),
                                           per_row=False, tr=256, rows_per_mod=seq)
        hs, es, ws, cnt = _norm_route_call(res_s, norm_g[layer, 1], sc2s, sh2s, w_router_t, b_router, cnt,
                                           per_row=True, tr=SAMPLE_ROWS)
        y, row_start = _moe(hp, hs, ep, es, cnt[:, 0].astype(I32), w_gate, w_up, w_down, layer)
        res_p = _combine_call(ep, row_start, res_p, g2p, wp, y, per_row=False, tr=256, rows_per_mod=seq)
        res_s = _combine_call(es, row_start, res_s, g2s, ws, y, per_row=True, tr=SAMPLE_ROWS)

    y_prompt = res_p.reshape(batch, seq, d)
    y_sample = res_s[:dec_batch].reshape(dec_batch, 1, d)
    st = lambda xs: jnp.stack(xs, axis=0)
    return (y_prompt, y_sample, st(new_kv_p[0]), st(new_kv_p[1]), st(new_kv_p[2]), st(bh_p), st(bconv_p),
            st(cconv_p), st(new_kv_s[0]), st(new_kv_s[1]), st(new_kv_s[2]), st(bh_s), st(bconv_s), st(cconv_s))
```

```python
import functools

import numpy as np
import jax
import jax.numpy as jnp
from jax import lax
from jax.experimental import pallas as pl
from jax.experimental.pallas import tpu as pltpu

F32 = jnp.float32
BF16 = jnp.bfloat16
I32 = jnp.int32
U32 = jnp.uint32

HEAD_DIM = 128
A_HEADS = 8
DILATIONS = ((128, 1), (512, 4), (2048, 16))
N_DIL = len(DILATIONS)
A_WIDTH = A_HEADS * HEAD_DIM
A_QKV = N_DIL * 3 * A_WIDTH
ATTN_SCALE = HEAD_DIM ** -0.5
QB = 128
B_BLOCKS = 8
B_CONV = 4
C_CONV = 3
LRU_C = 8.0
N_EXPERTS = 16
N_GROUPS = 4
EXP_PER_GROUP = N_EXPERTS // N_GROUPS
EPS = 1e-6
SAMPLE_ROWS = 16
SUBLANES = 8
LANES = 128
NEG_INF = float("-inf")

VMEM_LIMIT = 52 * 1024 * 1024
WIDE_VMEM_LIMIT = 58 * 1024 * 1024


def _params(sem, vmem=VMEM_LIMIT):
    return pltpu.CompilerParams(dimension_semantics=sem, vmem_limit_bytes=vmem)


def _mod_kernel(c_ref, w_ref, b_ref, o_ref):
    c = c_ref[...]
    cs = (c * jax.nn.sigmoid(c)).astype(BF16)
    o_ref[...] = jnp.dot(cs, w_ref[...].astype(BF16), preferred_element_type=F32) + b_ref[...]


def _mod_call(c_all, w_mod, b_mod):
    depth, d, n = w_mod.shape
    rows = c_all.shape[0]
    tn = 512
    return pl.pallas_call(
        _mod_kernel,
        grid=(depth, n // tn),
        in_specs=[
            pl.BlockSpec((rows, d), lambda l, j: (0, 0)),
            pl.BlockSpec((None, d, tn), lambda l, j: (l, 0, j)),
            pl.BlockSpec((None, 1, tn), lambda l, j: (l, 0, j)),
        ],
        out_specs=pl.BlockSpec((None, rows, tn), lambda l, j: (l, 0, j)),
        out_shape=jax.ShapeDtypeStruct((depth, rows, n), F32),
        compiler_params=_params(("arbitrary", "arbitrary")),
        name="mod_proj",
    )(c_all, w_mod, b_mod.reshape(depth, 1, n))


def _modnorm(x, g, sc, sh):
    h = x * lax.rsqrt(jnp.mean(x * x, axis=-1, keepdims=True) + EPS) * g
    return h * (1.0 + sc) + sh


def _norm_kernel(x_ref, g_ref, sc_ref, sh_ref, o_ref):
    o_ref[...] = _modnorm(x_ref[...], g_ref[...], sc_ref[...], sh_ref[...]).astype(o_ref.dtype)


def _route(logits_t, bias_ref):
    score = [jax.nn.sigmoid(logits_t[e:e + 1, :]) for e in range(N_EXPERTS)]
    sel = [score[e] + bias_ref[e] for e in range(N_EXPERTS)]
    gsum = []
    for g in range(N_GROUPS):
        a, b, c, d = sel[4 * g:4 * g + 4]
        hi1, lo1 = jnp.maximum(a, b), jnp.minimum(a, b)
        hi2, lo2 = jnp.maximum(c, d), jnp.minimum(c, d)
        top = jnp.maximum(hi1, hi2)
        second = jnp.maximum(jnp.minimum(hi1, hi2), jnp.maximum(lo1, lo2))
        gsum.append(top + second)
    best = gsum[0]
    grp = jnp.zeros_like(best, dtype=I32)
    for g in range(1, N_GROUPS):
        better = gsum[g] > best
        best = jnp.where(better, gsum[g], best)
        grp = jnp.where(better, g, grp)
    v1 = jnp.full_like(best, NEG_INF)
    e1 = jnp.zeros_like(grp)
    for e in range(N_EXPERTS):
        cand = jnp.where(grp == e // EXP_PER_GROUP, sel[e], NEG_INF)
        better = cand > v1
        v1 = jnp.where(better, cand, v1)
        e1 = jnp.where(better, e, e1)
    v2 = jnp.full_like(best, NEG_INF)
    e2 = jnp.zeros_like(grp)
    for e in range(N_EXPERTS):
        cand = jnp.where((grp == e // EXP_PER_GROUP) & (e1 != e), sel[e], NEG_INF)
        better = cand > v2
        v2 = jnp.where(better, cand, v2)
        e2 = jnp.where(better, e, e2)
    s1 = jnp.zeros_like(best)
    s2 = jnp.zeros_like(best)
    for e in range(N_EXPERTS):
        s1 = jnp.where(e1 == e, score[e], s1)
        s2 = jnp.where(e2 == e, score[e], s2)
    tot = s1 + s2
    return e1, e2, s1 / tot, s2 / tot


def _norm_route_kernel(bias_ref, x_ref, g_ref, sc_ref, sh_ref, wr_ref, cnt0_ref,
                       o_ref, e_ref, w_ref, cnt_ref, run_ref):
    @pl.when(pl.program_id(0) == 0)
    def _():
        run_ref[...] = cnt0_ref[...]

    h = _modnorm(x_ref[...], g_ref[...], sc_ref[...], sh_ref[...])
    o_ref[...] = h
    nt = (((1,), (1,)), ((), ()))
    logits_t = lax.dot_general(wr_ref[...].astype(BF16), h.astype(BF16), nt, preferred_element_type=F32)
    e1, e2, w1, w2 = _route(logits_t, bias_ref)
    n = e1.shape[1]
    expert = lax.broadcasted_iota(I32, (N_EXPERTS, n), 0)
    oh1 = (expert == e1).astype(F32)
    oh2 = (expert == e2).astype(F32)
    before = (lax.broadcasted_iota(I32, (n, n), 0) < lax.broadcasted_iota(I32, (n, n), 1)).astype(BF16)
    pre1 = jnp.dot(oh1.astype(BF16), before, preferred_element_type=F32)
    pre2 = jnp.dot(oh2.astype(BF16), before, preferred_element_type=F32)
    tot1 = jnp.sum(oh1, axis=1, keepdims=True)
    tot2 = jnp.sum(oh2, axis=1, keepdims=True)
    base = run_ref[...][:, :1]
    r1 = jnp.sum(oh1 * (base + pre1), axis=0, keepdims=True)
    r2 = jnp.sum(oh2 * (base + tot1 + pre2), axis=0, keepdims=True)
    run_ref[...] = run_ref[...] + (tot1 + tot2)
    cnt_ref[...] = run_ref[...]
    e_ref[...] = jnp.concatenate([e1, e2, r1.astype(I32), r2.astype(I32), jnp.zeros((SUBLANES - 4, n), I32)], axis=0)
    w_ref[...] = jnp.concatenate([w1, w2, jnp.zeros((SUBLANES - 2, n), F32)], axis=0)


def _mod_specs(per_row, tr, d, rows_per_mod):
    if per_row:
        spec = pl.BlockSpec((tr, d), lambda i, *_: (i, 0))
    else:
        spec = pl.BlockSpec((None, 1, d), lambda i, *_: (i * tr // rows_per_mod, 0, 0))
    return spec


def _norm_call(x, g, sc, sh, *, per_row, tr, rows_per_mod=None):
    rows, d = x.shape
    mspec = _mod_specs(per_row, tr, d, rows_per_mod)
    return pl.pallas_call(
        _norm_kernel,
        grid=(rows // tr,),
        in_specs=[pl.BlockSpec((tr, d), lambda i: (i, 0)),
                  pl.BlockSpec((1, d), lambda i: (0, 0)), mspec, mspec],
        out_specs=pl.BlockSpec((tr, d), lambda i: (i, 0)),
        out_shape=jax.ShapeDtypeStruct((rows, d), BF16),
        compiler_params=_params(("arbitrary",)),
        name="modnorm",
    )(x, g.reshape(1, d), sc, sh)


def _norm_route_call(x, g, sc, sh, w_router_t, b_router, cnt0, *, per_row, tr, rows_per_mod=None):
    rows, d = x.shape
    mspec = _mod_specs(per_row, tr, d, rows_per_mod)
    cspec = pl.BlockSpec((N_EXPERTS, LANES), lambda i, b: (0, 0))
    grid_spec = pltpu.PrefetchScalarGridSpec(
        num_scalar_prefetch=1,
        grid=(rows // tr,),
        in_specs=[pl.BlockSpec((tr, d), lambda i, b: (i, 0)),
                  pl.BlockSpec((1, d), lambda i, b: (0, 0)), mspec, mspec,
                  pl.BlockSpec((N_EXPERTS, d), lambda i, b: (0, 0)), cspec],
        out_specs=[pl.BlockSpec((tr, d), lambda i, b: (i, 0)),
                   pl.BlockSpec((SUBLANES, tr), lambda i, b: (0, i)),
                   pl.BlockSpec((SUBLANES, tr), lambda i, b: (0, i)), cspec],
        scratch_shapes=[pltpu.VMEM((N_EXPERTS, LANES), F32)],
    )
    return pl.pallas_call(
        _norm_route_kernel,
        grid_spec=grid_spec,
        out_shape=[jax.ShapeDtypeStruct((rows, d), F32),
                   jax.ShapeDtypeStruct((SUBLANES, rows), I32),
                   jax.ShapeDtypeStruct((SUBLANES, rows), F32),
                   jax.ShapeDtypeStruct((N_EXPERTS, LANES), F32)],
        compiler_params=_params(("arbitrary",)),
        name="modnorm_route",
    )(b_router, x, g.reshape(1, d), sc, sh, w_router_t, cnt0)


def _dense_wide_kernel(*refs, n_x, residual, li):
    xp_refs = refs[:n_x]
    if residual:
        (xs_ref, w_hbm, rp_ref, gp_ref, rs_ref, gs_ref, op_ref, os_ref, w_f32, w_b16, sem) = refs[n_x:]
    else:
        (xs_ref, w_hbm, op_ref, os_ref, w_f32, w_b16, sem) = refs[n_x:]
    j = pl.program_id(0)
    i = pl.program_id(1)
    tn = w_b16.shape[1]

    def w_copy(jj):
        col = pl.multiple_of(jj * tn, tn)
        return pltpu.make_async_copy(w_hbm.at[li, :, pl.ds(col, tn)], w_f32, sem)

    @pl.when((j == 0) & (i == 0))
    def _():
        w_copy(0).start()

    @pl.when(i == 0)
    def _():
        w_copy(j).wait()
        w_b16[...] = w_f32[...].astype(BF16)

        @pl.when(j + 1 < pl.num_programs(0))
        def _():
            w_copy(j + 1).start()

        acc_s = jnp.dot(xs_ref[...], w_b16[...], preferred_element_type=F32)
        if residual:
            acc_s = rs_ref[...] + gs_ref[...] * acc_s
        os_ref[...] = acc_s.astype(os_ref.dtype)

    acc = None
    k0 = 0
    for x_ref in xp_refs:
        kx = x_ref.shape[1]
        part = jnp.dot(x_ref[...], w_b16[k0:k0 + kx, :], preferred_element_type=F32)
        acc = part if acc is None else acc + part
        k0 += kx
    if residual:
        acc = rp_ref[...] + gp_ref[...] * acc
    op_ref[...] = acc.astype(op_ref.dtype)


def _dense_wide_call(xps, xs, w, li, *, res=None, rows_per_mod=None, out_dtype=F32, tm=1024, tn=1024,
                     vmem=VMEM_LIMIT):
    tp = xps[0].shape[0]
    k, n = w.shape[1:]
    rs = xs.shape[0]
    in_specs = [pl.BlockSpec((tm, x.shape[1]), lambda j, i: (i, 0)) for x in xps]
    in_specs += [pl.BlockSpec((rs, k), lambda j, i: (0, 0)), pl.BlockSpec(memory_space=pl.ANY)]
    args = list(xps) + [xs, w]
    if res is not None:
        in_specs += [pl.BlockSpec((tm, tn), lambda j, i: (i, j)),
                     pl.BlockSpec((None, 1, tn), lambda j, i: (i * tm // rows_per_mod, 0, j)),
                     pl.BlockSpec((rs, tn), lambda j, i: (0, j)),
                     pl.BlockSpec((rs, tn), lambda j, i: (0, j))]
        args += list(res)
    return pl.pallas_call(
        functools.partial(_dense_wide_kernel, n_x=len(xps), residual=res is not None, li=li),
        grid=(n // tn, tp // tm),
        in_specs=in_specs,
        out_specs=[pl.BlockSpec((tm, tn), lambda j, i: (i, j)),
                   pl.BlockSpec((rs, tn), lambda j, i: (0, j))],
        out_shape=[jax.ShapeDtypeStruct((tp, n), out_dtype), jax.ShapeDtypeStruct((rs, n), out_dtype)],
        scratch_shapes=[pltpu.VMEM((k, tn), F32), pltpu.VMEM((k, tn), BF16), pltpu.SemaphoreType.DMA],
        compiler_params=_params(("arbitrary", "arbitrary"), vmem),
        name="dense_wide",
    )(*args)


def _rms_rows(x, g):
    return x * lax.rsqrt(jnp.mean(x * x, axis=-1, keepdims=True) + EPS) * g


def _attn_prompt_kernel(q0, k0, v0, q1, k1, v1, q2, k2, v2, gq_ref, gk_ref, slope_ref,
                        ya_ref, kv0, kv1, kv2, qn_s, m_s, l_s, acc_s, kn0, kn1, kn2, kv_sem, *, seq):
    qkv = ((q0, k0, v0, kn0), (q1, k1, v1, kn1), (q2, k2, v2, kn2))
    kv_out = (kv0, kv1, kv2)
    b_idx = pl.program_id(0)
    h_idx = pl.program_id(1)

    def kv_copy(g, c):
        keep = kv_out[g].shape[1]
        src = (qkv[g][3], qkv[g][2])[c].at[pl.ds(seq - keep, keep)]
        dst = kv_out[g].at[b_idx, pl.ds(0, keep), c, h_idx, pl.ds(0, HEAD_DIM)]
        return pltpu.make_async_copy(src, dst, kv_sem.at[2 * g + c])

    slope = slope_ref[...]
    row = lax.broadcasted_iota(I32, (QB, 2 * QB), 0)
    col = lax.broadcasted_iota(I32, (QB, 2 * QB), 1)
    rel_prev = row + QB - col
    rel_first = (row - col)[:, :QB]
    nt = (((1,), (1,)), ((), ()))

    for g, (window, dil) in enumerate(DILATIONS):
        q_ref, k_ref, v_ref, kn_ref = qkv[g]
        nk = window // dil
        n_blk = seq // dil // QB
        qn_s[...] = _rms_rows(q_ref[...], gq_ref[g])
        kn_ref[...] = _rms_rows(k_ref[...], gk_ref[g])
        kv_copy(g, 0).start()
        kv_copy(g, 1).start()
        sl = slope[:, :1] * float(dil)
        bias_prev = jnp.where((rel_prev >= 0) & (rel_prev <= nk), -sl * rel_prev.astype(F32), NEG_INF)
        bias_first = jnp.where(rel_first >= 0, -sl * rel_first.astype(F32), NEG_INF)

        def rows(start, size, dil=dil):
            return pl.ds(start, size) if dil == 1 else pl.ds(start, size, stride=dil)

        def block(qstart, kstart, first, g=g, kn_ref=kn_ref, v_ref=v_ref, rows=rows,
                  bias_prev=bias_prev, bias_first=bias_first):
            nkeys = QB if first else 2 * QB
            qb = qn_s[rows(qstart, QB), :].astype(BF16)
            kb = kn_ref[rows(kstart, nkeys), :].astype(BF16)
            vb = v_ref[rows(kstart, nkeys), :].astype(BF16)
            s = lax.dot_general(qb, kb, nt, preferred_element_type=F32) * ATTN_SCALE
            s = s + (bias_first if first else bias_prev)
            m_blk = jnp.max(s, axis=-1, keepdims=True)
            p = jnp.exp(s - m_blk)
            l_blk = jnp.sum(p, axis=-1, keepdims=True)
            o_blk = jnp.dot(p.astype(BF16), vb, preferred_element_type=F32)
            qrows = rows(qstart, QB)
            if g == 0:
                m_s[qrows, :] = jnp.broadcast_to(m_blk, (QB, HEAD_DIM))
                l_s[qrows, :] = jnp.broadcast_to(l_blk, (QB, HEAD_DIM))
                acc_s[qrows, :] = o_blk
            else:
                m_old = m_s[qrows, :]
                m_new = jnp.maximum(m_old, m_blk)
                a_old = jnp.exp(m_old - m_new)
                a_blk = jnp.exp(m_blk - m_new)
                m_s[qrows, :] = m_new
                l_s[qrows, :] = l_s[qrows, :] * a_old + l_blk * a_blk
                acc_s[qrows, :] = acc_s[qrows, :] * a_old + o_blk * a_blk

        if dil == 1:
            block(0, 0, True)

            def body(n, carry, block=block):
                qstart = pl.multiple_of(n * QB, QB)
                block(qstart, pl.multiple_of(qstart - QB, QB), False)
                return carry

            lax.fori_loop(1, n_blk, body, 0)
        else:
            def body(r, carry, block=block, n_blk=n_blk, dil=dil):
                block(r, r, True)
                for n in range(1, n_blk):
                    block(r + n * QB * dil, r + (n - 1) * QB * dil, False)
                return carry

            lax.fori_loop(0, dil, body, 0)

    ya_ref[...] = (acc_s[...] / l_s[...]).astype(ya_ref.dtype)
    for g in range(N_DIL):
        kv_copy(g, 0).wait()
        kv_copy(g, 1).wait()


def _attn_prompt_call(z, gq, gk, slopes, *, batch, seq):
    t = batch * seq
    kv_shapes = [jax.ShapeDtypeStruct((batch, min(window, seq), 2, A_HEADS, HEAD_DIM), F32)
                 for window, _ in DILATIONS]

    def col(g, c):
        return lambda b, h: (b, g * 3 * A_HEADS + c * A_HEADS + h)

    in_specs = []
    for g in range(N_DIL):
        for c in range(3):
            in_specs.append(pl.BlockSpec((seq, HEAD_DIM), col(g, c)))
    gspec = pl.BlockSpec((N_DIL, None, 1, HEAD_DIM), lambda b, h: (0, h, 0, 0))
    in_specs += [gspec, gspec, pl.BlockSpec((None, 1, HEAD_DIM), lambda b, h: (h, 0, 0))]
    outs = pl.pallas_call(
        functools.partial(_attn_prompt_kernel, seq=seq),
        grid=(batch, A_HEADS),
        in_specs=in_specs,
        out_specs=[pl.BlockSpec((seq, HEAD_DIM), lambda b, h: (b, h))] + [pl.BlockSpec(memory_space=pl.ANY)] * 3,
        out_shape=[jax.ShapeDtypeStruct((t, A_WIDTH), BF16)] + kv_shapes,
        scratch_shapes=[pltpu.VMEM((seq, HEAD_DIM), F32)] * 7 + [pltpu.SemaphoreType.DMA((2 * N_DIL,))],
        compiler_params=_params(("arbitrary", "arbitrary")),
        name="attn_prompt",
    )(*([z] * 9), gq.reshape(N_DIL, A_HEADS, 1, HEAD_DIM), gk.reshape(N_DIL, A_HEADS, 1, HEAD_DIM), slopes)
    return outs[0], outs[1:]


def _round_bf16(x):
    return x.astype(BF16).astype(F32)


def _attn_sample_kernel(z_ref, c0_ref, c1_ref, c2_ref, gq_ref, gk_ref, slope_ref, ya_ref, new_ref):
    caches = (c0_ref, c1_ref, c2_ref)
    slope = slope_ref[...][:, :1]
    nk = DILATIONS[0][0] // DILATIONS[0][1]
    steps = (nk - lax.broadcasted_iota(I32, (nk, 1, 1), 0)).astype(F32)
    m_run = l_run = acc = None
    for g, (window, dil) in enumerate(DILATIONS):
        base = g * 3 * A_HEADS
        q = _rms_rows(z_ref[base:base + A_HEADS, :], gq_ref[g])
        k = _rms_rows(z_ref[base + A_HEADS:base + 2 * A_HEADS, :], gk_ref[g])
        v = z_ref[base + 2 * A_HEADS:base + 3 * A_HEADS, :]
        new_ref[g, 0] = k
        new_ref[g, 1] = v
        qr = _round_bf16(q)
        ck = _round_bf16(caches[g][:, 0])
        cv = _round_bf16(caches[g][:, 1])
        s_c = jnp.sum(ck * qr[None], axis=-1, keepdims=True) * ATTN_SCALE
        s_c = s_c - (slope * float(dil))[None] * steps
        s_n = jnp.sum(_round_bf16(k) * qr, axis=-1, keepdims=True) * ATTN_SCALE
        m_g = jnp.maximum(jnp.max(s_c, axis=0), s_n)
        p_c = jnp.exp(s_c - m_g[None])
        p_n = jnp.exp(s_n - m_g)
        l_g = jnp.sum(p_c, axis=0) + p_n
        o_g = jnp.sum(_round_bf16(p_c) * cv, axis=0) + _round_bf16(p_n) * _round_bf16(v)
        if g == 0:
            m_run, l_run, acc = m_g, l_g, o_g
        else:
            m_new = jnp.maximum(m_run, m_g)
            a_old = jnp.exp(m_run - m_new)
            a_g = jnp.exp(m_g - m_new)
            l_run = l_run * a_old + l_g * a_g
            acc = acc * a_old + o_g * a_g
            m_run = m_new
    ya_ref[...] = acc / l_run


def _attn_sample_call(z_s, caches, gq, gk, slopes8, *, dec_batch):
    n_colblk = z_s.shape[1] // HEAD_DIM
    z3 = z_s.reshape(z_s.shape[0], n_colblk, HEAD_DIM)
    in_specs = [pl.BlockSpec((None, n_colblk, HEAD_DIM), lambda b: (b, 0, 0))]
    args = [z3]
    for g, (window, dil) in enumerate(DILATIONS):
        nk = window // dil
        c = caches[g].reshape(dec_batch, nk, dil, 2, A_HEADS, HEAD_DIM)
        in_specs.append(pl.BlockSpec((None, nk, None, 2, A_HEADS, HEAD_DIM), lambda b: (b, 0, 0, 0, 0, 0)))
        args.append(c)
    full = lambda shape: pl.BlockSpec(shape, lambda b: (0,) * len(shape))
    in_specs += [full((N_DIL, A_HEADS, HEAD_DIM)), full((N_DIL, A_HEADS, HEAD_DIM)), full((A_HEADS, HEAD_DIM))]
    return pl.pallas_call(
        _attn_sample_kernel,
        grid=(dec_batch,),
        in_specs=in_specs,
        out_specs=[pl.BlockSpec((None, A_HEADS, HEAD_DIM), lambda b: (b, 0, 0)),
                   pl.BlockSpec((None, N_DIL, 2, A_HEADS, HEAD_DIM), lambda b: (b, 0, 0, 0, 0))],
        out_shape=[jax.ShapeDtypeStruct((dec_batch, A_HEADS, HEAD_DIM), F32),
                   jax.ShapeDtypeStruct((dec_batch, N_DIL, 2, A_HEADS, HEAD_DIM), F32)],
        compiler_params=_params(("arbitrary",)),
        name="attn_sample",
    )(*args, gq, gk, slopes8)


def _cache_roll_kernel(cur_ref, nxt_ref, new_ref, o_ref):
    last = pl.program_id(1) == pl.num_programs(1) - 1
    rows = cur_ref.shape[0]
    o_ref[0:rows - 1] = cur_ref[1:rows]
    o_ref[rows - 1:rows] = jnp.where(last, new_ref[...], nxt_ref[...])


def _cache_roll_call(cache, new, *, rows=256):
    b, w = cache.shape[:2]
    rows = min(rows, w)
    tail = cache.shape[2:]
    n_chunks = w // rows
    zeros = (0,) * len(tail)
    return pl.pallas_call(
        _cache_roll_kernel,
        grid=(b, n_chunks),
        in_specs=[pl.BlockSpec((None, rows) + tail, lambda i, j: (i, j) + zeros),
                  pl.BlockSpec((None, 1) + tail, lambda i, j: (i, jnp.minimum((j + 1) * rows, w - 1)) + zeros),
                  pl.BlockSpec((None, 1) + tail, lambda i, j: (i, 0) + zeros)],
        out_specs=pl.BlockSpec((None, rows) + tail, lambda i, j: (i, j) + zeros),
        out_shape=jax.ShapeDtypeStruct(cache.shape, cache.dtype),
        compiler_params=_params(("arbitrary", "arbitrary")),
        name="cache_roll",
    )(cache, cache, new)


def _softplus(x):
    return jnp.maximum(x, 0.0) + jnp.log1p(jnp.exp(-jnp.abs(x)))


def _lru_coeffs(xc, wa, ba, wi, bi, lam):
    xcb = xc.astype(BF16)
    r = jax.nn.sigmoid(jnp.dot(xcb, wa.astype(BF16), preferred_element_type=F32) + ba)
    i = jax.nn.sigmoid(jnp.dot(xcb, wi.astype(BF16), preferred_element_type=F32) + bi)
    log_a = -LRU_C * r * _softplus(-lam)
    a = jnp.exp(log_a)
    t = jnp.tanh(log_a)
    one_minus_a2 = -2.0 * t / (1.0 - t)
    return a, jnp.sqrt(one_minus_a2) * (i * xc)


def _lru_prompt_kernel(x_ref, gate_ref, cw_ref, cb_ref, wa_ref, ba_ref, wi_ref, bi_ref, lam_ref,
                       y_ref, hl_ref, pad_a, pad_b, *, seq):
    blk = x_ref.shape[1]
    off = SUBLANES
    x = x_ref[...]
    pad_a[0:off, :] = jnp.zeros((off, blk), F32)
    pad_a[off:off + seq, :] = x
    xc = pad_a[off - 3:off - 3 + seq, :] * cw_ref[0:1, :]
    xc = xc + pad_a[off - 2:off - 2 + seq, :] * cw_ref[1:2, :]
    xc = xc + pad_a[off - 1:off - 1 + seq, :] * cw_ref[2:3, :]
    xc = xc + x * cw_ref[3:4, :] + cb_ref[...]
    a, b = _lru_coeffs(xc, wa_ref[...], ba_ref[...], wi_ref[...], bi_ref[...], lam_ref[...])

    tpos = lax.broadcasted_iota(I32, (seq, blk), 0) & (SUBLANES - 1)
    pad_b[0:off, :] = jnp.zeros((off, blk), F32)
    for o in (1, 2, 4):
        pad_a[off:off + seq, :] = a
        pad_b[off:off + seq, :] = b
        keep = tpos >= o
        a_prev = jnp.where(keep, pad_a[off - o:off - o + seq, :], 1.0)
        b_prev = jnp.where(keep, pad_b[off - o:off - o + seq, :], 0.0)
        b = a * b_prev + b
        a = a * a_prev
    pad_a[off:off + seq, :] = a
    pad_b[off:off + seq, :] = b

    def tile(i, h_prev):
        r0 = pl.multiple_of(off + i * SUBLANES, SUBLANES)
        h = pad_b[pl.ds(r0, SUBLANES), :] + pad_a[pl.ds(r0, SUBLANES), :] * h_prev
        pad_b[pl.ds(r0, SUBLANES), :] = h
        return jnp.broadcast_to(h[SUBLANES - 1:SUBLANES, :], (SUBLANES, blk))

    lax.fori_loop(0, seq // SUBLANES, tile, jnp.zeros((SUBLANES, blk), F32))
    h = pad_b[off:off + seq, :]
    hl_ref[...] = h[seq - SUBLANES:, :]
    y_ref[...] = (h * jax.nn.gelu(gate_ref[...])).astype(y_ref.dtype)


def _lru_prompt_call(z, cw, cb, wa, ba, wi, bi, lam, *, batch, seq):
    t = batch * seq
    nb, blk, _ = wa.shape
    width = nb * blk
    x0 = A_QKV // blk
    g0 = (A_QKV + width) // blk
    vec = lambda v: v.reshape(1, width)
    vspec = pl.BlockSpec((1, blk), lambda b, n: (0, n))
    return pl.pallas_call(
        functools.partial(_lru_prompt_kernel, seq=seq),
        grid=(batch, nb),
        in_specs=[pl.BlockSpec((seq, blk), lambda b, n: (b, x0 + n)),
                  pl.BlockSpec((seq, blk), lambda b, n: (b, g0 + n)),
                  pl.BlockSpec((B_CONV, blk), lambda b, n: (0, n)), vspec,
                  pl.BlockSpec((None, blk, blk), lambda b, n: (n, 0, 0)), vspec,
                  pl.BlockSpec((None, blk, blk), lambda b, n: (n, 0, 0)), vspec, vspec],
        out_specs=[pl.BlockSpec((seq, blk), lambda b, n: (b, n)),
                   pl.BlockSpec((None, SUBLANES, blk), lambda b, n: (b, 0, n))],
        out_shape=[jax.ShapeDtypeStruct((t, width), BF16),
                   jax.ShapeDtypeStruct((batch, SUBLANES, width), F32)],
        scratch_shapes=[pltpu.VMEM((seq + SUBLANES, blk), F32)] * 2,
        compiler_params=_params(("arbitrary", "arbitrary")),
        name="rglru_prompt",
    )(z, z, cw, vec(cb), wa, vec(ba), wi, vec(bi), vec(lam))


def _lru_sample_kernel(x_ref, gate_ref, buf_ref, h0_ref, cw_ref, cb_ref, wa_ref, ba_ref, wi_ref, bi_ref,
                       lam_ref, y_ref, h_ref):
    nb, blk, _ = wa_ref.shape
    for n in range(nb):
        cs = slice(n * blk, (n + 1) * blk)
        x = x_ref[:, cs]
        xc = buf_ref[0, :, cs] * cw_ref[0:1, cs]
        xc = xc + buf_ref[1, :, cs] * cw_ref[1:2, cs]
        xc = xc + buf_ref[2, :, cs] * cw_ref[2:3, cs]
        xc = xc + x * cw_ref[3:4, cs] + cb_ref[:, cs]
        a, b = _lru_coeffs(xc, wa_ref[n], ba_ref[:, cs], wi_ref[n], bi_ref[:, cs], lam_ref[:, cs])
        h = b + a * h0_ref[:, cs]
        h_ref[:, cs] = h
        y_ref[:, cs] = (h * jax.nn.gelu(gate_ref[:, cs])).astype(y_ref.dtype)


def _lru_sample_call(x, gate, buf, h0, cw, cb, wa, ba, wi, bi, lam):
    rows, width = x.shape
    vec = lambda v: v.reshape(1, width)
    return pl.pallas_call(
        _lru_sample_kernel,
        out_shape=[jax.ShapeDtypeStruct((rows, width), BF16), jax.ShapeDtypeStruct((rows, width), F32)],
        compiler_params=pltpu.CompilerParams(vmem_limit_bytes=VMEM_LIMIT),
        name="rglru_sample",
    )(x, gate, buf, h0, cw, vec(cb), wa, vec(ba), wi, vec(bi), vec(lam))


def _proj_gconv_kernel(xp_ref, xs_ref, w_hbm, cw_ref, buf_ref, m_ref, ut_ref, ms_ref, us_ref,
                       w_f32, w_cat, pad, sem, *, li, tiles_per_seq):
    j = pl.program_id(0)
    i = pl.program_id(1)
    tm = xp_ref.shape[0]
    blk = cw_ref.shape[1]
    n_piece = 3
    d_model = w_hbm.shape[2] // n_piece

    def w_copy(k, jj):
        col = pl.multiple_of(k * d_model + jj * blk, blk)
        return pltpu.make_async_copy(w_hbm.at[li, :, pl.ds(col, blk)], w_f32.at[k], sem.at[k])

    @pl.when((j == 0) & (i == 0))
    def _():
        for k in range(n_piece):
            w_copy(k, 0).start()

    @pl.when(i == 0)
    def _():
        for k in range(n_piece):
            w_copy(k, j).wait()
            w_cat[:, k * blk:(k + 1) * blk] = w_f32[k].astype(BF16)

        @pl.when(j + 1 < pl.num_programs(0))
        def _():
            for k in range(n_piece):
                w_copy(k, j + 1).start()

        acc_s = jnp.dot(xs_ref[...], w_cat[...], preferred_element_type=F32)
        u_s = acc_s[:, blk:2 * blk] * acc_s[:, 2 * blk:]
        uc_s = buf_ref[0] * cw_ref[0:1, :] + buf_ref[1] * cw_ref[1:2, :]
        uc_s = uc_s + u_s * cw_ref[2:3, :]
        ms_ref[...] = (acc_s[:, :blk] * uc_s).astype(ms_ref.dtype)
        us_ref[...] = u_s

    @pl.when(i % tiles_per_seq == 0)
    def _():
        pad[0:SUBLANES, :] = jnp.zeros((SUBLANES, blk), F32)

    acc = jnp.dot(xp_ref[...], w_cat[...], preferred_element_type=F32)
    u = acc[:, blk:2 * blk] * acc[:, 2 * blk:]
    off = SUBLANES
    pad[off:off + tm, :] = u
    uc = pad[off - 2:off - 2 + tm, :] * cw_ref[0:1, :]
    uc = uc + pad[off - 1:off - 1 + tm, :] * cw_ref[1:2, :]
    uc = uc + u * cw_ref[2:3, :]
    m_ref[...] = (acc[:, :blk] * uc).astype(m_ref.dtype)
    tail = u[tm - SUBLANES:, :]
    ut_ref[...] = tail
    pad[0:SUBLANES, :] = tail


def _proj_gconv_call(xp, xs, w, li, cw, buf_s, *, batch, seq, tm=1024, blk=256):
    t, k = xp.shape
    rs = xs.shape[0]
    d = w.shape[2] // 3
    nblk = d // blk
    tiles_per_seq = seq // tm
    return pl.pallas_call(
        functools.partial(_proj_gconv_kernel, li=li, tiles_per_seq=tiles_per_seq),
        grid=(nblk, t // tm),
        in_specs=[pl.BlockSpec((tm, k), lambda j, i: (i, 0)),
                  pl.BlockSpec((rs, k), lambda j, i: (0, 0)),
                  pl.BlockSpec(memory_space=pl.ANY),
                  pl.BlockSpec((C_CONV, blk), lambda j, i: (0, j)),
                  pl.BlockSpec((C_CONV - 1, rs, blk), lambda j, i: (0, 0, j))],
        out_specs=[pl.BlockSpec((tm, blk), lambda j, i: (i, j)),
                   pl.BlockSpec((None, SUBLANES, blk), lambda j, i: (i // tiles_per_seq, 0, j)),
                   pl.BlockSpec((rs, blk), lambda j, i: (0, j)),
                   pl.BlockSpec((rs, blk), lambda j, i: (0, j))],
        out_shape=[jax.ShapeDtypeStruct((t, d), BF16), jax.ShapeDtypeStruct((batch, SUBLANES, d), F32),
                   jax.ShapeDtypeStruct((rs, d), BF16), jax.ShapeDtypeStruct((rs, d), F32)],
        scratch_shapes=[pltpu.VMEM((3, k, blk), F32), pltpu.VMEM((k, 3 * blk), BF16),
                        pltpu.VMEM((tm + SUBLANES, blk), F32), pltpu.SemaphoreType.DMA((3,))],
        compiler_params=_params(("arbitrary", "arbitrary")),
        name="proj_gconv",
    )(xp, xs, w, cw, buf_s)


MOE_TM = 512
MOE_HALF = MOE_TM // 2


def _bucket_row(route_ref, start_ref, k, t):
    return start_ref[0, route_ref[k, t]] + route_ref[2 + k, t]


def _bucket_kernel(rp_ref, rs_ref, start_ref, ztile_ref, hp_ref, hs_ref, xs_ref, zero_buf, sem, zsem):
    step = pl.program_id(0)
    tr = hp_ref.shape[0]
    n_s = rs_ref.shape[1]

    def copy_p(t, k):
        row = _bucket_row(rp_ref, start_ref, k, t)
        return pltpu.make_async_copy(hp_ref.at[pl.ds(t, 1)], xs_ref.at[pl.ds(row, 1)], sem)

    def copy_s(t, k):
        row = _bucket_row(rs_ref, start_ref, k, t)
        return pltpu.make_async_copy(hs_ref.at[pl.ds(t, 1)], xs_ref.at[pl.ds(row, 1)], sem)

    def copy_z(j):
        start = pl.multiple_of(ztile_ref[0, j] * MOE_HALF, MOE_HALF)
        return pltpu.make_async_copy(zero_buf, xs_ref.at[pl.ds(start, MOE_HALF)], zsem)

    @pl.when(step == 0)
    def _():
        zero_buf[...] = jnp.zeros_like(zero_buf)
        for j in range(ztile_ref.shape[1]):
            pl.when(ztile_ref[1, j] == 1)(lambda j=j: copy_z(j).start())
        for j in range(ztile_ref.shape[1]):
            pl.when(ztile_ref[1, j] == 1)(lambda j=j: copy_z(j).wait())
        for t in range(n_s):
            copy_s(t, 0).start()
            copy_s(t, 1).start()
        for t in range(n_s):
            copy_s(t, 0).wait()
            copy_s(t, 1).wait()

    def start_p(i, c):
        for j in range(SUBLANES):
            copy_p(i * SUBLANES + j, 0).start(priority=0)
            copy_p(i * SUBLANES + j, 1).start(priority=1)
        return c

    def wait_p(t, c):
        copy_p(t, 0).wait()
        copy_p(t, 1).wait()
        return c

    lax.fori_loop(0, tr // SUBLANES, start_p, 0)
    lax.fori_loop(0, tr, wait_p, 0)


def _bucket_call(rp, rs, row_start, ztiles, hp, hs, p_rows, *, tr=256):
    tp, dh = hp.shape
    n_s = rs.shape[1]
    smem = lambda shape, imap: pl.BlockSpec(shape, imap, memory_space=pltpu.SMEM)
    return pl.pallas_call(
        _bucket_kernel,
        grid=(tp // tr,),
        in_specs=[smem((SUBLANES, tr), lambda i: (0, i)), smem((SUBLANES, n_s), lambda i: (0, 0)),
                  smem(row_start.shape, lambda i: (0, 0)), smem(ztiles.shape, lambda i: (0, 0)),
                  pl.BlockSpec((tr, dh), lambda i: (i, 0)),
                  pl.BlockSpec((n_s, dh), lambda i: (0, 0))],
        out_specs=pl.BlockSpec(memory_space=pl.ANY),
        out_shape=jax.ShapeDtypeStruct((p_rows, dh), hp.dtype),
        scratch_shapes=[pltpu.VMEM((MOE_HALF, dh), hp.dtype), pltpu.SemaphoreType.DMA, pltpu.SemaphoreType.DMA],
        compiler_params=_params(("arbitrary",)),
        name="moe_bucket",
    )(rp, rs, row_start, ztiles, hp, hs)


(S_XTILE, S_OTILE, S_EXPERT, S_WCHUNK, S_OCHUNK, S_FIRST, S_VALID, S_FULL,
 S_HAS_NEXT, S_NEXT_EXPERT, S_NEXT_CHUNK) = range(11)


def _grouped_kernel(s_ref, x_ref, *refs, n_w, layer, tn, swiglu):
    w_hbm = refs[:n_w]
    o_ref = refs[n_w]
    w_f32 = refs[n_w + 1:2 * n_w + 1]
    w_b16 = refs[2 * n_w + 1]
    sem = refs[2 * n_w + 2]
    it = pl.program_id(0)

    def w_copy(k, expert, chunk):
        col = pl.multiple_of(chunk * tn, LANES)
        return pltpu.make_async_copy(w_hbm[k].at[layer, expert, :, pl.ds(col, tn)], w_f32[k], sem.at[k])

    @pl.when(it == 0)
    def _():
        for k in range(n_w):
            w_copy(k, s_ref[S_EXPERT, 0], s_ref[S_WCHUNK, 0]).start()

    @pl.when(s_ref[S_FIRST, it] == 1)
    def _():
        for k in range(n_w):
            w_copy(k, s_ref[S_EXPERT, it], s_ref[S_WCHUNK, it]).wait()
            w_b16[:, k * tn:(k + 1) * tn] = w_f32[k][...].astype(BF16)

        @pl.when(s_ref[S_HAS_NEXT, it] == 1)
        def _():
            for k in range(n_w):
                w_copy(k, s_ref[S_NEXT_EXPERT, it], s_ref[S_NEXT_CHUNK, it]).start()

    def compute(rows):
        acc = jnp.dot(x_ref[0:rows, :].astype(BF16), w_b16[...], preferred_element_type=F32)
        if swiglu:
            gate = acc[:, :tn]
            acc = gate * jax.nn.sigmoid(gate) * acc[:, tn:]
        o_ref[0:rows, :] = acc.astype(o_ref.dtype)

    valid = s_ref[S_VALID, it] == 1
    full = s_ref[S_FULL, it] == 1

    @pl.when(valid & full)
    def _():
        compute(MOE_TM)

    @pl.when(valid & jnp.logical_not(full))
    def _():
        compute(MOE_HALF)
        o_ref[MOE_HALF:, :] = jnp.zeros((MOE_TM - MOE_HALF, tn), o_ref.dtype)

    @pl.when(jnp.logical_not(valid))
    def _():
        o_ref[...] = jnp.zeros_like(o_ref)


def _grouped_call(sched, x, weights, layer, *, tn, out_dtype, swiglu, name):
    p, k = x.shape
    n = weights[0].shape[3]
    n_w = len(weights)
    n_items = sched.shape[1]
    grid_spec = pltpu.PrefetchScalarGridSpec(
        num_scalar_prefetch=1,
        grid=(n_items,),
        in_specs=[pl.BlockSpec((MOE_TM, k), lambda it, s: (s[S_XTILE, it], 0))]
                 + [pl.BlockSpec(memory_space=pl.ANY)] * n_w,
        out_specs=pl.BlockSpec((MOE_TM, tn), lambda it, s: (s[S_OTILE, it], s[S_OCHUNK, it])),
        scratch_shapes=[pltpu.VMEM((k, tn), F32)] * n_w + [pltpu.VMEM((k, n_w * tn), BF16),
                                                           pltpu.SemaphoreType.DMA((n_w,))],
    )
    return pl.pallas_call(
        functools.partial(_grouped_kernel, n_w=n_w, layer=layer, tn=tn, swiglu=swiglu),
        grid_spec=grid_spec,
        out_shape=jax.ShapeDtypeStruct((p, n), out_dtype),
        compiler_params=_params(("arbitrary",)),
        name=name,
    )(sched, x, *weights)


def _combine_kernel(route_ref, next_route_ref, start_ref, res_ref, gate_ref, w_ref, y_ref, *refs, tr, with_norm):
    if with_norm:
        g_ref, sc_ref, sh_ref, o_ref, h_ref, buf, sem = refs
    else:
        o_ref, buf, sem = refs
    step = pl.program_id(0)
    slot = step % 2

    def copy(table, s, t, k):
        row = _bucket_row(table, start_ref, k, t)
        return pltpu.make_async_copy(y_ref.at[pl.ds(row, 1)], buf.at[s, k, pl.ds(t, 1)], sem.at[s])

    def issue(table, s):
        def body(i, c):
            for j in range(SUBLANES):
                copy(table, s, i * SUBLANES + j, 0).start(priority=0)
                copy(table, s, i * SUBLANES + j, 1).start(priority=1)
            return c
        lax.fori_loop(0, tr // SUBLANES, body, 0)

    @pl.when(step == 0)
    def _():
        issue(route_ref, 0)

    @pl.when(step + 1 < pl.num_programs(0))
    def _():
        issue(next_route_ref, 1 - slot)

    def wait(t, c):
        copy(route_ref, slot, t, 0).wait()
        copy(route_ref, slot, t, 1).wait()
        return c

    lax.fori_loop(0, tr, wait, 0)
    w = jnp.transpose(w_ref[...])
    y = w[:, 0:1] * buf[slot, 0] + w[:, 1:2] * buf[slot, 1]
    new_res = res_ref[...] + gate_ref[...] * y
    o_ref[...] = new_res
    if with_norm:
        h_ref[...] = _modnorm(new_res, g_ref[...], sc_ref[...], sh_ref[...]).astype(h_ref.dtype)


def _combine_call(route, row_start, res, gate, w, y, *, per_row, tr, rows_per_mod=None, next_norm=None):
    rows, d = res.shape
    gspec = _mod_specs(per_row, tr, d, rows_per_mod)
    row_block = pl.BlockSpec((tr, d), lambda i: (i, 0))
    extra_in, extra_args, out_specs, out_shape = [], [], row_block, jax.ShapeDtypeStruct((rows, d), F32)
    if next_norm is not None:
        extra_in = [pl.BlockSpec((1, d), lambda i: (0, 0)), gspec, gspec]
        extra_args = [next_norm[0].reshape(1, d), next_norm[1], next_norm[2]]
        out_specs = [row_block, row_block]
        out_shape = [out_shape, jax.ShapeDtypeStruct((rows, d), BF16)]
    smem = lambda shape, imap: pl.BlockSpec(shape, imap, memory_space=pltpu.SMEM)
    n_tiles = rows // tr
    return pl.pallas_call(
        functools.partial(_combine_kernel, tr=tr, with_norm=next_norm is not None),
        grid=(n_tiles,),
        in_specs=[smem((SUBLANES, tr), lambda i: (0, i)),
                  smem((SUBLANES, tr), lambda i: (0, jnp.minimum(i + 1, n_tiles - 1))),
                  smem(row_start.shape, lambda i: (0, 0)),
                  row_block, gspec,
                  pl.BlockSpec((SUBLANES, tr), lambda i: (0, i)),
                  pl.BlockSpec(memory_space=pl.ANY)] + extra_in,
        out_specs=out_specs,
        out_shape=out_shape,
        scratch_shapes=[pltpu.VMEM((2, 2, tr, d), F32), pltpu.SemaphoreType.DMA((2,))],
        compiler_params=_params(("arbitrary",)),
        name="moe_combine",
    )(route, route, row_start, res, gate, w, y, *extra_args)


def _moe_schedule(counts, max_tiles, n_chunks):
    e_ids = jnp.arange(N_EXPERTS, dtype=I32)
    tiles_per_e = (counts + MOE_TM - 1) // MOE_TM
    tile_end = jnp.cumsum(tiles_per_e)
    tile_start = tile_end - tiles_per_e
    used = tile_end[-1]
    n_valid = used * n_chunks
    nonempty = tiles_per_e > 0
    later = (e_ids[None, :] > e_ids[:, None]) & nonempty[None, :]
    next_e = jnp.min(jnp.where(later, e_ids[None, :], N_EXPERTS - 1), axis=1)
    last_e = jnp.max(jnp.where(nonempty, e_ids, 0))
    it = jnp.arange(n_chunks * max_tiles, dtype=I32)
    ex = jnp.minimum(jnp.sum((it[:, None] >= (tile_end * n_chunks)[None, :]).astype(I32), axis=1), N_EXPERTS - 1)
    onehot = (ex[:, None] == e_ids[None, :]).astype(I32)
    look = lambda v: jnp.sum(onehot * v.astype(I32)[None, :], axis=1)
    t_e = jnp.maximum(look(tiles_per_e), 1)
    t0 = look(tile_start)
    local = it - t0 * n_chunks
    chunk = local // t_e
    idx = local % t_e
    valid = it < n_valid
    first = (idx == 0) & valid
    full = look(counts) - idx * MOE_TM > MOE_HALF
    has_next = first & (it + t_e < n_valid)
    last_chunk = chunk == n_chunks - 1
    next_expert = jnp.where(last_chunk, look(next_e), ex)
    next_chunk = jnp.where(last_chunk, 0, chunk + 1)
    q = it - n_valid
    xtile = jnp.where(valid, t0 + idx, used - 1)
    otile = jnp.where(valid, t0 + idx, used + q // n_chunks)
    ochunk = jnp.where(valid, chunk, q % n_chunks)
    return jnp.stack([xtile, otile, jnp.where(valid, ex, last_e), jnp.where(valid, chunk, n_chunks - 1), ochunk,
                      first, valid, full, has_next, next_expert, next_chunk]).astype(I32)


def _moe(hp, hs, rp, rs, counts, w_gate, w_up, w_down, layer):
    n_assign = 2 * (hp.shape[0] + hs.shape[0])
    max_tiles = (n_assign + N_EXPERTS * (MOE_TM - 1)) // MOE_TM
    tiles_per_e = (counts + MOE_TM - 1) // MOE_TM
    tile_end = jnp.cumsum(tiles_per_e)
    row_start = (tile_end - tiles_per_e) * MOE_TM
    half0 = row_start // MOE_HALF
    n_half = 2 * tiles_per_e
    last_real = (counts + MOE_HALF - 1) // MOE_HALF - 1
    unused = 2 * tile_end[-1] + jnp.arange(2 * N_EXPERTS, dtype=I32)
    zhalf = jnp.concatenate([half0 + last_real, half0 + n_half - 1, unused])
    zflag = jnp.concatenate([counts % MOE_HALF != 0, last_real < n_half - 1, unused < 2 * max_tiles])
    ztiles = jnp.stack([jnp.clip(zhalf, 0, 2 * max_tiles - 1), zflag.astype(I32)]).astype(I32)
    gu_chunks, dn_chunks = 3, 2
    row_start = row_start.astype(I32).reshape(1, N_EXPERTS)
    xs = _bucket_call(rp, rs, row_start, ztiles, hp, hs, max_tiles * MOE_TM)
    he = _grouped_call(_moe_schedule(counts, max_tiles, gu_chunks), xs, [w_gate, w_up], layer,
                       tn=w_gate.shape[3] // gu_chunks, out_dtype=BF16, swiglu=True, name="moe_gate_up")
    y = _grouped_call(_moe_schedule(counts, max_tiles, dn_chunks), he, [w_down], layer,
                      tn=w_down.shape[3] // dn_chunks, out_dtype=F32, swiglu=False, name="moe_down")
    return y, row_start


def kernel(x_prompt, x_sample, cache_a_kv0, cache_a_kv1, cache_a_kv2, state_b_h, state_b_conv, state_c_conv,
           c_prompt, c_sample, w_mod, b_mod, norm_g, w_in0, qk_g, b_conv_w, b_conv_b, b_wa, b_ba, b_wi, b_bi,
           b_lambda, w_out0, w_in1, c_conv_w, w_out1, w_router, b_router, w_gate, w_up, w_down):
    batch, seq, d = x_prompt.shape
    dec_batch = x_sample.shape[0]
    depth = w_mod.shape[0]
    tp = batch * seq
    width = b_lambda.shape[1]
    pad_s = SAMPLE_ROWS - dec_batch
    caches = (cache_a_kv0, cache_a_kv1, cache_a_kv2)

    def pad_rows(v):
        return jnp.pad(v, ((0, pad_s),) + ((0, 0),) * (v.ndim - 1))

    res_p = x_prompt.reshape(tp, d)
    res_s = pad_rows(x_sample.reshape(dec_batch, d))
    c_all = jnp.concatenate([c_prompt, jnp.zeros((SUBLANES - batch, d), F32), pad_rows(c_sample)], axis=0)
    mod = _mod_call(c_all, w_mod, b_mod)
    slopes = 2.0 ** (-8.0 * np.arange(1, A_HEADS + 1) / A_HEADS)
    slopes8 = jnp.asarray(np.repeat(slopes[:, None], HEAD_DIM, axis=1), dtype=F32)
    w_router_t = w_router.T

    new_kv_p = [[], [], []]
    new_kv_s = [[], [], []]
    bh_p, bconv_p, cconv_p, bh_s, bconv_s, cconv_s = [], [], [], [], [], []

    def mod_chunks(layer):
        m = mod[layer]
        return ([m[:batch, k * d:(k + 1) * d].reshape(batch, 1, d) for k in range(6)],
                [m[SUBLANES:, k * d:(k + 1) * d] for k in range(6)])

    h_p = h_s = None
    for layer in range(depth):
        li = layer // 2
        (sh1p, sc1p, g1p, sh2p, sc2p, g2p), (sh1s, sc1s, g1s, sh2s, sc2s, g2s) = mod_chunks(layer)
        if h_p is None:
            h_p = _norm_call(res_p, norm_g[layer, 0], sc1p, sh1p, per_row=False, tr=256, rows_per_mod=seq)
            h_s = _norm_call(res_s, norm_g[layer, 0], sc1s, sh1s, per_row=True, tr=SAMPLE_ROWS)

        if layer % 2 == 0:
            z_p, z_s = _dense_wide_call([h_p], h_s, w_in0, li, vmem=WIDE_VMEM_LIMIT)
            ya_p, kv_p = _attn_prompt_call(z_p, qk_g[li, 0], qk_g[li, 1], slopes8.reshape(A_HEADS, 1, HEAD_DIM),
                                           batch=batch, seq=seq)
            for g in range(N_DIL):
                new_kv_p[g].append(kv_p[g])
            yb_p, hl_p = _lru_prompt_call(z_p, b_conv_w[li], b_conv_b[li], b_wa[li], b_ba[li], b_wi[li],
                                          b_bi[li], b_lambda[li], batch=batch, seq=seq)
            bh_p.append(hl_p[:, SUBLANES - 1])
            bconv_p.append(z_p.reshape(batch, seq, -1)[:, seq - (B_CONV - 1):, A_QKV:A_QKV + width])

            ya_s, kvnew_s = _attn_sample_call(z_s, [c[li] for c in caches], qk_g[li, 0], qk_g[li, 1], slopes8,
                                              dec_batch=dec_batch)
            for g in range(N_DIL):
                new_kv_s[g].append(_cache_roll_call(caches[g][li], kvnew_s[:, g][:, None]))
            xb_s = z_s[:, A_QKV:A_QKV + width]
            gb_s = z_s[:, A_QKV + width:]
            buf = jnp.transpose(pad_rows(state_b_conv[li]), (1, 0, 2))
            yb_s, hnew_s = _lru_sample_call(xb_s, gb_s, buf, pad_rows(state_b_h[li]), b_conv_w[li], b_conv_b[li],
                                            b_wa[li], b_ba[li], b_wi[li], b_bi[li], b_lambda[li])
            bh_s.append(hnew_s[:dec_batch])
            bconv_s.append(jnp.concatenate([state_b_conv[li][:, 1:], xb_s[:dec_batch, None]], axis=1))

            mix_p = [ya_p, yb_p]
            mix_s = jnp.concatenate([pad_rows(ya_s.reshape(dec_batch, A_WIDTH)).astype(BF16), yb_s], axis=1)
            w_out = w_out0
        else:
            buf = jnp.transpose(pad_rows(state_c_conv[li]), (1, 0, 2))
            m_p, ut_p, mix_s, u_s = _proj_gconv_call(h_p, h_s, w_in1, li, c_conv_w[li], buf, batch=batch, seq=seq)
            mix_p = [m_p]
            cconv_p.append(ut_p[:, SUBLANES - (C_CONV - 1):])
            cconv_s.append(jnp.concatenate([state_c_conv[li][:, 1:], u_s[:dec_batch, None]], axis=1))
            w_out = w_out1

        res_p, res_s = _dense_wide_call(mix_p, mix_s, w_out, li, res=(res_p, g1p, res_s, g1s), rows_per_mod=seq,
                                        tm=512)

        hp, ep, wp, cnt = _norm_route_call(res_p, norm_g[layer, 1], sc2p, sh2p, w_router_t, b_router,
                                           jnp.zeros((N_EXPERTS, LANES), F32),
                                           per_row=False, tr=256, rows_per_mod=seq)
        hs, es, ws, cnt = _norm_route_call(res_s, norm_g[layer, 1], sc2s, sh2s, w_router_t, b_router, cnt,
                                           per_row=True, tr=SAMPLE_ROWS)
        y, row_start = _moe(hp, hs, ep, es, cnt[:, 0].astype(I32), w_gate, w_up, w_down, layer)
        if layer + 1 < depth:
            (nsh_p, nsc_p, *_), (nsh_s, nsc_s, *_) = mod_chunks(layer + 1)
            res_p, h_p = _combine_call(ep, row_start, res_p, g2p, wp, y, per_row=False, tr=256, rows_per_mod=seq,
                                       next_norm=(norm_g[layer + 1, 0], nsc_p, nsh_p))
            res_s, h_s = _combine_call(es, row_start, res_s, g2s, ws, y, per_row=True, tr=SAMPLE_ROWS,
                                       next_norm=(norm_g[layer + 1, 0], nsc_s, nsh_s))
        else:
            res_p = _combine_call(ep, row_start, res_p, g2p, wp, y, per_row=False, tr=256, rows_per_mod=seq)
            res_s = _combine_call(es, row_start, res_s, g2s, ws, y, per_row=True, tr=SAMPLE_ROWS)

    y_prompt = res_p.reshape(batch, seq, d)
    y_sample = res_s[:dec_batch].reshape(dec_batch, 1, d)
    st = lambda xs: jnp.stack(xs, axis=0)
    return (y_prompt, y_sample, st(new_kv_p[0]), st(new_kv_p[1]), st(new_kv_p[2]), st(bh_p), st(bconv_p),
            st(cconv_p), st(new_kv_s[0]), st(new_kv_s[1]), st(new_kv_s[2]), st(bh_s), st(bconv_s), st(cconv_s))
```

```python
import functools

import numpy as np
import jax
import jax.numpy as jnp
from jax import lax
from jax.experimental import pallas as pl
from jax.experimental.pallas import tpu as pltpu

F32 = jnp.float32
BF16 = jnp.bfloat16
I32 = jnp.int32
U32 = jnp.uint32

HEAD_DIM = 128
A_HEADS = 8
DILATIONS = ((128, 1), (512, 4), (2048, 16))
N_DIL = len(DILATIONS)
A_WIDTH = A_HEADS * HEAD_DIM
A_QKV = N_DIL * 3 * A_WIDTH
ATTN_SCALE = HEAD_DIM ** -0.5
QB = 128
B_BLOCKS = 8
B_CONV = 4
C_CONV = 3
LRU_C = 8.0
N_EXPERTS = 16
N_GROUPS = 4
EXP_PER_GROUP = N_EXPERTS // N_GROUPS
EPS = 1e-6
SAMPLE_ROWS = 16
SUBLANES = 8
LANES = 128
NEG_INF = float("-inf")

VMEM_LIMIT = 52 * 1024 * 1024
WIDE_VMEM_LIMIT = 58 * 1024 * 1024


def _params(sem, vmem=VMEM_LIMIT):
    return pltpu.CompilerParams(dimension_semantics=sem, vmem_limit_bytes=vmem)


def _mod_kernel(c_ref, w_ref, b_ref, o_ref):
    c = c_ref[...]
    cs = (c * jax.nn.sigmoid(c)).astype(BF16)
    o_ref[...] = jnp.dot(cs, w_ref[...].astype(BF16), preferred_element_type=F32) + b_ref[...]


def _mod_call(c_all, w_mod, b_mod):
    depth, d, n = w_mod.shape
    rows = c_all.shape[0]
    tn = 512
    return pl.pallas_call(
        _mod_kernel,
        grid=(depth, n // tn),
        in_specs=[
            pl.BlockSpec((rows, d), lambda l, j: (0, 0)),
            pl.BlockSpec((None, d, tn), lambda l, j: (l, 0, j)),
            pl.BlockSpec((None, 1, tn), lambda l, j: (l, 0, j)),
        ],
        out_specs=pl.BlockSpec((None, rows, tn), lambda l, j: (l, 0, j)),
        out_shape=jax.ShapeDtypeStruct((depth, rows, n), F32),
        compiler_params=_params(("arbitrary", "arbitrary")),
        name="mod_proj",
    )(c_all, w_mod, b_mod.reshape(depth, 1, n))


def _modnorm(x, g, sc, sh):
    h = x * lax.rsqrt(jnp.mean(x * x, axis=-1, keepdims=True) + EPS) * g
    return h * (1.0 + sc) + sh


def _norm_kernel(x_ref, g_ref, sc_ref, sh_ref, o_ref):
    o_ref[...] = _modnorm(x_ref[...], g_ref[...], sc_ref[...], sh_ref[...]).astype(o_ref.dtype)


def _route(logits_t, bias_ref):
    score = [jax.nn.sigmoid(logits_t[e:e + 1, :]) for e in range(N_EXPERTS)]
    sel = [score[e] + bias_ref[e] for e in range(N_EXPERTS)]
    gsum = []
    for g in range(N_GROUPS):
        a, b, c, d = sel[4 * g:4 * g + 4]
        hi1, lo1 = jnp.maximum(a, b), jnp.minimum(a, b)
        hi2, lo2 = jnp.maximum(c, d), jnp.minimum(c, d)
        top = jnp.maximum(hi1, hi2)
        second = jnp.maximum(jnp.minimum(hi1, hi2), jnp.maximum(lo1, lo2))
        gsum.append(top + second)
    best = gsum[0]
    grp = jnp.zeros_like(best, dtype=I32)
    for g in range(1, N_GROUPS):
        better = gsum[g] > best
        best = jnp.where(better, gsum[g], best)
        grp = jnp.where(better, g, grp)
    v1 = jnp.full_like(best, NEG_INF)
    e1 = jnp.zeros_like(grp)
    for e in range(N_EXPERTS):
        cand = jnp.where(grp == e // EXP_PER_GROUP, sel[e], NEG_INF)
        better = cand > v1
        v1 = jnp.where(better, cand, v1)
        e1 = jnp.where(better, e, e1)
    v2 = jnp.full_like(best, NEG_INF)
    e2 = jnp.zeros_like(grp)
    for e in range(N_EXPERTS):
        cand = jnp.where((grp == e // EXP_PER_GROUP) & (e1 != e), sel[e], NEG_INF)
        better = cand > v2
        v2 = jnp.where(better, cand, v2)
        e2 = jnp.where(better, e, e2)
    s1 = jnp.zeros_like(best)
    s2 = jnp.zeros_like(best)
    for e in range(N_EXPERTS):
        s1 = jnp.where(e1 == e, score[e], s1)
        s2 = jnp.where(e2 == e, score[e], s2)
    tot = s1 + s2
    return e1, e2, s1 / tot, s2 / tot


def _norm_route_kernel(bias_ref, x_ref, g_ref, sc_ref, sh_ref, wr_ref, cnt0_ref,
                       o_ref, e_ref, w_ref, cnt_ref, run_ref):
    @pl.when(pl.program_id(0) == 0)
    def _():
        run_ref[...] = cnt0_ref[...]

    h = _modnorm(x_ref[...], g_ref[...], sc_ref[...], sh_ref[...])
    o_ref[...] = h
    nt = (((1,), (1,)), ((), ()))
    logits_t = lax.dot_general(wr_ref[...].astype(BF16), h.astype(BF16), nt, preferred_element_type=F32)
    e1, e2, w1, w2 = _route(logits_t, bias_ref)
    n = e1.shape[1]
    expert = lax.broadcasted_iota(I32, (N_EXPERTS, n), 0)
    oh1 = (expert == e1).astype(F32)
    oh2 = (expert == e2).astype(F32)
    before = (lax.broadcasted_iota(I32, (n, n), 0) < lax.broadcasted_iota(I32, (n, n), 1)).astype(BF16)
    pre1 = jnp.dot(oh1.astype(BF16), before, preferred_element_type=F32)
    pre2 = jnp.dot(oh2.astype(BF16), before, preferred_element_type=F32)
    tot1 = jnp.sum(oh1, axis=1, keepdims=True)
    tot2 = jnp.sum(oh2, axis=1, keepdims=True)
    base = run_ref[...][:, :1]
    r1 = jnp.sum(oh1 * (base + pre1), axis=0, keepdims=True)
    r2 = jnp.sum(oh2 * (base + tot1 + pre2), axis=0, keepdims=True)
    run_ref[...] = run_ref[...] + (tot1 + tot2)
    cnt_ref[...] = run_ref[...]
    e_ref[...] = jnp.concatenate([e1, e2, r1.astype(I32), r2.astype(I32), jnp.zeros((SUBLANES - 4, n), I32)], axis=0)
    w_ref[...] = jnp.concatenate([w1, w2, jnp.zeros((SUBLANES - 2, n), F32)], axis=0)


def _mod_specs(per_row, tr, d, rows_per_mod):
    if per_row:
        spec = pl.BlockSpec((tr, d), lambda i, *_: (i, 0))
    else:
        spec = pl.BlockSpec((None, 1, d), lambda i, *_: (i * tr // rows_per_mod, 0, 0))
    return spec


def _norm_call(x, g, sc, sh, *, per_row, tr, rows_per_mod=None):
    rows, d = x.shape
    mspec = _mod_specs(per_row, tr, d, rows_per_mod)
    return pl.pallas_call(
        _norm_kernel,
        grid=(rows // tr,),
        in_specs=[pl.BlockSpec((tr, d), lambda i: (i, 0)),
                  pl.BlockSpec((1, d), lambda i: (0, 0)), mspec, mspec],
        out_specs=pl.BlockSpec((tr, d), lambda i: (i, 0)),
        out_shape=jax.ShapeDtypeStruct((rows, d), BF16),
        compiler_params=_params(("arbitrary",)),
        name="modnorm",
    )(x, g.reshape(1, d), sc, sh)


def _norm_route_call(x, g, sc, sh, w_router_t, b_router, cnt0, *, per_row, tr, rows_per_mod=None):
    rows, d = x.shape
    mspec = _mod_specs(per_row, tr, d, rows_per_mod)
    cspec = pl.BlockSpec((N_EXPERTS, LANES), lambda i, b: (0, 0))
    grid_spec = pltpu.PrefetchScalarGridSpec(
        num_scalar_prefetch=1,
        grid=(rows // tr,),
        in_specs=[pl.BlockSpec((tr, d), lambda i, b: (i, 0)),
                  pl.BlockSpec((1, d), lambda i, b: (0, 0)), mspec, mspec,
                  pl.BlockSpec((N_EXPERTS, d), lambda i, b: (0, 0)), cspec],
        out_specs=[pl.BlockSpec((tr, d), lambda i, b: (i, 0)),
                   pl.BlockSpec((SUBLANES, tr), lambda i, b: (0, i)),
                   pl.BlockSpec((SUBLANES, tr), lambda i, b: (0, i)), cspec],
        scratch_shapes=[pltpu.VMEM((N_EXPERTS, LANES), F32)],
    )
    return pl.pallas_call(
        _norm_route_kernel,
        grid_spec=grid_spec,
        out_shape=[jax.ShapeDtypeStruct((rows, d), F32),
                   jax.ShapeDtypeStruct((SUBLANES, rows), I32),
                   jax.ShapeDtypeStruct((SUBLANES, rows), F32),
                   jax.ShapeDtypeStruct((N_EXPERTS, LANES), F32)],
        compiler_params=_params(("arbitrary",)),
        name="modnorm_route",
    )(b_router, x, g.reshape(1, d), sc, sh, w_router_t, cnt0)


def _dense_wide_kernel(*refs, n_x, residual, li):
    xp_refs = refs[:n_x]
    if residual:
        (xs_ref, w_hbm, rp_ref, gp_ref, rs_ref, gs_ref, op_ref, os_ref, w_f32, w_b16, sem) = refs[n_x:]
    else:
        (xs_ref, w_hbm, op_ref, os_ref, w_f32, w_b16, sem) = refs[n_x:]
    j = pl.program_id(0)
    i = pl.program_id(1)
    tn = w_b16.shape[1]

    def w_copy(jj):
        col = pl.multiple_of(jj * tn, tn)
        return pltpu.make_async_copy(w_hbm.at[li, :, pl.ds(col, tn)], w_f32, sem)

    @pl.when((j == 0) & (i == 0))
    def _():
        w_copy(0).start()

    @pl.when(i == 0)
    def _():
        w_copy(j).wait()
        w_b16[...] = w_f32[...].astype(BF16)

        @pl.when(j + 1 < pl.num_programs(0))
        def _():
            w_copy(j + 1).start()

        acc_s = jnp.dot(xs_ref[...], w_b16[...], preferred_element_type=F32)
        if residual:
            acc_s = rs_ref[...] + gs_ref[...] * acc_s
        os_ref[...] = acc_s.astype(os_ref.dtype)

    acc = None
    k0 = 0
    for x_ref in xp_refs:
        kx = x_ref.shape[1]
        part = jnp.dot(x_ref[...], w_b16[k0:k0 + kx, :], preferred_element_type=F32)
        acc = part if acc is None else acc + part
        k0 += kx
    if residual:
        acc = rp_ref[...] + gp_ref[...] * acc
    op_ref[...] = acc.astype(op_ref.dtype)


def _dense_wide_call(xps, xs, w, li, *, res=None, rows_per_mod=None, out_dtype=F32, tm=1024, tn=1024,
                     vmem=VMEM_LIMIT):
    tp = xps[0].shape[0]
    k, n = w.shape[1:]
    rs = xs.shape[0]
    in_specs = [pl.BlockSpec((tm, x.shape[1]), lambda j, i: (i, 0)) for x in xps]
    in_specs += [pl.BlockSpec((rs, k), lambda j, i: (0, 0)), pl.BlockSpec(memory_space=pl.ANY)]
    args = list(xps) + [xs, w]
    if res is not None:
        in_specs += [pl.BlockSpec((tm, tn), lambda j, i: (i, j)),
                     pl.BlockSpec((None, 1, tn), lambda j, i: (i * tm // rows_per_mod, 0, j)),
                     pl.BlockSpec((rs, tn), lambda j, i: (0, j)),
                     pl.BlockSpec((rs, tn), lambda j, i: (0, j))]
        args += list(res)
    return pl.pallas_call(
        functools.partial(_dense_wide_kernel, n_x=len(xps), residual=res is not None, li=li),
        grid=(n // tn, tp // tm),
        in_specs=in_specs,
        out_specs=[pl.BlockSpec((tm, tn), lambda j, i: (i, j)),
                   pl.BlockSpec((rs, tn), lambda j, i: (0, j))],
        out_shape=[jax.ShapeDtypeStruct((tp, n), out_dtype), jax.ShapeDtypeStruct((rs, n), out_dtype)],
        scratch_shapes=[pltpu.VMEM((k, tn), F32), pltpu.VMEM((k, tn), BF16), pltpu.SemaphoreType.DMA],
        compiler_params=_params(("arbitrary", "arbitrary"), vmem),
        name="dense_wide",
    )(*args)


def _rms_rows(x, g):
    return x * lax.rsqrt(jnp.mean(x * x, axis=-1, keepdims=True) + EPS) * g


def _attn_prompt_kernel(q0, k0, v0, q1, k1, v1, q2, k2, v2, gq_ref, gk_ref, slope_ref,
                        ya_ref, kv0, kv1, kv2, qn_s, m_s, l_s, acc_s, kn0, kn1, kn2, kv_sem, *, seq):
    qkv = ((q0, k0, v0, kn0), (q1, k1, v1, kn1), (q2, k2, v2, kn2))
    kv_out = (kv0, kv1, kv2)
    b_idx = pl.program_id(0)
    h_idx = pl.program_id(1)

    def kv_copy(g, c):
        keep = kv_out[g].shape[1]
        src = (qkv[g][3], qkv[g][2])[c].at[pl.ds(seq - keep, keep)]
        dst = kv_out[g].at[b_idx, pl.ds(0, keep), c, h_idx, pl.ds(0, HEAD_DIM)]
        return pltpu.make_async_copy(src, dst, kv_sem.at[2 * g + c])

    slope = slope_ref[...]
    row = lax.broadcasted_iota(I32, (QB, 2 * QB), 0)
    col = lax.broadcasted_iota(I32, (QB, 2 * QB), 1)
    rel_prev = row + QB - col
    rel_first = (row - col)[:, :QB]
    nt = (((1,), (1,)), ((), ()))

    for g, (window, dil) in enumerate(DILATIONS):
        q_ref, k_ref, v_ref, kn_ref = qkv[g]
        nk = window // dil
        n_blk = seq // dil // QB
        qn_s[...] = _rms_rows(q_ref[...], gq_ref[g])
        kn_ref[...] = _rms_rows(k_ref[...], gk_ref[g])
        kv_copy(g, 0).start()
        kv_copy(g, 1).start()
        sl = slope[:, :1] * float(dil)
        bias_prev = jnp.where((rel_prev >= 0) & (rel_prev <= nk), -sl * rel_prev.astype(F32), NEG_INF)
        bias_first = jnp.where(rel_first >= 0, -sl * rel_first.astype(F32), NEG_INF)

        def rows(start, size, dil=dil):
            return pl.ds(start, size) if dil == 1 else pl.ds(start, size, stride=dil)

        def block(qstart, kstart, first, g=g, kn_ref=kn_ref, v_ref=v_ref, rows=rows,
                  bias_prev=bias_prev, bias_first=bias_first):
            nkeys = QB if first else 2 * QB
            qb = qn_s[rows(qstart, QB), :].astype(BF16)
            kb = kn_ref[rows(kstart, nkeys), :].astype(BF16)
            vb = v_ref[rows(kstart, nkeys), :].astype(BF16)
            s = lax.dot_general(qb, kb, nt, preferred_element_type=F32) * ATTN_SCALE
            s = s + (bias_first if first else bias_prev)
            m_blk = jnp.max(s, axis=-1, keepdims=True)
            p = jnp.exp(s - m_blk)
            l_blk = jnp.sum(p, axis=-1, keepdims=True)
            o_blk = jnp.dot(p.astype(BF16), vb, preferred_element_type=F32)
            qrows = rows(qstart, QB)
            if g == 0:
                m_s[qrows, :] = jnp.broadcast_to(m_blk, (QB, HEAD_DIM))
                l_s[qrows, :] = jnp.broadcast_to(l_blk, (QB, HEAD_DIM))
                acc_s[qrows, :] = o_blk
            else:
                m_old = m_s[qrows, :]
                m_new = jnp.maximum(m_old, m_blk)
                a_old = jnp.exp(m_old - m_new)
                a_blk = jnp.exp(m_blk - m_new)
                m_s[qrows, :] = m_new
                l_s[qrows, :] = l_s[qrows, :] * a_old + l_blk * a_blk
                acc_s[qrows, :] = acc_s[qrows, :] * a_old + o_blk * a_blk

        if dil == 1:
            block(0, 0, True)

            def body(n, carry, block=block):
                qstart = pl.multiple_of(n * QB, QB)
                block(qstart, pl.multiple_of(qstart - QB, QB), False)
                return carry

            lax.fori_loop(1, n_blk, body, 0, unroll=3)
        else:
            def body(r, carry, block=block, n_blk=n_blk, dil=dil):
                block(r, r, True)
                for n in range(1, n_blk):
                    block(r + n * QB * dil, r + (n - 1) * QB * dil, False)
                return carry

            lax.fori_loop(0, dil, body, 0, unroll=4 if n_blk == 1 else 1)

    ya_ref[...] = (acc_s[...] / l_s[...]).astype(ya_ref.dtype)
    for g in range(N_DIL):
        kv_copy(g, 0).wait()
        kv_copy(g, 1).wait()


def _attn_prompt_call(z, gq, gk, slopes, *, batch, seq):
    t = batch * seq
    kv_shapes = [jax.ShapeDtypeStruct((batch, min(window, seq), 2, A_HEADS, HEAD_DIM), F32)
                 for window, _ in DILATIONS]

    def col(g, c):
        return lambda b, h: (b, g * 3 * A_HEADS + c * A_HEADS + h)

    in_specs = []
    for g in range(N_DIL):
        for c in range(3):
            in_specs.append(pl.BlockSpec((seq, HEAD_DIM), col(g, c)))
    gspec = pl.BlockSpec((N_DIL, None, 1, HEAD_DIM), lambda b, h: (0, h, 0, 0))
    in_specs += [gspec, gspec, pl.BlockSpec((None, 1, HEAD_DIM), lambda b, h: (h, 0, 0))]
    outs = pl.pallas_call(
        functools.partial(_attn_prompt_kernel, seq=seq),
        grid=(batch, A_HEADS),
        in_specs=in_specs,
        out_specs=[pl.BlockSpec((seq, HEAD_DIM), lambda b, h: (b, h))] + [pl.BlockSpec(memory_space=pl.ANY)] * 3,
        out_shape=[jax.ShapeDtypeStruct((t, A_WIDTH), BF16)] + kv_shapes,
        scratch_shapes=[pltpu.VMEM((seq, HEAD_DIM), F32)] * 7 + [pltpu.SemaphoreType.DMA((2 * N_DIL,))],
        compiler_params=_params(("arbitrary", "arbitrary")),
        name="attn_prompt",
    )(*([z] * 9), gq.reshape(N_DIL, A_HEADS, 1, HEAD_DIM), gk.reshape(N_DIL, A_HEADS, 1, HEAD_DIM), slopes)
    return outs[0], outs[1:]


def _round_bf16(x):
    return x.astype(BF16).astype(F32)


def _attn_sample_kernel(z_ref, c0_ref, c1_ref, c2_ref, gq_ref, gk_ref, slope_ref, ya_ref, new_ref):
    caches = (c0_ref, c1_ref, c2_ref)
    slope = slope_ref[...][:, :1]
    nk = DILATIONS[0][0] // DILATIONS[0][1]
    steps = (nk - lax.broadcasted_iota(I32, (nk, 1, 1), 0)).astype(F32)
    m_run = l_run = acc = None
    for g, (window, dil) in enumerate(DILATIONS):
        base = g * 3 * A_HEADS
        q = _rms_rows(z_ref[base:base + A_HEADS, :], gq_ref[g])
        k = _rms_rows(z_ref[base + A_HEADS:base + 2 * A_HEADS, :], gk_ref[g])
        v = z_ref[base + 2 * A_HEADS:base + 3 * A_HEADS, :]
        new_ref[g, 0] = k
        new_ref[g, 1] = v
        qr = _round_bf16(q)
        ck = _round_bf16(caches[g][:, 0])
        cv = _round_bf16(caches[g][:, 1])
        s_c = jnp.sum(ck * qr[None], axis=-1, keepdims=True) * ATTN_SCALE
        s_c = s_c - (slope * float(dil))[None] * steps
        s_n = jnp.sum(_round_bf16(k) * qr, axis=-1, keepdims=True) * ATTN_SCALE
        m_g = jnp.maximum(jnp.max(s_c, axis=0), s_n)
        p_c = jnp.exp(s_c - m_g[None])
        p_n = jnp.exp(s_n - m_g)
        l_g = jnp.sum(p_c, axis=0) + p_n
        o_g = jnp.sum(_round_bf16(p_c) * cv, axis=0) + _round_bf16(p_n) * _round_bf16(v)
        if g == 0:
            m_run, l_run, acc = m_g, l_g, o_g
        else:
            m_new = jnp.maximum(m_run, m_g)
            a_old = jnp.exp(m_run - m_new)
            a_g = jnp.exp(m_g - m_new)
            l_run = l_run * a_old + l_g * a_g
            acc = acc * a_old + o_g * a_g
            m_run = m_new
    ya_ref[...] = acc / l_run


def _attn_sample_call(z_s, caches, gq, gk, slopes8, *, dec_batch):
    n_colblk = z_s.shape[1] // HEAD_DIM
    z3 = z_s.reshape(z_s.shape[0], n_colblk, HEAD_DIM)
    in_specs = [pl.BlockSpec((None, n_colblk, HEAD_DIM), lambda b: (b, 0, 0))]
    args = [z3]
    for g, (window, dil) in enumerate(DILATIONS):
        nk = window // dil
        c = caches[g].reshape(dec_batch, nk, dil, 2, A_HEADS, HEAD_DIM)
        in_specs.append(pl.BlockSpec((None, nk, None, 2, A_HEADS, HEAD_DIM), lambda b: (b, 0, 0, 0, 0, 0)))
        args.append(c)
    full = lambda shape: pl.BlockSpec(shape, lambda b: (0,) * len(shape))
    in_specs += [full((N_DIL, A_HEADS, HEAD_DIM)), full((N_DIL, A_HEADS, HEAD_DIM)), full((A_HEADS, HEAD_DIM))]
    return pl.pallas_call(
        _attn_sample_kernel,
        grid=(dec_batch,),
        in_specs=in_specs,
        out_specs=[pl.BlockSpec((None, A_HEADS, HEAD_DIM), lambda b: (b, 0, 0)),
                   pl.BlockSpec((None, N_DIL, 2, A_HEADS, HEAD_DIM), lambda b: (b, 0, 0, 0, 0))],
        out_shape=[jax.ShapeDtypeStruct((dec_batch, A_HEADS, HEAD_DIM), F32),
                   jax.ShapeDtypeStruct((dec_batch, N_DIL, 2, A_HEADS, HEAD_DIM), F32)],
        compiler_params=_params(("arbitrary",)),
        name="attn_sample",
    )(*args, gq, gk, slopes8)


def _cache_roll_kernel(cur_ref, nxt_ref, new_ref, o_ref):
    last = pl.program_id(1) == pl.num_programs(1) - 1
    rows = cur_ref.shape[0]
    o_ref[0:rows - 1] = cur_ref[1:rows]
    o_ref[rows - 1:rows] = jnp.where(last, new_ref[...], nxt_ref[...])


def _cache_roll_call(cache, new, *, rows=256):
    b, w = cache.shape[:2]
    rows = min(rows, w)
    tail = cache.shape[2:]
    n_chunks = w // rows
    zeros = (0,) * len(tail)
    return pl.pallas_call(
        _cache_roll_kernel,
        grid=(b, n_chunks),
        in_specs=[pl.BlockSpec((None, rows) + tail, lambda i, j: (i, j) + zeros),
                  pl.BlockSpec((None, 1) + tail, lambda i, j: (i, jnp.minimum((j + 1) * rows, w - 1)) + zeros),
                  pl.BlockSpec((None, 1) + tail, lambda i, j: (i, 0) + zeros)],
        out_specs=pl.BlockSpec((None, rows) + tail, lambda i, j: (i, j) + zeros),
        out_shape=jax.ShapeDtypeStruct(cache.shape, cache.dtype),
        compiler_params=_params(("arbitrary", "arbitrary")),
        name="cache_roll",
    )(cache, cache, new)


def _softplus(x):
    return jnp.maximum(x, 0.0) + jnp.log1p(jnp.exp(-jnp.abs(x)))


def _lru_coeffs(xc, wa, ba, wi, bi, lam):
    xcb = xc.astype(BF16)
    r = jax.nn.sigmoid(jnp.dot(xcb, wa.astype(BF16), preferred_element_type=F32) + ba)
    i = jax.nn.sigmoid(jnp.dot(xcb, wi.astype(BF16), preferred_element_type=F32) + bi)
    log_a = -LRU_C * r * _softplus(-lam)
    a = jnp.exp(log_a)
    t = jnp.tanh(log_a)
    one_minus_a2 = -2.0 * t / (1.0 - t)
    return a, jnp.sqrt(one_minus_a2) * (i * xc)


def _lru_prompt_kernel(x_ref, gate_ref, cw_ref, cb_ref, wa_ref, ba_ref, wi_ref, bi_ref, lam_ref,
                       y_ref, hl_ref, pad_a, pad_b, *, seq):
    blk = x_ref.shape[1]
    off = SUBLANES
    x = x_ref[...]
    pad_a[0:off, :] = jnp.zeros((off, blk), F32)
    pad_a[off:off + seq, :] = x
    xc = pad_a[off - 3:off - 3 + seq, :] * cw_ref[0:1, :]
    xc = xc + pad_a[off - 2:off - 2 + seq, :] * cw_ref[1:2, :]
    xc = xc + pad_a[off - 1:off - 1 + seq, :] * cw_ref[2:3, :]
    xc = xc + x * cw_ref[3:4, :] + cb_ref[...]
    a, b = _lru_coeffs(xc, wa_ref[...], ba_ref[...], wi_ref[...], bi_ref[...], lam_ref[...])

    tpos = lax.broadcasted_iota(I32, (seq, blk), 0) & (SUBLANES - 1)
    pad_b[0:off, :] = jnp.zeros((off, blk), F32)
    for o in (1, 2, 4):
        pad_a[off:off + seq, :] = a
        pad_b[off:off + seq, :] = b
        keep = tpos >= o
        a_prev = jnp.where(keep, pad_a[off - o:off - o + seq, :], 1.0)
        b_prev = jnp.where(keep, pad_b[off - o:off - o + seq, :], 0.0)
        b = a * b_prev + b
        a = a * a_prev
    pad_a[off:off + seq, :] = a
    pad_b[off:off + seq, :] = b

    def tile(i, h_prev):
        r0 = pl.multiple_of(off + i * SUBLANES, SUBLANES)
        h = pad_b[pl.ds(r0, SUBLANES), :] + pad_a[pl.ds(r0, SUBLANES), :] * h_prev
        pad_b[pl.ds(r0, SUBLANES), :] = h
        return jnp.broadcast_to(h[SUBLANES - 1:SUBLANES, :], (SUBLANES, blk))

    lax.fori_loop(0, seq // SUBLANES, tile, jnp.zeros((SUBLANES, blk), F32))
    h = pad_b[off:off + seq, :]
    hl_ref[...] = h[seq - SUBLANES:, :]
    y_ref[...] = (h * jax.nn.gelu(gate_ref[...])).astype(y_ref.dtype)


def _lru_prompt_call(z, cw, cb, wa, ba, wi, bi, lam, *, batch, seq):
    t = batch * seq
    nb, blk, _ = wa.shape
    width = nb * blk
    x0 = A_QKV // blk
    g0 = (A_QKV + width) // blk
    vec = lambda v: v.reshape(1, width)
    vspec = pl.BlockSpec((1, blk), lambda b, n: (0, n))
    return pl.pallas_call(
        functools.partial(_lru_prompt_kernel, seq=seq),
        grid=(batch, nb),
        in_specs=[pl.BlockSpec((seq, blk), lambda b, n: (b, x0 + n)),
                  pl.BlockSpec((seq, blk), lambda b, n: (b, g0 + n)),
                  pl.BlockSpec((B_CONV, blk), lambda b, n: (0, n)), vspec,
                  pl.BlockSpec((None, blk, blk), lambda b, n: (n, 0, 0)), vspec,
                  pl.BlockSpec((None, blk, blk), lambda b, n: (n, 0, 0)), vspec, vspec],
        out_specs=[pl.BlockSpec((seq, blk), lambda b, n: (b, n)),
                   pl.BlockSpec((None, SUBLANES, blk), lambda b, n: (b, 0, n))],
        out_shape=[jax.ShapeDtypeStruct((t, width), BF16),
                   jax.ShapeDtypeStruct((batch, SUBLANES, width), F32)],
        scratch_shapes=[pltpu.VMEM((seq + SUBLANES, blk), F32)] * 2,
        compiler_params=_params(("arbitrary", "arbitrary")),
        name="rglru_prompt",
    )(z, z, cw, vec(cb), wa, vec(ba), wi, vec(bi), vec(lam))


def _lru_sample_kernel(x_ref, gate_ref, buf_ref, h0_ref, cw_ref, cb_ref, wa_ref, ba_ref, wi_ref, bi_ref,
                       lam_ref, y_ref, h_ref):
    nb, blk, _ = wa_ref.shape
    for n in range(nb):
        cs = slice(n * blk, (n + 1) * blk)
        x = x_ref[:, cs]
        xc = buf_ref[0, :, cs] * cw_ref[0:1, cs]
        xc = xc + buf_ref[1, :, cs] * cw_ref[1:2, cs]
        xc = xc + buf_ref[2, :, cs] * cw_ref[2:3, cs]
        xc = xc + x * cw_ref[3:4, cs] + cb_ref[:, cs]
        a, b = _lru_coeffs(xc, wa_ref[n], ba_ref[:, cs], wi_ref[n], bi_ref[:, cs], lam_ref[:, cs])
        h = b + a * h0_ref[:, cs]
        h_ref[:, cs] = h
        y_ref[:, cs] = (h * jax.nn.gelu(gate_ref[:, cs])).astype(y_ref.dtype)


def _lru_sample_call(x, gate, buf, h0, cw, cb, wa, ba, wi, bi, lam):
    rows, width = x.shape
    vec = lambda v: v.reshape(1, width)
    return pl.pallas_call(
        _lru_sample_kernel,
        out_shape=[jax.ShapeDtypeStruct((rows, width), BF16), jax.ShapeDtypeStruct((rows, width), F32)],
        compiler_params=pltpu.CompilerParams(vmem_limit_bytes=VMEM_LIMIT),
        name="rglru_sample",
    )(x, gate, buf, h0, cw, vec(cb), wa, vec(ba), wi, vec(bi), vec(lam))


def _proj_gconv_kernel(xp_ref, xs_ref, w_hbm, cw_ref, buf_ref, m_ref, ut_ref, ms_ref, us_ref,
                       w_f32, w_cat, pad, sem, *, li, tiles_per_seq):
    j = pl.program_id(0)
    i = pl.program_id(1)
    tm = xp_ref.shape[0]
    blk = cw_ref.shape[1]
    n_piece = 3
    d_model = w_hbm.shape[2] // n_piece

    def w_copy(k, jj):
        col = pl.multiple_of(k * d_model + jj * blk, blk)
        return pltpu.make_async_copy(w_hbm.at[li, :, pl.ds(col, blk)], w_f32.at[k], sem.at[k])

    @pl.when((j == 0) & (i == 0))
    def _():
        for k in range(n_piece):
            w_copy(k, 0).start()

    @pl.when(i == 0)
    def _():
        for k in range(n_piece):
            w_copy(k, j).wait()
            w_cat[:, k * blk:(k + 1) * blk] = w_f32[k].astype(BF16)

        @pl.when(j + 1 < pl.num_programs(0))
        def _():
            for k in range(n_piece):
                w_copy(k, j + 1).start()

        acc_s = jnp.dot(xs_ref[...], w_cat[...], preferred_element_type=F32)
        u_s = acc_s[:, blk:2 * blk] * acc_s[:, 2 * blk:]
        uc_s = buf_ref[0] * cw_ref[0:1, :] + buf_ref[1] * cw_ref[1:2, :]
        uc_s = uc_s + u_s * cw_ref[2:3, :]
        ms_ref[...] = (acc_s[:, :blk] * uc_s).astype(ms_ref.dtype)
        us_ref[...] = u_s

    @pl.when(i % tiles_per_seq == 0)
    def _():
        pad[0:SUBLANES, :] = jnp.zeros((SUBLANES, blk), F32)

    acc = jnp.dot(xp_ref[...], w_cat[...], preferred_element_type=F32)
    u = acc[:, blk:2 * blk] * acc[:, 2 * blk:]
    off = SUBLANES
    pad[off:off + tm, :] = u
    uc = pad[off - 2:off - 2 + tm, :] * cw_ref[0:1, :]
    uc = uc + pad[off - 1:off - 1 + tm, :] * cw_ref[1:2, :]
    uc = uc + u * cw_ref[2:3, :]
    m_ref[...] = (acc[:, :blk] * uc).astype(m_ref.dtype)
    tail = u[tm - SUBLANES:, :]
    ut_ref[...] = tail
    pad[0:SUBLANES, :] = tail


def _proj_gconv_call(xp, xs, w, li, cw, buf_s, *, batch, seq, tm=1024, blk=256):
    t, k = xp.shape
    rs = xs.shape[0]
    d = w.shape[2] // 3
    nblk = d // blk
    tiles_per_seq = seq // tm
    return pl.pallas_call(
        functools.partial(_proj_gconv_kernel, li=li, tiles_per_seq=tiles_per_seq),
        grid=(nblk, t // tm),
        in_specs=[pl.BlockSpec((tm, k), lambda j, i: (i, 0)),
                  pl.BlockSpec((rs, k), lambda j, i: (0, 0)),
                  pl.BlockSpec(memory_space=pl.ANY),
                  pl.BlockSpec((C_CONV, blk), lambda j, i: (0, j)),
                  pl.BlockSpec((C_CONV - 1, rs, blk), lambda j, i: (0, 0, j))],
        out_specs=[pl.BlockSpec((tm, blk), lambda j, i: (i, j)),
                   pl.BlockSpec((None, SUBLANES, blk), lambda j, i: (i // tiles_per_seq, 0, j)),
                   pl.BlockSpec((rs, blk), lambda j, i: (0, j)),
                   pl.BlockSpec((rs, blk), lambda j, i: (0, j))],
        out_shape=[jax.ShapeDtypeStruct((t, d), BF16), jax.ShapeDtypeStruct((batch, SUBLANES, d), F32),
                   jax.ShapeDtypeStruct((rs, d), BF16), jax.ShapeDtypeStruct((rs, d), F32)],
        scratch_shapes=[pltpu.VMEM((3, k, blk), F32), pltpu.VMEM((k, 3 * blk), BF16),
                        pltpu.VMEM((tm + SUBLANES, blk), F32), pltpu.SemaphoreType.DMA((3,))],
        compiler_params=_params(("arbitrary", "arbitrary")),
        name="proj_gconv",
    )(xp, xs, w, cw, buf_s)


MOE_TM = 512
MOE_HALF = MOE_TM // 2


def _bucket_row(route_ref, start_ref, k, t):
    return start_ref[0, route_ref[k, t]] + route_ref[2 + k, t]


def _bucket_kernel(rp_ref, rs_ref, start_ref, ztile_ref, hp_ref, hs_ref, xs_ref, zero_buf, sem, zsem):
    step = pl.program_id(0)
    tr = hp_ref.shape[0]
    n_s = rs_ref.shape[1]

    def copy_p(t, k):
        row = _bucket_row(rp_ref, start_ref, k, t)
        return pltpu.make_async_copy(hp_ref.at[pl.ds(t, 1)], xs_ref.at[pl.ds(row, 1)], sem)

    def copy_s(t, k):
        row = _bucket_row(rs_ref, start_ref, k, t)
        return pltpu.make_async_copy(hs_ref.at[pl.ds(t, 1)], xs_ref.at[pl.ds(row, 1)], sem)

    def copy_z(j):
        start = pl.multiple_of(ztile_ref[0, j] * MOE_HALF, MOE_HALF)
        return pltpu.make_async_copy(zero_buf, xs_ref.at[pl.ds(start, MOE_HALF)], zsem)

    @pl.when(step == 0)
    def _():
        zero_buf[...] = jnp.zeros_like(zero_buf)
        for j in range(ztile_ref.shape[1]):
            pl.when(ztile_ref[1, j] == 1)(lambda j=j: copy_z(j).start())
        for j in range(ztile_ref.shape[1]):
            pl.when(ztile_ref[1, j] == 1)(lambda j=j: copy_z(j).wait())
        for t in range(n_s):
            copy_s(t, 0).start()
            copy_s(t, 1).start()
        for t in range(n_s):
            copy_s(t, 0).wait()
            copy_s(t, 1).wait()

    def start_p(i, c):
        for j in range(SUBLANES):
            copy_p(i * SUBLANES + j, 0).start(priority=0)
            copy_p(i * SUBLANES + j, 1).start(priority=1)
        return c

    def wait_p(t, c):
        copy_p(t, 0).wait()
        copy_p(t, 1).wait()
        return c

    lax.fori_loop(0, tr // SUBLANES, start_p, 0)
    lax.fori_loop(0, tr, wait_p, 0)


def _bucket_call(rp, rs, row_start, ztiles, hp, hs, p_rows, *, tr=256):
    tp, dh = hp.shape
    n_s = rs.shape[1]
    smem = lambda shape, imap: pl.BlockSpec(shape, imap, memory_space=pltpu.SMEM)
    return pl.pallas_call(
        _bucket_kernel,
        grid=(tp // tr,),
        in_specs=[smem((SUBLANES, tr), lambda i: (0, i)), smem((SUBLANES, n_s), lambda i: (0, 0)),
                  smem(row_start.shape, lambda i: (0, 0)), smem(ztiles.shape, lambda i: (0, 0)),
                  pl.BlockSpec((tr, dh), lambda i: (i, 0)),
                  pl.BlockSpec((n_s, dh), lambda i: (0, 0))],
        out_specs=pl.BlockSpec(memory_space=pl.ANY),
        out_shape=jax.ShapeDtypeStruct((p_rows, dh), hp.dtype),
        scratch_shapes=[pltpu.VMEM((MOE_HALF, dh), hp.dtype), pltpu.SemaphoreType.DMA, pltpu.SemaphoreType.DMA],
        compiler_params=_params(("arbitrary",)),
        name="moe_bucket",
    )(rp, rs, row_start, ztiles, hp, hs)


(S_XTILE, S_OTILE, S_EXPERT, S_WCHUNK, S_OCHUNK, S_FIRST, S_VALID, S_FULL,
 S_HAS_NEXT, S_NEXT_EXPERT, S_NEXT_CHUNK) = range(11)


def _grouped_kernel(s_ref, x_ref, *refs, n_w, layer, tn, swiglu):
    w_hbm = refs[:n_w]
    o_ref = refs[n_w]
    w_f32 = refs[n_w + 1:2 * n_w + 1]
    w_b16 = refs[2 * n_w + 1]
    sem = refs[2 * n_w + 2]
    it = pl.program_id(0)

    def w_copy(k, expert, chunk):
        col = pl.multiple_of(chunk * tn, LANES)
        return pltpu.make_async_copy(w_hbm[k].at[layer, expert, :, pl.ds(col, tn)], w_f32[k], sem.at[k])

    @pl.when(it == 0)
    def _():
        for k in range(n_w):
            w_copy(k, s_ref[S_EXPERT, 0], s_ref[S_WCHUNK, 0]).start()

    @pl.when(s_ref[S_FIRST, it] == 1)
    def _():
        for k in range(n_w):
            w_copy(k, s_ref[S_EXPERT, it], s_ref[S_WCHUNK, it]).wait()
            w_b16[:, k * tn:(k + 1) * tn] = w_f32[k][...].astype(BF16)

        @pl.when(s_ref[S_HAS_NEXT, it] == 1)
        def _():
            for k in range(n_w):
                w_copy(k, s_ref[S_NEXT_EXPERT, it], s_ref[S_NEXT_CHUNK, it]).start()

    def compute(rows):
        acc = jnp.dot(x_ref[0:rows, :].astype(BF16), w_b16[...], preferred_element_type=F32)
        if swiglu:
            gate = acc[:, :tn]
            acc = gate * jax.nn.sigmoid(gate) * acc[:, tn:]
        o_ref[0:rows, :] = acc.astype(o_ref.dtype)

    valid = s_ref[S_VALID, it] == 1
    full = s_ref[S_FULL, it] == 1

    @pl.when(valid & full)
    def _():
        compute(MOE_TM)

    @pl.when(valid & jnp.logical_not(full))
    def _():
        compute(MOE_HALF)
        o_ref[MOE_HALF:, :] = jnp.zeros((MOE_TM - MOE_HALF, tn), o_ref.dtype)

    @pl.when(jnp.logical_not(valid))
    def _():
        o_ref[...] = jnp.zeros_like(o_ref)


def _grouped_call(sched, x, weights, layer, *, tn, out_dtype, swiglu, name):
    p, k = x.shape
    n = weights[0].shape[3]
    n_w = len(weights)
    n_items = sched.shape[1]
    grid_spec = pltpu.PrefetchScalarGridSpec(
        num_scalar_prefetch=1,
        grid=(n_items,),
        in_specs=[pl.BlockSpec((MOE_TM, k), lambda it, s: (s[S_XTILE, it], 0))]
                 + [pl.BlockSpec(memory_space=pl.ANY)] * n_w,
        out_specs=pl.BlockSpec((MOE_TM, tn), lambda it, s: (s[S_OTILE, it], s[S_OCHUNK, it])),
        scratch_shapes=[pltpu.VMEM((k, tn), F32)] * n_w + [pltpu.VMEM((k, n_w * tn), BF16),
                                                           pltpu.SemaphoreType.DMA((n_w,))],
    )
    return pl.pallas_call(
        functools.partial(_grouped_kernel, n_w=n_w, layer=layer, tn=tn, swiglu=swiglu),
        grid_spec=grid_spec,
        out_shape=jax.ShapeDtypeStruct((p, n), out_dtype),
        compiler_params=_params(("arbitrary",)),
        name=name,
    )(sched, x, *weights)


def _combine_kernel(route_ref, next_route_ref, start_ref, res_ref, gate_ref, w_ref, y_ref, *refs, tr, with_norm):
    if with_norm:
        g_ref, sc_ref, sh_ref, o_ref, h_ref, buf, sem = refs
    else:
        o_ref, buf, sem = refs
    step = pl.program_id(0)
    slot = step % 2

    def copy(table, s, t, k):
        row = _bucket_row(table, start_ref, k, t)
        return pltpu.make_async_copy(y_ref.at[pl.ds(row, 1)], buf.at[s, k, pl.ds(t, 1)], sem.at[s])

    def issue(table, s):
        def body(i, c):
            for j in range(SUBLANES):
                copy(table, s, i * SUBLANES + j, 0).start(priority=0)
                copy(table, s, i * SUBLANES + j, 1).start(priority=1)
            return c
        lax.fori_loop(0, tr // SUBLANES, body, 0)

    @pl.when(step == 0)
    def _():
        issue(route_ref, 0)

    @pl.when(step + 1 < pl.num_programs(0))
    def _():
        issue(next_route_ref, 1 - slot)

    def wait(t, c):
        copy(route_ref, slot, t, 0).wait()
        copy(route_ref, slot, t, 1).wait()
        return c

    lax.fori_loop(0, tr, wait, 0)
    w = jnp.transpose(w_ref[...])
    y = w[:, 0:1] * buf[slot, 0] + w[:, 1:2] * buf[slot, 1]
    new_res = res_ref[...] + gate_ref[...] * y
    o_ref[...] = new_res
    if with_norm:
        h_ref[...] = _modnorm(new_res, g_ref[...], sc_ref[...], sh_ref[...]).astype(h_ref.dtype)


def _combine_call(route, row_start, res, gate, w, y, *, per_row, tr, rows_per_mod=None, next_norm=None):
    rows, d = res.shape
    gspec = _mod_specs(per_row, tr, d, rows_per_mod)
    row_block = pl.BlockSpec((tr, d), lambda i: (i, 0))
    extra_in, extra_args, out_specs, out_shape = [], [], row_block, jax.ShapeDtypeStruct((rows, d), F32)
    if next_norm is not None:
        extra_in = [pl.BlockSpec((1, d), lambda i: (0, 0)), gspec, gspec]
        extra_args = [next_norm[0].reshape(1, d), next_norm[1], next_norm[2]]
        out_specs = [row_block, row_block]
        out_shape = [out_shape, jax.ShapeDtypeStruct((rows, d), BF16)]
    smem = lambda shape, imap: pl.BlockSpec(shape, imap, memory_space=pltpu.SMEM)
    n_tiles = rows // tr
    return pl.pallas_call(
        functools.partial(_combine_kernel, tr=tr, with_norm=next_norm is not None),
        grid=(n_tiles,),
        in_specs=[smem((SUBLANES, tr), lambda i: (0, i)),
                  smem((SUBLANES, tr), lambda i: (0, jnp.minimum(i + 1, n_tiles - 1))),
                  smem(row_start.shape, lambda i: (0, 0)),
                  row_block, gspec,
                  pl.BlockSpec((SUBLANES, tr), lambda i: (0, i)),
                  pl.BlockSpec(memory_space=pl.ANY)] + extra_in,
        out_specs=out_specs,
        out_shape=out_shape,
        scratch_shapes=[pltpu.VMEM((2, 2, tr, d), F32), pltpu.SemaphoreType.DMA((2,))],
        compiler_params=_params(("arbitrary",)),
        name="moe_combine",
    )(route, route, row_start, res, gate, w, y, *extra_args)


def _moe_schedule(counts, max_tiles, n_chunks):
    e_ids = jnp.arange(N_EXPERTS, dtype=I32)
    tiles_per_e = (counts + MOE_TM - 1) // MOE_TM
    tile_end = jnp.cumsum(tiles_per_e)
    tile_start = tile_end - tiles_per_e
    used = tile_end[-1]
    n_valid = used * n_chunks
    nonempty = tiles_per_e > 0
    later = (e_ids[None, :] > e_ids[:, None]) & nonempty[None, :]
    next_e = jnp.min(jnp.where(later, e_ids[None, :], N_EXPERTS - 1), axis=1)
    last_e = jnp.max(jnp.where(nonempty, e_ids, 0))
    it = jnp.arange(n_chunks * max_tiles, dtype=I32)
    ex = jnp.minimum(jnp.sum((it[:, None] >= (tile_end * n_chunks)[None, :]).astype(I32), axis=1), N_EXPERTS - 1)
    onehot = (ex[:, None] == e_ids[None, :]).astype(I32)
    look = lambda v: jnp.sum(onehot * v.astype(I32)[None, :], axis=1)
    t_e = jnp.maximum(look(tiles_per_e), 1)
    t0 = look(tile_start)
    local = it - t0 * n_chunks
    chunk = local // t_e
    idx = local % t_e
    valid = it < n_valid
    first = (idx == 0) & valid
    full = look(counts) - idx * MOE_TM > MOE_HALF
    has_next = first & (it + t_e < n_valid)
    last_chunk = chunk == n_chunks - 1
    next_expert = jnp.where(last_chunk, look(next_e), ex)
    next_chunk = jnp.where(last_chunk, 0, chunk + 1)
    q = it - n_valid
    xtile = jnp.where(valid, t0 + idx, used - 1)
    otile = jnp.where(valid, t0 + idx, used + q // n_chunks)
    ochunk = jnp.where(valid, chunk, q % n_chunks)
    return jnp.stack([xtile, otile, jnp.where(valid, ex, last_e), jnp.where(valid, chunk, n_chunks - 1), ochunk,
                      first, valid, full, has_next, next_expert, next_chunk]).astype(I32)


def _moe(hp, hs, rp, rs, counts, w_gate, w_up, w_down, layer):
    n_assign = 2 * (hp.shape[0] + hs.shape[0])
    max_tiles = (n_assign + N_EXPERTS * (MOE_TM - 1)) // MOE_TM
    tiles_per_e = (counts + MOE_TM - 1) // MOE_TM
    tile_end = jnp.cumsum(tiles_per_e)
    row_start = (tile_end - tiles_per_e) * MOE_TM
    half0 = row_start // MOE_HALF
    n_half = 2 * tiles_per_e
    last_real = (counts + MOE_HALF - 1) // MOE_HALF - 1
    unused = 2 * tile_end[-1] + jnp.arange(2 * N_EXPERTS, dtype=I32)
    zhalf = jnp.concatenate([half0 + last_real, half0 + n_half - 1, unused])
    zflag = jnp.concatenate([counts % MOE_HALF != 0, last_real < n_half - 1, unused < 2 * max_tiles])
    ztiles = jnp.stack([jnp.clip(zhalf, 0, 2 * max_tiles - 1), zflag.astype(I32)]).astype(I32)
    gu_chunks, dn_chunks = 3, 2
    row_start = row_start.astype(I32).reshape(1, N_EXPERTS)
    xs = _bucket_call(rp, rs, row_start, ztiles, hp, hs, max_tiles * MOE_TM)
    he = _grouped_call(_moe_schedule(counts, max_tiles, gu_chunks), xs, [w_gate, w_up], layer,
                       tn=w_gate.shape[3] // gu_chunks, out_dtype=BF16, swiglu=True, name="moe_gate_up")
    y = _grouped_call(_moe_schedule(counts, max_tiles, dn_chunks), he, [w_down], layer,
                      tn=w_down.shape[3] // dn_chunks, out_dtype=F32, swiglu=False, name="moe_down")
    return y, row_start


def kernel(x_prompt, x_sample, cache_a_kv0, cache_a_kv1, cache_a_kv2, state_b_h, state_b_conv, state_c_conv,
           c_prompt, c_sample, w_mod, b_mod, norm_g, w_in0, qk_g, b_conv_w, b_conv_b, b_wa, b_ba, b_wi, b_bi,
           b_lambda, w_out0, w_in1, c_conv_w, w_out1, w_router, b_router, w_gate, w_up, w_down):
    batch, seq, d = x_prompt.shape
    dec_batch = x_sample.shape[0]
    depth = w_mod.shape[0]
    tp = batch * seq
    width = b_lambda.shape[1]
    pad_s = SAMPLE_ROWS - dec_batch
    caches = (cache_a_kv0, cache_a_kv1, cache_a_kv2)

    def pad_rows(v):
        return jnp.pad(v, ((0, pad_s),) + ((0, 0),) * (v.ndim - 1))

    res_p = x_prompt.reshape(tp, d)
    res_s = pad_rows(x_sample.reshape(dec_batch, d))
    c_all = jnp.concatenate([c_prompt, jnp.zeros((SUBLANES - batch, d), F32), pad_rows(c_sample)], axis=0)
    mod = _mod_call(c_all, w_mod, b_mod)
    slopes = 2.0 ** (-8.0 * np.arange(1, A_HEADS + 1) / A_HEADS)
    slopes8 = jnp.asarray(np.repeat(slopes[:, None], HEAD_DIM, axis=1), dtype=F32)
    w_router_t = w_router.T

    new_kv_p = [[], [], []]
    new_kv_s = [[], [], []]
    bh_p, bconv_p, cconv_p, bh_s, bconv_s, cconv_s = [], [], [], [], [], []

    def mod_chunks(layer):
        m = mod[layer]
        return ([m[:batch, k * d:(k + 1) * d].reshape(batch, 1, d) for k in range(6)],
                [m[SUBLANES:, k * d:(k + 1) * d] for k in range(6)])

    h_p = h_s = None
    for layer in range(depth):
        li = layer // 2
        (sh1p, sc1p, g1p, sh2p, sc2p, g2p), (sh1s, sc1s, g1s, sh2s, sc2s, g2s) = mod_chunks(layer)
        if h_p is None:
            h_p = _norm_call(res_p, norm_g[layer, 0], sc1p, sh1p, per_row=False, tr=256, rows_per_mod=seq)
            h_s = _norm_call(res_s, norm_g[layer, 0], sc1s, sh1s, per_row=True, tr=SAMPLE_ROWS)

        if layer % 2 == 0:
            z_p, z_s = _dense_wide_call([h_p], h_s, w_in0, li, vmem=WIDE_VMEM_LIMIT)
            ya_p, kv_p = _attn_prompt_call(z_p, qk_g[li, 0], qk_g[li, 1], slopes8.reshape(A_HEADS, 1, HEAD_DIM),
                                           batch=batch, seq=seq)
            for g in range(N_DIL):
                new_kv_p[g].append(kv_p[g])
            yb_p, hl_p = _lru_prompt_call(z_p, b_conv_w[li], b_conv_b[li], b_wa[li], b_ba[li], b_wi[li],
                                          b_bi[li], b_lambda[li], batch=batch, seq=seq)
            bh_p.append(hl_p[:, SUBLANES - 1])
            bconv_p.append(z_p.reshape(batch, seq, -1)[:, seq - (B_CONV - 1):, A_QKV:A_QKV + width])

            ya_s, kvnew_s = _attn_sample_call(z_s, [c[li] for c in caches], qk_g[li, 0], qk_g[li, 1], slopes8,
                                              dec_batch=dec_batch)
            for g in range(N_DIL):
                new_kv_s[g].append(_cache_roll_call(caches[g][li], kvnew_s[:, g][:, None]))
            xb_s = z_s[:, A_QKV:A_QKV + width]
            gb_s = z_s[:, A_QKV + width:]
            buf = jnp.transpose(pad_rows(state_b_conv[li]), (1, 0, 2))
            yb_s, hnew_s = _lru_sample_call(xb_s, gb_s, buf, pad_rows(state_b_h[li]), b_conv_w[li], b_conv_b[li],
                                            b_wa[li], b_ba[li], b_wi[li], b_bi[li], b_lambda[li])
            bh_s.append(hnew_s[:dec_batch])
            bconv_s.append(jnp.concatenate([state_b_conv[li][:, 1:], xb_s[:dec_batch, None]], axis=1))

            mix_p = [ya_p, yb_p]
            mix_s = jnp.concatenate([pad_rows(ya_s.reshape(dec_batch, A_WIDTH)).astype(BF16), yb_s], axis=1)
            w_out = w_out0
        else:
            buf = jnp.transpose(pad_rows(state_c_conv[li]), (1, 0, 2))
            m_p, ut_p, mix_s, u_s = _proj_gconv_call(h_p, h_s, w_in1, li, c_conv_w[li], buf, batch=batch, seq=seq)
            mix_p = [m_p]
            cconv_p.append(ut_p[:, SUBLANES - (C_CONV - 1):])
            cconv_s.append(jnp.concatenate([state_c_conv[li][:, 1:], u_s[:dec_batch, None]], axis=1))
            w_out = w_out1

        res_p, res_s = _dense_wide_call(mix_p, mix_s, w_out, li, res=(res_p, g1p, res_s, g1s), rows_per_mod=seq,
                                        tm=512)

        hp, ep, wp, cnt = _norm_route_call(res_p, norm_g[layer, 1], sc2p, sh2p, w_router_t, b_router,
                                           jnp.zeros((N_EXPERTS, LANES), F32),
                                           per_row=False, tr=256, rows_per_mod=seq)
        hs, es, ws, cnt = _norm_route_call(res_s, norm_g[layer, 1], sc2s, sh2s, w_router_t, b_router, cnt,
                                           per_row=True, tr=SAMPLE_ROWS)
        y, row_start = _moe(hp, hs, ep, es, cnt[:, 0].astype(I32), w_gate, w_up, w_down, layer)
        if layer + 1 < depth:
            (nsh_p, nsc_p, *_), (nsh_s, nsc_s, *_) = mod_chunks(layer + 1)
            res_p, h_p = _combine_call(ep, row_start, res_p, g2p, wp, y, per_row=False, tr=256, rows_per_mod=seq,
                                       next_norm=(norm_g[layer + 1, 0], nsc_p, nsh_p))
            res_s, h_s = _combine_call(es, row_start, res_s, g2s, ws, y, per_row=True, tr=SAMPLE_ROWS,
                                       next_norm=(norm_g[layer + 1, 0], nsc_s, nsh_s))
        else:
            res_p = _combine_call(ep, row_start, res_p, g2p, wp, y, per_row=False, tr=256, rows_per_mod=seq)
            res_s = _combine_call(es, row_start, res_s, g2s, ws, y, per_row=True, tr=SAMPLE_ROWS)

    y_prompt = res_p.reshape(batch, seq, d)
    y_sample = res_s[:dec_batch].reshape(dec_batch, 1, d)
    st = lambda xs: jnp.stack(xs, axis=0)
    return (y_prompt, y_sample, st(new_kv_p[0]), st(new_kv_p[1]), st(new_kv_p[2]), st(bh_p), st(bconv_p),
            st(cconv_p), st(new_kv_s[0]), st(new_kv_s[1]), st(new_kv_s[2]), st(bh_s), st(bconv_s), st(cconv_s))
```
